```python
import jax, jax.numpy as jnp
from jax import lax
import numpy as np

D_MODEL = 2048
BATCH = 4
SEQ = 2048
DEPTH = 1
DEC_BATCH = 128
DEC_SEQ = 8
PAST_LEN = 2048
PAGE_SIZE = 128

H_RET = 8
DK_RET = 128
DV_RET = 256
RET_CHUNK = 128
H_SB = 8
D_SB = 128
SB_BLOCK = 128
SB_BIAS_INIT = -8.0
ROPE_BASE = 10000.0
EPS_RMS = 1e-6
EPS_GN = 1e-5
W_RET_QK = H_RET * DK_RET
W_RET_V = H_RET * DV_RET
W_SB = H_SB * D_SB
SPLITS = (W_RET_QK, W_RET_QK, W_RET_V, W_RET_V, W_SB, W_SB, W_SB, W_SB, D_MODEL, D_MODEL)
IN_COLS = 2 * W_RET_QK + 2 * W_RET_V + 4 * W_SB + 2 * D_MODEL

kernel_name = 'retention_stickbreaking_gated_hybrid_step'


def rmsnorm(x, g):
    xf = x.astype(jnp.float32)
    r = lax.rsqrt(jnp.mean(xf * xf, axis=-1, keepdims=True) + EPS_RMS)
    return (xf * r * g.astype(jnp.float32)).astype(x.dtype)


def rope(x, pos):
    half = x.shape[-1] // 2
    inv = ROPE_BASE ** (-jnp.arange(half, dtype=jnp.float32) / half)
    ang = pos.astype(jnp.float32)[:, None] * inv[None, :]
    cos = jnp.cos(ang)[None, :, None, :]
    sin = jnp.sin(ang)[None, :, None, :]
    x1 = x[..., :half].astype(jnp.float32)
    x2 = x[..., half:].astype(jnp.float32)
    return jnp.concatenate([x1 * cos - x2 * sin, x2 * cos + x1 * sin], axis=-1)


def ret_log_gamma():
    return jnp.log1p(-jnp.exp2(-5.0 - jnp.arange(H_RET, dtype=jnp.float32)))


def retention_chunk(S, q, k, v):
    lg = ret_log_gamma()
    C = q.shape[1]
    S = S.astype(jnp.float32)
    q = q.astype(jnp.float32)
    k = k.astype(jnp.float32)
    v = v.astype(jnp.float32)
    idx = jnp.arange(C, dtype=jnp.float32)
    diff = idx[:, None] - idx[None, :]
    decay = jnp.where(diff >= 0, jnp.exp(lg[:, None, None] * jnp.maximum(diff, 0.0)[None]), 0.0)
    inner = jnp.einsum('bihd,bjhd->bhij', q, k) * decay[None]
    o_inner = jnp.einsum('bhij,bjhe->bihe', inner, v)
    cross_decay = jnp.exp(lg[:, None] * (idx + 1.0)[None, :]).T
    o_cross = jnp.einsum('bihd,bhde->bihe', q, S) * cross_decay[None, :, :, None]
    k_decay = jnp.exp(lg[:, None] * (C - 1.0 - idx)[None, :]).T
    S_new = jnp.exp(lg * C)[None, :, None, None] * S + jnp.einsum('bjhd,bjhe->bhde', k * k_decay[None, :, :, None], v)
    return S_new, o_inner + o_cross


def stick_breaking(q, k, v, q_pos, k_pos, bias):
    z = jnp.einsum('bqhd,bkhd->bhqk', q.astype(jnp.float32), k.astype(jnp.float32)) * (D_SB ** -0.5)
    z = z + bias.astype(jnp.float32)[None, :, None, None]
    mask = (k_pos[None, :] < q_pos[:, None])[None, None]
    log_keep = jnp.where(mask, jax.nn.log_sigmoid(-z), 0.0)
    after = lax.cumsum(log_keep, axis=3, reverse=True) - log_keep
    a = jnp.where(mask, jnp.exp(jax.nn.log_sigmoid(z) + after), 0.0)
    return jnp.einsum('bhqk,bkhd->bqhd', a, v.astype(jnp.float32))


def project(xn, w_in):
    cuts = [int(c) for c in np.cumsum(SPLITS)[:-1]]
    return jnp.split(xn @ w_in, cuts, axis=-1)


def ret_qkv(rq, rk, rv, pos):
    B, T = rq.shape[:2]
    q = rope(rq.reshape(B, T, H_RET, DK_RET), pos)
    k = rope(rk.reshape(B, T, H_RET, DK_RET), pos) * (DK_RET ** -0.5)
    v = rv.reshape(B, T, H_RET, DV_RET)
    return q, k, v


def merge_out(x, o_ret, g_ret, o_sb, g_sb, gate_r, gate_s, ret_gn_gain, w_ret_proj, w_sb_proj, w_out):
    B, T = x.shape[:2]
    o = o_ret.astype(jnp.float32)
    mu = jnp.mean(o, axis=-1, keepdims=True)
    var = jnp.mean(jnp.square(o - mu), axis=-1, keepdims=True)
    o = ((o - mu) * lax.rsqrt(var + EPS_GN)).reshape(B, T, W_RET_V) * ret_gn_gain.astype(jnp.float32)
    ret_branch = (o * jax.nn.silu(g_ret.astype(jnp.float32))).astype(x.dtype)
    sb_branch = (o_sb.reshape(B, T, W_SB) * jax.nn.silu(g_sb.astype(jnp.float32))).astype(x.dtype)
    merged = jax.nn.sigmoid(gate_r) * (ret_branch @ w_ret_proj) + jax.nn.sigmoid(gate_s) * (sb_branch @ w_sb_proj)
    return x + merged @ w_out


def prompt_layer(x, norm_gain, w_in, sb_bias, ret_gn_gain, w_ret_proj, w_sb_proj, w_out):
    B, T = x.shape[:2]
    xn = rmsnorm(x, norm_gain)
    rq, rk, rv, rg, sq, sk, sv, sg, gr, gs = project(xn, w_in)
    pos = jnp.arange(T, dtype=jnp.int32)
    q, k, v = ret_qkv(rq, rk, rv, pos)
    nc = T // RET_CHUNK
    to_chunks = lambda a: a.reshape(B, nc, RET_CHUNK, *a.shape[2:]).swapaxes(0, 1)
    S0 = jnp.zeros((B, H_RET, DK_RET, DV_RET), jnp.float32)
    S_fin, o = lax.scan(lambda S, xs: retention_chunk(S, xs[0], xs[1], xs[2]), S0,
                        (to_chunks(q), to_chunks(k), to_chunks(v)))
    o_ret = o.swapaxes(0, 1).reshape(B, T, H_RET, DV_RET)
    sq = sq.reshape(B, T, H_SB, D_SB)
    sk = sk.reshape(B, T, H_SB, D_SB)
    sv = sv.reshape(B, T, H_SB, D_SB)
    nb = T // SB_BLOCK
    qb = sq.reshape(B, nb, SB_BLOCK, H_SB, D_SB).swapaxes(0, 1)
    posb = pos.reshape(nb, SB_BLOCK)
    o_sb = lax.map(lambda a: stick_breaking(a[0], sk, sv, a[1], pos, sb_bias), (qb, posb))
    o_sb = o_sb.swapaxes(0, 1).reshape(B, T, H_SB, D_SB)
    h = merge_out(x, o_ret, rg, o_sb, sg, gr, gs, ret_gn_gain, w_ret_proj, w_sb_proj, w_out)
    return h, S_fin, sk, sv


def sample_layer(x, S_past, pool_k, pool_v, page_table, norm_gain, w_in, sb_bias, ret_gn_gain, w_ret_proj, w_sb_proj, w_out):
    B, T = x.shape[:2]
    past_len = page_table.shape[1] * pool_k.shape[1]
    xn = rmsnorm(x, norm_gain)
    rq, rk, rv, rg, sq, sk, sv, sg, gr, gs = project(xn, w_in)
    pos = past_len + jnp.arange(T, dtype=jnp.int32)
    q, k, v = ret_qkv(rq, rk, rv, pos)
    S_new, o_ret = retention_chunk(S_past, q, k, v)
    sq = sq.reshape(B, T, H_SB, D_SB)
    sk = sk.reshape(B, T, H_SB, D_SB)
    sv = sv.reshape(B, T, H_SB, D_SB)
    past_k = pool_k[page_table].reshape(B, past_len, H_SB, D_SB)
    past_v = pool_v[page_table].reshape(B, past_len, H_SB, D_SB)
    keys = jnp.concatenate([past_k, sk.astype(past_k.dtype)], axis=1)
    vals = jnp.concatenate([past_v, sv.astype(past_v.dtype)], axis=1)
    k_pos = jnp.arange(past_len + T, dtype=jnp.int32)
    o_sb = stick_breaking(sq, keys, vals, pos, k_pos, sb_bias)
    h = merge_out(x, o_ret, rg, o_sb, sg, gr, gs, ret_gn_gain, w_ret_proj, w_sb_proj, w_out)
    return h, S_new, sk, sv


def setup_inputs(seed: int = 0) -> dict:
    key = jax.random.key(seed)
    ks = jax.random.split(key, 14)
    n_pages = PAST_LEN // PAGE_SIZE
    n_used = DEC_BATCH * n_pages
    n_pool = n_used + n_used // 4
    f32 = jnp.float32
    x_prompt = jax.random.normal(ks[0], (BATCH, SEQ, D_MODEL), f32)
    x_sample = jax.random.normal(ks[1], (DEC_BATCH, DEC_SEQ, D_MODEL), f32)
    state_ret = 0.5 * jax.random.normal(ks[2], (DEPTH, DEC_BATCH, H_RET, DK_RET, DV_RET), f32)
    cache_sb_k = jax.random.normal(ks[3], (DEPTH, n_pool, PAGE_SIZE, H_SB, D_SB), f32)
    cache_sb_v = jax.random.normal(ks[4], (DEPTH, n_pool, PAGE_SIZE, H_SB, D_SB), f32)
    page_table = jax.random.permutation(ks[5], n_pool)[:n_used].reshape(DEC_BATCH, n_pages).astype(jnp.int32)
    norm_gain = 1.0 + 0.01 * jax.random.normal(ks[6], (DEPTH, D_MODEL), f32)
    w_in = jax.random.normal(ks[7], (DEPTH, D_MODEL, IN_COLS), f32) * D_MODEL ** -0.5
    sb_bias = SB_BIAS_INIT + 0.1 * jax.random.normal(ks[13], (DEPTH, H_SB), f32)
    ret_gn_gain = 1.0 + 0.01 * jax.random.normal(ks[8], (DEPTH, W_RET_V), f32)
    w_ret_proj = jax.random.normal(ks[9], (DEPTH, W_RET_V, D_MODEL), f32) * W_RET_V ** -0.5
    w_sb_proj = jax.random.normal(ks[10], (DEPTH, W_SB, D_MODEL), f32) * W_SB ** -0.5
    w_out = jax.random.normal(ks[11], (DEPTH, D_MODEL, D_MODEL), f32) * D_MODEL ** -0.5
    final_norm_gain = 1.0 + 0.01 * jax.random.normal(ks[12], (D_MODEL,), f32)
    return {'x_prompt': x_prompt, 'x_sample': x_sample, 'state_ret': state_ret,
            'cache_sb_k': cache_sb_k, 'cache_sb_v': cache_sb_v, 'page_table': page_table,
            'norm_gain': norm_gain, 'w_in': w_in, 'sb_bias': sb_bias, 'ret_gn_gain': ret_gn_gain,
            'w_ret_proj': w_ret_proj, 'w_sb_proj': w_sb_proj, 'w_out': w_out,
            'final_norm_gain': final_norm_gain}


def reference(x_prompt, x_sample, state_ret, cache_sb_k, cache_sb_v, page_table,
              norm_gain, w_in, sb_bias, ret_gn_gain, w_ret_proj, w_sb_proj, w_out, final_norm_gain):
    hp = x_prompt
    hs = x_sample
    ret_p, k_p, v_p, ret_s, k_s, v_s = [], [], [], [], [], []
    for l in range(DEPTH):
        hp, S_p, sk_p, sv_p = prompt_layer(hp, norm_gain[l], w_in[l], sb_bias[l], ret_gn_gain[l],
                                           w_ret_proj[l], w_sb_proj[l], w_out[l])
        hs, S_s, sk_s, sv_s = sample_layer(hs, state_ret[l], cache_sb_k[l], cache_sb_v[l], page_table,
                                           norm_gain[l], w_in[l], sb_bias[l], ret_gn_gain[l],
                                           w_ret_proj[l], w_sb_proj[l], w_out[l])
        ret_p.append(S_p)
        k_p.append(sk_p)
        v_p.append(sv_p)
        ret_s.append(S_s)
        k_s.append(sk_s)
        v_s.append(sv_s)
    y_prompt = rmsnorm(hp, final_norm_gain)
    y_sample = rmsnorm(hs, final_norm_gain)
    return (y_prompt, y_sample, jnp.stack(ret_p), jnp.stack(k_p), jnp.stack(v_p),
            jnp.stack(ret_s), jnp.stack(k_s), jnp.stack(v_s))
```

```python
import functools

import jax
import jax.numpy as jnp
import numpy as np
from jax import lax
from jax.experimental import pallas as pl
from jax.experimental.pallas import tpu as pltpu

F32 = jnp.float32
BF16 = jnp.bfloat16

D_MODEL = 2048
H_RET = 8
DK_RET = 128
DV_RET = 256
RET_CHUNK = 128
H_SB = 8
D_SB = 128
ROPE_BASE = 10000.0
EPS_RMS = 1e-6
EPS_GN = 1e-5
W_QK = H_RET * DK_RET
W_V = H_RET * DV_RET
W_SB = H_SB * D_SB

PROJ_TN = 1024
P_COLS = 2 * W_QK + 2 * W_V + 2 * W_SB + 2 * D_MODEL
N_P_TILES = P_COLS // PROJ_TN
COL_Q, COL_K, COL_V, COL_G, COL_SQ, COL_SG, COL_GR, COL_GS = 0, 1024, 2048, 4096, 6144, 7168, 8192, 10240

VMEM_LIMIT = 56 * 1024 * 1024

_NT = (((1,), (1,)), ((), ()))


def _silu(g):
    return g * jax.nn.sigmoid(g)


def _proj_body(x_ref, gain_ref, cos_ref, sin_ref, w_ref, p_ref, sk_ref, sv_ref, skh_ref, svh_ref, xn_ref):
    j = pl.program_id(1)

    @pl.when(j == 0)
    def _norm():
        x = x_ref[...]
        r = lax.rsqrt(jnp.mean(x * x, axis=-1, keepdims=True) + EPS_RMS)
        xn_ref[...] = (x * r * gain_ref[...]).astype(BF16)

    acc = jnp.dot(xn_ref[...], w_ref[...], preferred_element_type=F32)

    @pl.when(j < 2)
    def _rope():
        scale = jnp.where(j == 1, F32(DK_RET**-0.5), F32(1.0))
        cos = cos_ref[...]
        sin = sin_ref[...]
        for h in range(H_RET):
            xh = acc[:, h * DK_RET : (h + 1) * DK_RET]
            rot = pltpu.roll(xh, DK_RET // 2, axis=1)
            p_ref[:, h * DK_RET : (h + 1) * DK_RET] = ((xh * cos + rot * sin) * scale).astype(p_ref.dtype)

    @pl.when((j >= 2) & (j < 8))
    def _plain():
        p_ref[...] = acc.astype(p_ref.dtype)

    @pl.when((j >= 8) & (j < N_P_TILES))
    def _gate():
        p_ref[...] = jax.nn.sigmoid(acc).astype(p_ref.dtype)

    def heads_out(tok_ref, head_ref):
        tm = acc.shape[0]
        for h in range(H_SB):
            ah = acc[:, h * D_SB : (h + 1) * D_SB]
            tok_ref[pl.ds(h, tm, stride=H_SB), :] = ah
            head_ref[h] = ah.astype(BF16)

    @pl.when(j == N_P_TILES)
    def _sk():
        heads_out(sk_ref, skh_ref)

    @pl.when(j == N_P_TILES + 1)
    def _sv():
        heads_out(sv_ref, svh_ref)


def _project(x, gain, cos, sin, w, *, tm, rope_period_tiles, p_dtype):
    m = x.shape[0]
    n_tiles = w.shape[1] // PROJ_TN
    return pl.pallas_call(
        _proj_body,
        grid=(m // tm, n_tiles),
        in_specs=[
            pl.BlockSpec((tm, D_MODEL), lambda i, j: (i, 0)),
            pl.BlockSpec((1, D_MODEL), lambda i, j: (0, 0)),
            pl.BlockSpec((tm, DK_RET), lambda i, j: (i % rope_period_tiles, 0)),
            pl.BlockSpec((tm, DK_RET), lambda i, j: (i % rope_period_tiles, 0)),
            pl.BlockSpec((D_MODEL, PROJ_TN), lambda i, j: (0, j)),
        ],
        out_specs=[
            pl.BlockSpec((tm, PROJ_TN), lambda i, j: (i, jnp.minimum(j, N_P_TILES - 1))),
            pl.BlockSpec((tm * H_SB, D_SB), lambda i, j: (i, 0)),
            pl.BlockSpec((tm * H_SB, D_SB), lambda i, j: (i, 0)),
            pl.BlockSpec((H_SB, tm, D_SB), lambda i, j: (0, i, 0)),
            pl.BlockSpec((H_SB, tm, D_SB), lambda i, j: (0, i, 0)),
        ],
        out_shape=[
            jax.ShapeDtypeStruct((m, P_COLS), p_dtype),
            jax.ShapeDtypeStruct((m * H_SB, D_SB), F32),
            jax.ShapeDtypeStruct((m * H_SB, D_SB), F32),
            jax.ShapeDtypeStruct((H_SB, m, D_SB), BF16),
            jax.ShapeDtypeStruct((H_SB, m, D_SB), BF16),
        ],
        scratch_shapes=[pltpu.VMEM((tm, D_MODEL), BF16)],
        compiler_params=pltpu.CompilerParams(
            dimension_semantics=("arbitrary", "arbitrary"), vmem_limit_bytes=VMEM_LIMIT
        ),
        name="proj",
    )(x, gain, cos, sin, w)


def _group_norm_gate(o, gain, g):
    mu = jnp.mean(o, axis=-1, keepdims=True)
    d = o - mu
    var = jnp.mean(d * d, axis=-1, keepdims=True)
    return d * lax.rsqrt(var + EPS_GN) * gain * _silu(g)


def _ret_prompt_body(q_ref, k_ref, v_ref, g_ref, gain_ref, decay_ref, cd_ref, kd_ref, gs_ref, out_ref, s_ref):
    c = pl.program_id(1)

    @pl.when(c == 0)
    def _init():
        s_ref[...] = jnp.zeros_like(s_ref)

    for h in range(H_RET):
        q = q_ref[:, h * DK_RET : (h + 1) * DK_RET]
        k = k_ref[:, h * DK_RET : (h + 1) * DK_RET]
        v = v_ref[:, h * DV_RET : (h + 1) * DV_RET]
        s = s_ref[0, h]
        inner = lax.dot_general(q, k, _NT, preferred_element_type=F32) * decay_ref[h]
        o = jnp.dot(inner.astype(BF16), v, preferred_element_type=F32)
        o = o + jnp.dot(q, s.astype(BF16), preferred_element_type=F32) * cd_ref[:, h : h + 1]
        kdec_t = (k.astype(F32) * kd_ref[:, h : h + 1]).T.astype(BF16)
        s_ref[0, h] = gs_ref[h] * s + jnp.dot(kdec_t, v, preferred_element_type=F32)
        g = g_ref[:, h * DV_RET : (h + 1) * DV_RET].astype(F32)
        gain = gain_ref[:, h * DV_RET : (h + 1) * DV_RET]
        out_ref[:, h * DV_RET : (h + 1) * DV_RET] = _group_norm_gate(o, gain, g).astype(out_ref.dtype)


def _ret_tables(chunk):
    lg = jnp.log1p(-jnp.exp2(-5.0 - jnp.arange(H_RET, dtype=F32)))
    idx = jnp.arange(chunk, dtype=F32)
    diff = idx[:, None] - idx[None, :]
    decay = jnp.where(diff >= 0, jnp.exp(lg[:, None, None] * jnp.maximum(diff, 0.0)[None]), 0.0)
    cross = jnp.exp(lg[:, None] * (idx + 1.0)[None, :]).T
    kdec = jnp.exp(lg[:, None] * (chunk - 1.0 - idx)[None, :]).T
    sdec = jnp.broadcast_to(jnp.exp(lg * chunk)[:, None, None], (H_RET, 1, DV_RET))
    return decay, cross, kdec, sdec


def _ret_prompt(p, gn_gain, batch, seq):
    nc = seq // RET_CHUNK
    decay, cross, kdec, sdec = _ret_tables(RET_CHUNK)
    row = lambda b, c: b * nc + c
    const2 = lambda b, c: (0, 0)
    const3 = lambda b, c: (0, 0, 0)
    return pl.pallas_call(
        _ret_prompt_body,
        grid=(batch, nc),
        in_specs=[
            pl.BlockSpec((RET_CHUNK, W_QK), lambda b, c: (row(b, c), COL_Q // W_QK)),
            pl.BlockSpec((RET_CHUNK, W_QK), lambda b, c: (row(b, c), COL_K // W_QK)),
            pl.BlockSpec((RET_CHUNK, W_V), lambda b, c: (row(b, c), COL_V // W_V)),
            pl.BlockSpec((RET_CHUNK, W_V), lambda b, c: (row(b, c), COL_G // W_V)),
            pl.BlockSpec((1, W_V), const2),
            pl.BlockSpec((H_RET, RET_CHUNK, RET_CHUNK), const3),
            pl.BlockSpec((RET_CHUNK, H_RET), const2),
            pl.BlockSpec((RET_CHUNK, H_RET), const2),
            pl.BlockSpec((H_RET, 1, DV_RET), const3),
        ],
        out_specs=[
            pl.BlockSpec((RET_CHUNK, W_V), lambda b, c: (row(b, c), 0)),
            pl.BlockSpec((1, H_RET, DK_RET, DV_RET), lambda b, c: (b, 0, 0, 0)),
        ],
        out_shape=[
            jax.ShapeDtypeStruct((batch * seq, W_V), BF16),
            jax.ShapeDtypeStruct((batch, H_RET, DK_RET, DV_RET), F32),
        ],
        compiler_params=pltpu.CompilerParams(
            dimension_semantics=("arbitrary", "arbitrary"), vmem_limit_bytes=VMEM_LIMIT
        ),
        name="ret_prompt",
    )(p, p, p, p, gn_gain, decay, cross, kdec, sdec)


def _ret_sample_body(q_ref, k_ref, v_ref, g_ref, s_ref, gain_ref, decay_ref, cd_ref, kd_ref, gs_ref, out_ref, snew_ref):
    t_len = q_ref.shape[0]
    for h in range(H_RET):
        q = q_ref[:, h * DK_RET : (h + 1) * DK_RET]
        k = k_ref[:, h * DK_RET : (h + 1) * DK_RET]
        v = v_ref[:, h * DV_RET : (h + 1) * DV_RET]
        s = s_ref[0, h]
        dec = decay_ref[h]
        o = jnp.dot(q.astype(BF16), s.astype(BF16), preferred_element_type=F32) * cd_ref[:, h : h + 1]
        for j in range(t_len):
            w = jnp.sum(q * k[j : j + 1, :], axis=-1, keepdims=True) * dec[:, j : j + 1]
            o = o + w * v[j : j + 1, :]
        kdec_t = (k * kd_ref[:, h : h + 1]).T.astype(BF16)
        upd = jnp.dot(kdec_t, v.astype(BF16), preferred_element_type=F32)
        snew_ref[0, h] = gs_ref[h] * s + upd
        g = g_ref[:, h * DV_RET : (h + 1) * DV_RET]
        gain = gain_ref[:, h * DV_RET : (h + 1) * DV_RET]
        out_ref[:, h * DV_RET : (h + 1) * DV_RET] = _group_norm_gate(o, gain, g)


def _ret_sample(p, state, gn_gain, batch, t_len):
    decay, cross, kdec, sdec = _ret_tables(t_len)
    const2 = lambda b: (0, 0)
    const3 = lambda b: (0, 0, 0)
    return pl.pallas_call(
        _ret_sample_body,
        grid=(batch,),
        in_specs=[
            pl.BlockSpec((t_len, W_QK), lambda b: (b, COL_Q // W_QK)),
            pl.BlockSpec((t_len, W_QK), lambda b: (b, COL_K // W_QK)),
            pl.BlockSpec((t_len, W_V), lambda b: (b, COL_V // W_V)),
            pl.BlockSpec((t_len, W_V), lambda b: (b, COL_G // W_V)),
            pl.BlockSpec((1, H_RET, DK_RET, DV_RET), lambda b: (b, 0, 0, 0)),
            pl.BlockSpec((1, W_V), const2),
            pl.BlockSpec((H_RET, t_len, t_len), const3),
            pl.BlockSpec((t_len, H_RET), const2),
            pl.BlockSpec((t_len, H_RET), const2),
            pl.BlockSpec((H_RET, 1, DV_RET), const3),
        ],
        out_specs=[
            pl.BlockSpec((t_len, W_V), lambda b: (b, 0)),
            pl.BlockSpec((1, H_RET, DK_RET, DV_RET), lambda b: (b, 0, 0, 0)),
        ],
        out_shape=[
            jax.ShapeDtypeStruct((batch * t_len, W_V), F32),
            jax.ShapeDtypeStruct((batch, H_RET, DK_RET, DV_RET), F32),
        ],
        compiler_params=pltpu.CompilerParams(dimension_semantics=("arbitrary",), vmem_limit_bytes=VMEM_LIMIT),
        name="ret_sample",
    )(p, p, p, p, state, gn_gain, decay, cross, kdec, sdec)


def _sb_weights(qk, tri, bias, carry, mask):
    z = qk * F32(D_SB**-0.5) + bias
    sp = jnp.log1p(jnp.exp(-jnp.abs(z)))
    log_beta = jnp.minimum(z, 0.0) - sp
    log_keep = -jnp.maximum(z, 0.0) - sp
    if mask is not None:
        log_keep = jnp.where(mask, log_keep, 0.0)
    hi = log_keep.astype(BF16)
    lo = (log_keep - hi.astype(F32)).astype(BF16)
    after = jnp.dot(hi, tri, preferred_element_type=F32) + jnp.dot(lo, tri, preferred_element_type=F32)
    a = jnp.exp(log_beta + after + carry)
    if mask is not None:
        a = jnp.where(mask, a, 0.0)
    return a, carry + after[:, 0:1] + log_keep[:, 0:1]


def _tri(n):
    r = lax.broadcasted_iota(jnp.int32, (n, n), 0)
    c = lax.broadcasted_iota(jnp.int32, (n, n), 1)
    return (r > c).astype(BF16)


SBP_T = 256


def _sb_prompt_body(bias_ref, q_ref, k_ref, v_ref, sg_ref, out_ref):
    h = pl.program_id(1)
    i = pl.program_id(2)
    q = q_ref[...]
    bias = bias_ref[h]
    tri = _tri(SBP_T)
    r = lax.broadcasted_iota(jnp.int32, (SBP_T, SBP_T), 0)
    c = lax.broadcasted_iota(jnp.int32, (SBP_T, SBP_T), 1)

    def tile(kb, carry, acc, mask):
        off = pl.multiple_of(kb * SBP_T, SBP_T)
        qk = lax.dot_general(q, k_ref[pl.ds(off, SBP_T), :], _NT, preferred_element_type=F32)
        a, carry = _sb_weights(qk, tri, bias, carry, mask)
        o = jnp.dot(a.astype(BF16), v_ref[pl.ds(off, SBP_T), :], preferred_element_type=F32)
        return carry, acc + o

    carry, acc = tile(i, jnp.zeros((SBP_T, 1), F32), jnp.zeros((SBP_T, D_SB), F32), c < r)
    carry, acc = lax.fori_loop(0, i, lambda n, st: tile(i - 1 - n, st[0], st[1], None), (carry, acc))
    out_ref[...] = (acc * _silu(sg_ref[...].astype(F32))).astype(out_ref.dtype)


def _sb_prompt(p, sk_heads, sv_heads, sb_bias, batch, seq):
    nq = seq // SBP_T
    return pl.pallas_call(
        _sb_prompt_body,
        grid=(batch, H_SB, nq),
        in_specs=[
            pl.BlockSpec(memory_space=pltpu.SMEM),
            pl.BlockSpec((SBP_T, D_SB), lambda b, h, i: (b * nq + i, COL_SQ // D_SB + h)),
            pl.BlockSpec((None, seq, D_SB), lambda b, h, i: (h, b, 0)),
            pl.BlockSpec((None, seq, D_SB), lambda b, h, i: (h, b, 0)),
            pl.BlockSpec((SBP_T, D_SB), lambda b, h, i: (b * nq + i, COL_SG // D_SB + h)),
        ],
        out_specs=pl.BlockSpec((SBP_T, D_SB), lambda b, h, i: (b * nq + i, h)),
        out_shape=jax.ShapeDtypeStruct((batch * seq, W_SB), BF16),
        compiler_params=pltpu.CompilerParams(
            dimension_semantics=("arbitrary", "arbitrary", "arbitrary"), vmem_limit_bytes=VMEM_LIMIT
        ),
        name="sb_prompt",
    )(sb_bias, p, sk_heads, sv_heads, p)


SBS_PAGES = 4


def _sb_sample_body(pt_ref, bias_ref, q_ref, kn_ref, vn_ref, sg_ref, *rest):
    k_refs = rest[:SBS_PAGES]
    v_refs = rest[SBS_PAGES : 2 * SBS_PAGES]
    out_ref, acc_ref, carry_ref, kpad_ref, vpad_ref = rest[2 * SBS_PAGES :]
    g = pl.program_id(1)
    t_len = q_ref.shape[0]
    rows = H_SB * t_len
    page = k_refs[0].shape[0] // H_SB
    tri = _tri(page)
    q = q_ref[...]
    q_heads = [q[:, h * D_SB : (h + 1) * D_SB].astype(BF16) for h in range(H_SB)]

    def head_rows(ref, h):
        return ref[pl.ds(h, page, stride=H_SB), :].astype(BF16)

    def block(k_ref, v_ref, mask):
        qk = jnp.concatenate(
            [lax.dot_general(q_heads[h], head_rows(k_ref, h), _NT, preferred_element_type=F32) for h in range(H_SB)],
            axis=0,
        )
        a, carry = _sb_weights(qk, tri, bias_ref[...], carry_ref[...], mask)
        carry_ref[...] = carry
        acc_ref[...] += jnp.concatenate(
            [
                jnp.dot(a[h * t_len : (h + 1) * t_len].astype(BF16), head_rows(v_ref, h), preferred_element_type=F32)
                for h in range(H_SB)
            ],
            axis=0,
        )

    @pl.when(g == 0)
    def _new_tokens():
        acc_ref[...] = jnp.zeros_like(acc_ref)
        carry_ref[...] = jnp.zeros_like(carry_ref)
        kpad_ref[...] = jnp.zeros_like(kpad_ref)
        vpad_ref[...] = jnp.zeros_like(vpad_ref)
        kpad_ref[0 : rows, :] = kn_ref[...]
        vpad_ref[0 : rows, :] = vn_ref[...]
        r = lax.broadcasted_iota(jnp.int32, (rows, page), 0)
        c = lax.broadcasted_iota(jnp.int32, (rows, page), 1)
        block(kpad_ref, vpad_ref, c < (r % t_len))

    for pg in reversed(range(SBS_PAGES)):
        block(k_refs[pg], v_refs[pg], None)

    @pl.when(g == pl.num_programs(1) - 1)
    def _finish():
        acc = acc_ref[...]
        o = jnp.concatenate([acc[h * t_len : (h + 1) * t_len, :] for h in range(H_SB)], axis=1)
        out_ref[...] = o * _silu(sg_ref[...])


def _sb_sample(page_table, p, sk, sv, pool_k, pool_v, sb_bias, batch, t_len):
    n_pages = page_table.shape[1]
    n_groups = n_pages // SBS_PAGES
    page_rows = pool_k.shape[1]
    rows = H_SB * t_len
    bias_rows = jnp.repeat(sb_bias, t_len)[:, None]

    def page_spec(pg):
        def index(b, g, pt):
            return (pt[b * n_pages + (n_groups - 1 - g) * SBS_PAGES + pg], 0, 0)

        return pl.BlockSpec((None, page_rows, D_SB), index)

    grid_spec = pltpu.PrefetchScalarGridSpec(
        num_scalar_prefetch=1,
        grid=(batch, n_groups),
        in_specs=[
            pl.BlockSpec((rows, 1), lambda b, g, pt: (0, 0)),
            pl.BlockSpec((t_len, W_SB), lambda b, g, pt: (b, COL_SQ // W_SB)),
            pl.BlockSpec((rows, D_SB), lambda b, g, pt: (b, 0)),
            pl.BlockSpec((rows, D_SB), lambda b, g, pt: (b, 0)),
            pl.BlockSpec((t_len, W_SB), lambda b, g, pt: (b, COL_SG // W_SB)),
        ]
        + [page_spec(pg) for pg in range(SBS_PAGES)] * 2,
        out_specs=pl.BlockSpec((t_len, W_SB), lambda b, g, pt: (b, 0)),
        scratch_shapes=[
            pltpu.VMEM((rows, D_SB), F32),
            pltpu.VMEM((rows, 1), F32),
            pltpu.VMEM((page_rows, D_SB), F32),
            pltpu.VMEM((page_rows, D_SB), F32),
        ],
    )
    return pl.pallas_call(
        _sb_sample_body,
        grid_spec=grid_spec,
        out_shape=jax.ShapeDtypeStruct((batch * t_len, W_SB), F32),
        compiler_params=pltpu.CompilerParams(
            dimension_semantics=("arbitrary", "arbitrary"), vmem_limit_bytes=VMEM_LIMIT
        ),
        name="sb_sample",
    )(page_table.reshape(-1), bias_rows, p, sk, sv, p, *([pool_k] * SBS_PAGES), *([pool_v] * SBS_PAGES))


def _merge_body(x_ref, rb_ref, sb_ref, gr_ref, gs_ref, wr_ref, ws_ref, wo_ref, fg_ref, y_ref):
    m = gr_ref[...].astype(F32) * jnp.dot(rb_ref[...].astype(BF16), wr_ref[...], preferred_element_type=F32)
    m = m + gs_ref[...].astype(F32) * jnp.dot(sb_ref[...].astype(BF16), ws_ref[...], preferred_element_type=F32)
    hid = x_ref[...] + jnp.dot(m.astype(BF16), wo_ref[...], preferred_element_type=F32)
    r = lax.rsqrt(jnp.mean(hid * hid, axis=-1, keepdims=True) + EPS_RMS)
    y_ref[...] = hid * r * fg_ref[...]


def _merge(x, ret_branch, sb_branch, p, w_ret, w_sb, w_out, final_gain, *, tm):
    m = x.shape[0]
    resident = functools.partial(pl.BlockSpec, index_map=lambda i: (0, 0), pipeline_mode=pl.Buffered(1))
    return pl.pallas_call(
        _merge_body,
        grid=(m // tm,),
        in_specs=[
            pl.BlockSpec((tm, D_MODEL), lambda i: (i, 0)),
            pl.BlockSpec((tm, W_V), lambda i: (i, 0)),
            pl.BlockSpec((tm, W_SB), lambda i: (i, 0)),
            pl.BlockSpec((tm, D_MODEL), lambda i: (i, COL_GR // D_MODEL)),
            pl.BlockSpec((tm, D_MODEL), lambda i: (i, COL_GS // D_MODEL)),
            resident((W_V, D_MODEL)),
            resident((W_SB, D_MODEL)),
            resident((D_MODEL, D_MODEL)),
            pl.BlockSpec((1, D_MODEL), lambda i: (0, 0)),
        ],
        out_specs=pl.BlockSpec((tm, D_MODEL), lambda i: (i, 0)),
        out_shape=jax.ShapeDtypeStruct((m, D_MODEL), F32),
        compiler_params=pltpu.CompilerParams(dimension_semantics=("arbitrary",), vmem_limit_bytes=VMEM_LIMIT),
        name="merge",
    )(x, ret_branch, sb_branch, p, p, w_ret, w_sb, w_out, final_gain)


def _rope_tables(pos):
    half = DK_RET // 2
    inv = ROPE_BASE ** (-jnp.arange(half, dtype=F32) / half)
    ang = pos.astype(F32)[:, None] * inv[None, :]
    cos, sin = jnp.cos(ang), jnp.sin(ang)
    return jnp.concatenate([cos, cos], axis=-1), jnp.concatenate([-sin, sin], axis=-1)


def _layer(xp, xs, state, pool_k, pool_v, page_table, norm_gain, w_in, sb_bias, gn_gain, w_ret, w_sb, w_out, final_gain):
    batch, seq, _ = xp.shape
    dec_batch, t_len, _ = xs.shape
    n_pool, page = pool_k.shape[:2]
    past_len = page_table.shape[1] * page

    c_sk = 2 * W_QK + 2 * W_V + W_SB
    c_sg = c_sk + 2 * W_SB
    w_perm = jnp.concatenate([w_in[:, :c_sk], w_in[:, c_sg:], w_in[:, c_sk:c_sg]], axis=1).astype(BF16)
    w_ret_b, w_sb_b, w_out_b = w_ret.astype(BF16), w_sb.astype(BF16), w_out.astype(BF16)
    gain2 = norm_gain[None, :]
    gn_gain2 = gn_gain[None, :]
    final_gain2 = final_gain[None, :]

    x2p = xp.reshape(batch * seq, D_MODEL)
    x2s = xs.reshape(dec_batch * t_len, D_MODEL)
    tm_p, tm_s = 512, 512
    cos_p, sin_p = _rope_tables(jnp.arange(seq, dtype=jnp.int32))
    cos_s, sin_s = _rope_tables(past_len + jnp.arange(t_len, dtype=jnp.int32))
    cos_s, sin_s = jnp.tile(cos_s, (tm_s // t_len, 1)), jnp.tile(sin_s, (tm_s // t_len, 1))

    pp, skp, svp, skp_heads, svp_heads = _project(
        x2p, gain2, cos_p, sin_p, w_perm, tm=tm_p, rope_period_tiles=seq // tm_p, p_dtype=BF16
    )
    ps, sks, svs, _, _ = _project(x2s, gain2, cos_s, sin_s, w_perm, tm=tm_s, rope_period_tiles=1, p_dtype=F32)

    rbp, state_p = _ret_prompt(pp, gn_gain2, batch, seq)
    rbs, state_s = _ret_sample(ps, state, gn_gain2, dec_batch, t_len)

    sbp = _sb_prompt(pp, skp_heads, svp_heads, sb_bias, batch, seq)
    sbs = _sb_sample(
        page_table, ps, sks, svs, pool_k.reshape(n_pool, page * H_SB, D_SB), pool_v.reshape(n_pool, page * H_SB, D_SB),
        sb_bias, dec_batch, t_len,
    )

    yp = _merge(x2p, rbp, sbp, pp, w_ret_b, w_sb_b, w_out_b, final_gain2, tm=256)
    ys = _merge(x2s, rbs, sbs, ps, w_ret_b, w_sb_b, w_out_b, final_gain2, tm=256)

    return (
        yp.reshape(batch, seq, D_MODEL),
        ys.reshape(dec_batch, t_len, D_MODEL),
        state_p,
        skp.reshape(batch, seq, H_SB, D_SB),
        svp.reshape(batch, seq, H_SB, D_SB),
        state_s,
        sks.reshape(dec_batch, t_len, H_SB, D_SB),
        svs.reshape(dec_batch, t_len, H_SB, D_SB),
    )


def kernel(x_prompt, x_sample, state_ret, cache_sb_k, cache_sb_v, page_table, norm_gain, w_in, sb_bias, ret_gn_gain, w_ret_proj, w_sb_proj, w_out, final_norm_gain):
    depth = w_in.shape[0]
    assert depth == 1, "single-layer trunk"
    outs = _layer(
        x_prompt, x_sample, state_ret[0], cache_sb_k[0], cache_sb_v[0], page_table, norm_gain[0], w_in[0],
        sb_bias[0], ret_gn_gain[0], w_ret_proj[0], w_sb_proj[0], w_out[0], final_norm_gain,
    )
    yp, ys, state_p, skp, svp, state_s, sks, svs = outs
    return (yp, ys, state_p[None], skp[None], svp[None], state_s[None], sks[None], svs[None])
```

```python
import functools

import jax
import jax.numpy as jnp
import numpy as np
from jax import lax
from jax.experimental import pallas as pl
from jax.experimental.pallas import tpu as pltpu

F32 = jnp.float32
BF16 = jnp.bfloat16

D_MODEL = 2048
H_RET = 8
DK_RET = 128
DV_RET = 256
RET_CHUNK = 128
H_SB = 8
D_SB = 128
ROPE_BASE = 10000.0
EPS_RMS = 1e-6
EPS_GN = 1e-5
W_QK = H_RET * DK_RET
W_V = H_RET * DV_RET
W_SB = H_SB * D_SB

PROJ_TN = 1024
P_COLS = 2 * W_QK + 2 * W_V + 2 * W_SB + 2 * D_MODEL
N_P_TILES = P_COLS // PROJ_TN
COL_Q, COL_K, COL_V, COL_G, COL_SQ, COL_SG, COL_GR, COL_GS = 0, 1024, 2048, 4096, 6144, 7168, 8192, 10240

LOG2E = float(np.log2(np.e))
SB_Q_SCALE = D_SB**-0.5 * LOG2E

VMEM_LIMIT = 56 * 1024 * 1024

_NT = (((1,), (1,)), ((), ()))


def _silu(g):
    return g * jax.nn.sigmoid(g)


def _proj_body(x_ref, gain_ref, cos_ref, sin_ref, w_ref, p_ref, sk_ref, sv_ref, skh_ref, svh_ref, xn_ref):
    j = pl.program_id(1)

    @pl.when(j == 0)
    def _norm():
        x = x_ref[...]
        r = lax.rsqrt(jnp.mean(x * x, axis=-1, keepdims=True) + EPS_RMS)
        xn_ref[...] = (x * r * gain_ref[...]).astype(BF16)

    acc = jnp.dot(xn_ref[...], w_ref[...], preferred_element_type=F32)

    @pl.when(j < 2)
    def _rope():
        scale = jnp.where(j == 1, F32(DK_RET**-0.5), F32(1.0))
        cos = cos_ref[...]
        sin = sin_ref[...]
        for h in range(H_RET):
            xh = acc[:, h * DK_RET : (h + 1) * DK_RET]
            rot = pltpu.roll(xh, DK_RET // 2, axis=1)
            p_ref[:, h * DK_RET : (h + 1) * DK_RET] = ((xh * cos + rot * sin) * scale).astype(p_ref.dtype)

    @pl.when((j >= 2) & (j < 8))
    def _plain():
        scale = jnp.where(j == COL_SQ // PROJ_TN, F32(SB_Q_SCALE), F32(1.0))
        p_ref[...] = (acc * scale).astype(p_ref.dtype)

    @pl.when((j >= 8) & (j < N_P_TILES))
    def _gate():
        p_ref[...] = (0.5 * jnp.tanh(0.5 * acc) + 0.5).astype(p_ref.dtype)

    def heads_out(tok_ref, head_ref):
        tm = acc.shape[0]
        for h in range(H_SB):
            ah = acc[:, h * D_SB : (h + 1) * D_SB]
            tok_ref[pl.ds(h, tm, stride=H_SB), :] = ah
            head_ref[h] = ah.astype(BF16)

    @pl.when(j == N_P_TILES)
    def _sk():
        heads_out(sk_ref, skh_ref)

    @pl.when(j == N_P_TILES + 1)
    def _sv():
        heads_out(sv_ref, svh_ref)


def _w_tile(j):
    first_kv = COL_SQ // PROJ_TN + 1
    n_kv = 2 * W_SB // PROJ_TN
    return jnp.where(j < first_kv, j, jnp.where(j < N_P_TILES, j + n_kv, j - (N_P_TILES - first_kv)))


def _project(x, gain, cos, sin, w, *, tm, rope_period_tiles, p_dtype):
    m = x.shape[0]
    n_tiles = w.shape[1] // PROJ_TN
    return pl.pallas_call(
        _proj_body,
        grid=(m // tm, n_tiles),
        in_specs=[
            pl.BlockSpec((tm, D_MODEL), lambda i, j: (i, 0)),
            pl.BlockSpec((1, D_MODEL), lambda i, j: (0, 0)),
            pl.BlockSpec((tm, DK_RET), lambda i, j: (i % rope_period_tiles, 0)),
            pl.BlockSpec((tm, DK_RET), lambda i, j: (i % rope_period_tiles, 0)),
            pl.BlockSpec((D_MODEL, PROJ_TN), lambda i, j: (0, _w_tile(j))),
        ],
        out_specs=[
            pl.BlockSpec((tm, PROJ_TN), lambda i, j: (i, jnp.minimum(j, N_P_TILES - 1))),
            pl.BlockSpec((tm * H_SB, D_SB), lambda i, j: (i, 0)),
            pl.BlockSpec((tm * H_SB, D_SB), lambda i, j: (i, 0)),
            pl.BlockSpec((H_SB, tm, D_SB), lambda i, j: (0, i, 0)),
            pl.BlockSpec((H_SB, tm, D_SB), lambda i, j: (0, i, 0)),
        ],
        out_shape=[
            jax.ShapeDtypeStruct((m, P_COLS), p_dtype),
            jax.ShapeDtypeStruct((m * H_SB, D_SB), F32),
            jax.ShapeDtypeStruct((m * H_SB, D_SB), F32),
            jax.ShapeDtypeStruct((H_SB, m, D_SB), BF16),
            jax.ShapeDtypeStruct((H_SB, m, D_SB), BF16),
        ],
        scratch_shapes=[pltpu.VMEM((tm, D_MODEL), BF16)],
        compiler_params=pltpu.CompilerParams(
            dimension_semantics=("arbitrary", "arbitrary"), vmem_limit_bytes=VMEM_LIMIT
        ),
        name="proj",
    )(x, gain, cos, sin, w)


def _group_norm_gate(o, gain, g):
    mu = jnp.mean(o, axis=-1, keepdims=True)
    d = o - mu
    var = jnp.mean(d * d, axis=-1, keepdims=True)
    return d * lax.rsqrt(var + EPS_GN) * gain * _silu(g)


def _ret_prompt_body(q_ref, k_ref, v_ref, g_ref, gain_ref, decay_ref, cd_ref, kd_ref, gs_ref, out_ref, s_ref):
    c = pl.program_id(1)

    @pl.when(c == 0)
    def _init():
        s_ref[...] = jnp.zeros_like(s_ref)

    for h in range(H_RET):
        q = q_ref[:, h * DK_RET : (h + 1) * DK_RET]
        k = k_ref[:, h * DK_RET : (h + 1) * DK_RET]
        v = v_ref[:, h * DV_RET : (h + 1) * DV_RET]
        s = s_ref[0, h]
        inner = lax.dot_general(q, k, _NT, preferred_element_type=F32) * decay_ref[h]
        o = jnp.dot(inner.astype(BF16), v, preferred_element_type=F32)
        o = o + jnp.dot(q, s.astype(BF16), preferred_element_type=F32) * cd_ref[:, h : h + 1]
        kdec_t = (k.astype(F32) * kd_ref[:, h : h + 1]).T.astype(BF16)
        s_ref[0, h] = gs_ref[h] * s + jnp.dot(kdec_t, v, preferred_element_type=F32)
        g = g_ref[:, h * DV_RET : (h + 1) * DV_RET].astype(F32)
        gain = gain_ref[:, h * DV_RET : (h + 1) * DV_RET]
        out_ref[:, h * DV_RET : (h + 1) * DV_RET] = _group_norm_gate(o, gain, g).astype(out_ref.dtype)


def _ret_tables(chunk):
    lg = np.log1p(-np.exp2(-5.0 - np.arange(H_RET, dtype=np.float64)))
    idx = np.arange(chunk, dtype=np.float64)
    diff = idx[:, None] - idx[None, :]
    decay = np.where(diff >= 0, np.exp(lg[:, None, None] * np.maximum(diff, 0.0)[None]), 0.0)
    cross = np.exp(lg[:, None] * (idx + 1.0)[None, :]).T
    kdec = np.exp(lg[:, None] * (chunk - 1.0 - idx)[None, :]).T
    sdec = np.broadcast_to(np.exp(lg * chunk)[:, None, None], (H_RET, 1, DV_RET))
    return tuple(np.asarray(a, np.float32) for a in (decay, cross, kdec, sdec))


def _ret_prompt(p, gn_gain, batch, seq):
    nc = seq // RET_CHUNK
    decay, cross, kdec, sdec = _ret_tables(RET_CHUNK)
    row = lambda b, c: b * nc + c
    const2 = lambda b, c: (0, 0)
    const3 = lambda b, c: (0, 0, 0)
    return pl.pallas_call(
        _ret_prompt_body,
        grid=(batch, nc),
        in_specs=[
            pl.BlockSpec((RET_CHUNK, W_QK), lambda b, c: (row(b, c), COL_Q // W_QK)),
            pl.BlockSpec((RET_CHUNK, W_QK), lambda b, c: (row(b, c), COL_K // W_QK)),
            pl.BlockSpec((RET_CHUNK, W_V), lambda b, c: (row(b, c), COL_V // W_V)),
            pl.BlockSpec((RET_CHUNK, W_V), lambda b, c: (row(b, c), COL_G // W_V)),
            pl.BlockSpec((1, W_V), const2),
            pl.BlockSpec((H_RET, RET_CHUNK, RET_CHUNK), const3),
            pl.BlockSpec((RET_CHUNK, H_RET), const2),
            pl.BlockSpec((RET_CHUNK, H_RET), const2),
            pl.BlockSpec((H_RET, 1, DV_RET), const3),
        ],
        out_specs=[
            pl.BlockSpec((RET_CHUNK, W_V), lambda b, c: (row(b, c), 0)),
            pl.BlockSpec((1, H_RET, DK_RET, DV_RET), lambda b, c: (b, 0, 0, 0)),
        ],
        out_shape=[
            jax.ShapeDtypeStruct((batch * seq, W_V), BF16),
            jax.ShapeDtypeStruct((batch, H_RET, DK_RET, DV_RET), F32),
        ],
        compiler_params=pltpu.CompilerParams(
            dimension_semantics=("arbitrary", "arbitrary"), vmem_limit_bytes=VMEM_LIMIT
        ),
        name="ret_prompt",
    )(p, p, p, p, gn_gain, decay, cross, kdec, sdec)


RS_SEQS = 16


def _ret_sample_body(q_ref, k_ref, v_ref, g_ref, s_ref, gain_ref, decay_ref, cd_ref, kd_ref, gs_ref, out_ref, snew_ref):
    n_seq = s_ref.shape[0]
    rows = q_ref.shape[0]
    t_len = rows // n_seq
    q = q_ref[...]
    k = k_ref[...]
    v = v_ref[...].astype(BF16)
    s = s_ref[...]
    inner = lax.dot_general(q.astype(BF16), k.astype(BF16), _NT, preferred_element_type=F32) * decay_ref[...]
    o = jnp.dot(inner.astype(BF16), v, preferred_element_type=F32)
    row_seq = lax.broadcasted_iota(jnp.int32, (rows, DK_RET), 0) // t_len
    q_exp = jnp.concatenate([jnp.where(row_seq == b, q, 0.0) for b in range(n_seq)], axis=1).astype(BF16)
    s_rows = s.reshape(n_seq * DK_RET, DV_RET).astype(BF16)
    o = o + jnp.dot(q_exp, s_rows, preferred_element_type=F32) * cd_ref[...]
    kdec_t = (k * kd_ref[...]).T
    col_seq = lax.broadcasted_iota(jnp.int32, (DK_RET, rows), 1) // t_len
    k_exp_t = jnp.concatenate([jnp.where(col_seq == b, kdec_t, 0.0) for b in range(n_seq)], axis=0).astype(BF16)
    upd = jnp.dot(k_exp_t, v, preferred_element_type=F32)
    snew_ref[...] = gs_ref[...] * s + upd.reshape(n_seq, DK_RET, DV_RET)
    out_ref[...] = _group_norm_gate(o, gain_ref[...], g_ref[...])


def _ret_sample(p, state, gn_gain, batch, t_len):
    decay, cross, kdec, sdec = _ret_tables(t_len)
    rows = RS_SEQS * t_len
    decay_bd = np.stack([np.kron(np.eye(RS_SEQS, dtype=np.float32), decay[h]) for h in range(H_RET)])
    cross_rows = np.tile(cross.T, (1, RS_SEQS))[:, :, None]
    kdec_rows = np.tile(kdec.T, (1, RS_SEQS))[:, :, None]
    qk_blk = lambda col: pl.BlockSpec((rows, DK_RET), lambda i, h: (i, col // DK_RET + h))
    v_blk = lambda col: pl.BlockSpec((rows, DV_RET), lambda i, h: (i, col // DV_RET + h))
    state_blk = pl.BlockSpec((RS_SEQS, None, DK_RET, DV_RET), lambda i, h: (i, h, 0, 0))
    return pl.pallas_call(
        _ret_sample_body,
        grid=(batch // RS_SEQS, H_RET),
        in_specs=[
            qk_blk(COL_Q),
            qk_blk(COL_K),
            v_blk(COL_V),
            v_blk(COL_G),
            state_blk,
            pl.BlockSpec((1, DV_RET), lambda i, h: (0, h)),
            pl.BlockSpec((None, rows, rows), lambda i, h: (h, 0, 0)),
            pl.BlockSpec((None, rows, 1), lambda i, h: (h, 0, 0)),
            pl.BlockSpec((None, rows, 1), lambda i, h: (h, 0, 0)),
            pl.BlockSpec((None, 1, DV_RET), lambda i, h: (h, 0, 0)),
        ],
        out_specs=[pl.BlockSpec((rows, DV_RET), lambda i, h: (i, h)), state_blk],
        out_shape=[
            jax.ShapeDtypeStruct((batch * t_len, W_V), F32),
            jax.ShapeDtypeStruct((batch, H_RET, DK_RET, DV_RET), F32),
        ],
        compiler_params=pltpu.CompilerParams(
            dimension_semantics=("arbitrary", "arbitrary"), vmem_limit_bytes=VMEM_LIMIT
        ),
        name="ret_sample",
    )(p, p, p, p, state, gn_gain, decay_bd, cross_rows, kdec_rows, sdec)


def _sb_weights(qk2, tri, bias2, carry, mask):
    drop, log2_beta = _sb_drop(qk2, bias2, mask)
    return _sb_finish(drop, log2_beta, _sb_after(drop, tri), carry, mask)


def _sb_drop(qk2, bias2, mask):
    z2 = qk2 + bias2
    drop = jnp.maximum(z2, 0.0) + jnp.log2(1.0 + jnp.exp2(-jnp.abs(z2)))
    log2_beta = z2 - drop
    if mask is not None:
        drop = jnp.where(mask, drop, 0.0)
    return drop, log2_beta


def _sb_after(drop, tri):
    hi = drop.astype(BF16)
    lo = (drop - hi.astype(F32)).astype(BF16)
    return jnp.dot(hi, tri, preferred_element_type=F32) + jnp.dot(lo, tri, preferred_element_type=F32)


def _sb_finish(drop, log2_beta, after, carry, mask):
    a = jnp.exp2(log2_beta - after - carry)
    if mask is not None:
        a = jnp.where(mask, a, 0.0)
    return a, carry + after[:, 0:1] + drop[:, 0:1]


def _tri_np(n):
    return np.tril(np.ones((n, n), np.float32), -1)


SBP_T = 256
SBP_HEADS = 4


def _sb_prompt_body(bias_ref, q_ref, k_ref, v_ref, sg_ref, tri_ref, out_ref):
    hg = pl.program_id(1)
    i = pl.program_id(2)
    heads = range(SBP_HEADS)
    bias2 = [bias_ref[hg * SBP_HEADS + g] for g in heads]
    tri = tri_ref[...]
    q = [q_ref[:, g * D_SB : (g + 1) * D_SB] for g in heads]
    r = lax.broadcasted_iota(jnp.int32, (SBP_T, SBP_T), 0)
    c = lax.broadcasted_iota(jnp.int32, (SBP_T, SBP_T), 1)

    def tile(kb, state, mask):
        off = pl.multiple_of(kb * SBP_T, SBP_T)
        qk2 = [lax.dot_general(q[g], k_ref[g, pl.ds(off, SBP_T), :], _NT, preferred_element_type=F32) for g in heads]
        dl = [_sb_drop(qk2[g], bias2[g], mask) for g in heads]
        after = [_sb_after(dl[g][0], tri) for g in heads]
        new_state = []
        for g in heads:
            carry, acc = state[g]
            a, carry = _sb_finish(dl[g][0], dl[g][1], after[g], carry, mask)
            acc = acc + jnp.dot(a.astype(BF16), v_ref[g, pl.ds(off, SBP_T), :], preferred_element_type=F32)
            new_state.append((carry, acc))
        return tuple(new_state)

    state = tuple((jnp.zeros((SBP_T, 1), F32), jnp.zeros((SBP_T, D_SB), F32)) for _ in heads)
    state = tile(i, state, c < r)
    state = lax.fori_loop(0, i, lambda n, st: tile(i - 1 - n, st, None), state)
    acc = jnp.concatenate([st[1] for st in state], axis=1)
    out_ref[...] = (acc * _silu(sg_ref[...].astype(F32))).astype(out_ref.dtype)


def _sb_prompt(p, sk_heads, sv_heads, sb_bias, batch, seq):
    nq = seq // SBP_T
    width = SBP_HEADS * D_SB
    return pl.pallas_call(
        _sb_prompt_body,
        grid=(batch, H_SB // SBP_HEADS, nq),
        in_specs=[
            pl.BlockSpec(memory_space=pltpu.SMEM),
            pl.BlockSpec((SBP_T, width), lambda b, h, i: (b * nq + i, COL_SQ // width + h)),
            pl.BlockSpec((SBP_HEADS, seq, D_SB), lambda b, h, i: (h, b, 0)),
            pl.BlockSpec((SBP_HEADS, seq, D_SB), lambda b, h, i: (h, b, 0)),
            pl.BlockSpec((SBP_T, width), lambda b, h, i: (b * nq + i, COL_SG // width + h)),
            pl.BlockSpec((SBP_T, SBP_T), lambda b, h, i: (0, 0)),
        ],
        out_specs=pl.BlockSpec((SBP_T, width), lambda b, h, i: (b * nq + i, h)),
        out_shape=jax.ShapeDtypeStruct((batch * seq, W_SB), BF16),
        compiler_params=pltpu.CompilerParams(
            dimension_semantics=("arbitrary", "arbitrary", "arbitrary"), vmem_limit_bytes=VMEM_LIMIT
        ),
        name="sb_prompt",
    )(sb_bias * LOG2E, p, sk_heads, sv_heads, p, jnp.asarray(_tri_np(SBP_T), BF16))


SBS_PAGES = 4


def _sb_sample_body(pt_ref, bias_ref, hm_ref, tri_ref, q_ref, kn_ref, vn_ref, sg_ref, *rest):
    k_refs = rest[:SBS_PAGES]
    v_refs = rest[SBS_PAGES : 2 * SBS_PAGES]
    out_ref, qbd_ref, acc_ref, carry_ref, kd_ref, vd_ref, stage_ref = rest[2 * SBS_PAGES :]
    g = pl.program_id(1)
    t_len = q_ref.shape[0]
    rows = H_SB * t_len
    page = k_refs[0].shape[0] // H_SB

    def densify(src_ref, dst_ref, row0, n_keys):
        for h in range(H_SB):
            dst_ref[row0 : row0 + n_keys, h * D_SB : (h + 1) * D_SB] = src_ref[pl.ds(h, n_keys, stride=H_SB), :].astype(
                dst_ref.dtype
            )

    def sweep(n_keys, mask):
        qk2 = lax.dot_general(qbd_ref[...], kd_ref[0:n_keys, :], _NT, preferred_element_type=F32)
        a, carry = _sb_weights(qk2, tri_ref[0:n_keys, 0:n_keys], bias_ref[...], carry_ref[...], mask)
        carry_ref[...] = carry
        acc_ref[...] += jnp.dot(a.astype(BF16), vd_ref[0:n_keys, :], preferred_element_type=F32)

    @pl.when(g == 0)
    def _new_tokens():
        q = q_ref[...]
        qbd_ref[...] = jnp.concatenate([q * hm_ref[h : h + 1, :] for h in range(H_SB)], axis=0).astype(BF16)
        acc_ref[...] = jnp.zeros_like(acc_ref)
        carry_ref[...] = jnp.zeros_like(carry_ref)
        for src_ref, dst_ref in ((kn_ref, kd_ref), (vn_ref, vd_ref)):
            stage_ref[...] = jnp.zeros_like(stage_ref)
            densify(src_ref, stage_ref, 0, t_len)
            dst_ref[0:page, :] = stage_ref[...].astype(BF16)
        r = lax.broadcasted_iota(jnp.int32, (rows, page), 0)
        c = lax.broadcasted_iota(jnp.int32, (rows, page), 1)
        sweep(page, c < (r % t_len))

    for pg in range(SBS_PAGES):
        densify(k_refs[pg], kd_ref, pg * page, page)
        densify(v_refs[pg], vd_ref, pg * page, page)
    sweep(SBS_PAGES * page, None)

    @pl.when(g == pl.num_programs(1) - 1)
    def _finish():
        acc = acc_ref[...]
        o = jnp.concatenate(
            [acc[h * t_len : (h + 1) * t_len, h * D_SB : (h + 1) * D_SB] for h in range(H_SB)], axis=1
        )
        out_ref[...] = o * _silu(sg_ref[...])


def _sb_sample(page_table, p, sk, sv, pool_k, pool_v, sb_bias, batch, t_len):
    n_pages = page_table.shape[1]
    n_groups = n_pages // SBS_PAGES
    page_rows = pool_k.shape[1]
    rows = H_SB * t_len
    page = page_rows // H_SB
    step_keys = SBS_PAGES * page
    bias_rows = jnp.repeat(sb_bias * LOG2E, t_len)[:, None]
    head_mask = np.repeat(np.eye(H_SB, dtype=np.float32), D_SB, axis=1)
    tri = jnp.asarray(_tri_np(step_keys), BF16)

    def page_spec(pg):
        def index(b, g, pt):
            return (pt[b * n_pages + (n_groups - 1 - g) * SBS_PAGES + pg], 0, 0)

        return pl.BlockSpec((None, page_rows, D_SB), index)

    const2 = lambda b, g, pt: (0, 0)
    grid_spec = pltpu.PrefetchScalarGridSpec(
        num_scalar_prefetch=1,
        grid=(batch, n_groups),
        in_specs=[
            pl.BlockSpec((rows, 1), const2),
            pl.BlockSpec((H_SB, W_SB), const2),
            pl.BlockSpec((step_keys, step_keys), const2),
            pl.BlockSpec((t_len, W_SB), lambda b, g, pt: (b, COL_SQ // W_SB)),
            pl.BlockSpec((rows, D_SB), lambda b, g, pt: (b, 0)),
            pl.BlockSpec((rows, D_SB), lambda b, g, pt: (b, 0)),
            pl.BlockSpec((t_len, W_SB), lambda b, g, pt: (b, COL_SG // W_SB)),
        ]
        + [page_spec(pg) for pg in range(SBS_PAGES)] * 2,
        out_specs=pl.BlockSpec((t_len, W_SB), lambda b, g, pt: (b, 0)),
        scratch_shapes=[
            pltpu.VMEM((rows, W_SB), BF16),
            pltpu.VMEM((rows, W_SB), F32),
            pltpu.VMEM((rows, 1), F32),
            pltpu.VMEM((step_keys, W_SB), BF16),
            pltpu.VMEM((step_keys, W_SB), BF16),
            pltpu.VMEM((page, W_SB), F32),
        ],
    )
    return pl.pallas_call(
        _sb_sample_body,
        grid_spec=grid_spec,
        out_shape=jax.ShapeDtypeStruct((batch * t_len, W_SB), F32),
        compiler_params=pltpu.CompilerParams(
            dimension_semantics=("arbitrary", "arbitrary"), vmem_limit_bytes=VMEM_LIMIT
        ),
        name="sb_sample",
    )(
        page_table.reshape(-1), bias_rows, head_mask, tri, p, sk, sv, p,
        *([pool_k] * SBS_PAGES), *([pool_v] * SBS_PAGES),
    )


def _merge_body(x_ref, rb_ref, sb_ref, gr_ref, gs_ref, wr_ref, ws_ref, wo_ref, fg_ref, y_ref):
    m = gr_ref[...].astype(F32) * jnp.dot(rb_ref[...].astype(BF16), wr_ref[...], preferred_element_type=F32)
    m = m + gs_ref[...].astype(F32) * jnp.dot(sb_ref[...].astype(BF16), ws_ref[...], preferred_element_type=F32)
    hid = x_ref[...] + jnp.dot(m.astype(BF16), wo_ref[...], preferred_element_type=F32)
    r = lax.rsqrt(jnp.mean(hid * hid, axis=-1, keepdims=True) + EPS_RMS)
    y_ref[...] = hid * r * fg_ref[...]


def _merge(x, ret_branch, sb_branch, p, w_ret, w_sb, w_out, final_gain, *, tm):
    m = x.shape[0]
    resident = functools.partial(pl.BlockSpec, index_map=lambda i: (0, 0), pipeline_mode=pl.Buffered(1))
    return pl.pallas_call(
        _merge_body,
        grid=(m // tm,),
        in_specs=[
            pl.BlockSpec((tm, D_MODEL), lambda i: (i, 0)),
            pl.BlockSpec((tm, W_V), lambda i: (i, 0)),
            pl.BlockSpec((tm, W_SB), lambda i: (i, 0)),
            pl.BlockSpec((tm, D_MODEL), lambda i: (i, COL_GR // D_MODEL)),
            pl.BlockSpec((tm, D_MODEL), lambda i: (i, COL_GS // D_MODEL)),
            resident((W_V, D_MODEL)),
            resident((W_SB, D_MODEL)),
            resident((D_MODEL, D_MODEL)),
            pl.BlockSpec((1, D_MODEL), lambda i: (0, 0)),
        ],
        out_specs=pl.BlockSpec((tm, D_MODEL), lambda i: (i, 0)),
        out_shape=jax.ShapeDtypeStruct((m, D_MODEL), F32),
        compiler_params=pltpu.CompilerParams(dimension_semantics=("arbitrary",), vmem_limit_bytes=VMEM_LIMIT),
        name="merge",
    )(x, ret_branch, sb_branch, p, p, w_ret, w_sb, w_out, final_gain)


def _rope_tables(pos):
    half = DK_RET // 2
    inv = ROPE_BASE ** (-np.arange(half, dtype=np.float64) / half)
    ang = np.asarray(pos, np.float64)[:, None] * inv[None, :]
    cos, sin = np.cos(ang), np.sin(ang)
    cos_full = np.concatenate([cos, cos], axis=-1)
    sin_signed = np.concatenate([-sin, sin], axis=-1)
    return cos_full.astype(np.float32), sin_signed.astype(np.float32)


def _layer(xp, xs, state, pool_k, pool_v, page_table, norm_gain, w_in, sb_bias, gn_gain, w_ret, w_sb, w_out, final_gain):
    batch, seq, _ = xp.shape
    dec_batch, t_len, _ = xs.shape
    n_pool, page = pool_k.shape[:2]
    past_len = page_table.shape[1] * page

    w_in_b = w_in.astype(BF16)
    w_ret_b, w_sb_b, w_out_b = w_ret.astype(BF16), w_sb.astype(BF16), w_out.astype(BF16)
    gain2 = norm_gain[None, :]
    gn_gain2 = gn_gain[None, :]
    final_gain2 = final_gain[None, :]

    x2p = xp.reshape(batch * seq, D_MODEL)
    x2s = xs.reshape(dec_batch * t_len, D_MODEL)
    tm_p, tm_s = 512, 512
    cos_p, sin_p = _rope_tables(np.arange(seq))
    cos_s, sin_s = _rope_tables(past_len + np.arange(t_len))
    cos_s, sin_s = np.tile(cos_s, (tm_s // t_len, 1)), np.tile(sin_s, (tm_s // t_len, 1))

    pp, skp, svp, skp_heads, svp_heads = _project(
        x2p, gain2, cos_p, sin_p, w_in_b, tm=tm_p, rope_period_tiles=seq // tm_p, p_dtype=BF16
    )
    ps, sks, svs, _, _ = _project(x2s, gain2, cos_s, sin_s, w_in_b, tm=tm_s, rope_period_tiles=1, p_dtype=F32)

    rbp, state_p = _ret_prompt(pp, gn_gain2, batch, seq)
    rbs, state_s = _ret_sample(ps, state, gn_gain2, dec_batch, t_len)

    sbp = _sb_prompt(pp, skp_heads, svp_heads, sb_bias, batch, seq)
    sbs = _sb_sample(
        page_table, ps, sks, svs, pool_k.reshape(n_pool, page * H_SB, D_SB), pool_v.reshape(n_pool, page * H_SB, D_SB),
        sb_bias, dec_batch, t_len,
    )

    yp = _merge(x2p, rbp, sbp, pp, w_ret_b, w_sb_b, w_out_b, final_gain2, tm=256)
    ys = _merge(x2s, rbs, sbs, ps, w_ret_b, w_sb_b, w_out_b, final_gain2, tm=256)

    return (
        yp.reshape(batch, seq, D_MODEL),
        ys.reshape(dec_batch, t_len, D_MODEL),
        state_p,
        skp.reshape(batch, seq, H_SB, D_SB),
        svp.reshape(batch, seq, H_SB, D_SB),
        state_s,
        sks.reshape(dec_batch, t_len, H_SB, D_SB),
        svs.reshape(dec_batch, t_len, H_SB, D_SB),
    )


def kernel(x_prompt, x_sample, state_ret, cache_sb_k, cache_sb_v, page_table, norm_gain, w_in, sb_bias, ret_gn_gain, w_ret_proj, w_sb_proj, w_out, final_norm_gain):
    depth = w_in.shape[0]
    assert depth == 1, "single-layer trunk"
    outs = _layer(
        x_prompt, x_sample, state_ret[0], cache_sb_k[0], cache_sb_v[0], page_table, norm_gain[0], w_in[0],
        sb_bias[0], ret_gn_gain[0], w_ret_proj[0], w_sb_proj[0], w_out[0], final_norm_gain,
    )
    yp, ys, state_p, skp, svp, state_s, sks, svs = outs
    return (yp, ys, state_p[None], skp[None], svp[None], state_s[None], sks[None], svs[None])
```

```python
import functools

import jax
import jax.numpy as jnp
import numpy as np
from jax import lax
from jax.experimental import pallas as pl
from jax.experimental.pallas import tpu as pltpu

F32 = jnp.float32
BF16 = jnp.bfloat16

D_MODEL = 2048
H_RET = 8
DK_RET = 128
DV_RET = 256
RET_CHUNK = 128
H_SB = 8
D_SB = 128
ROPE_BASE = 10000.0
EPS_RMS = 1e-6
EPS_GN = 1e-5
W_QK = H_RET * DK_RET
W_V = H_RET * DV_RET
W_SB = H_SB * D_SB

PROJ_TN = 1024
PROJ_GROUP = 512
PROJ_ROW_GROUPS = 2
P_COLS = 2 * W_QK + 2 * W_V + 2 * W_SB + 2 * D_MODEL
N_P_TILES = P_COLS // PROJ_TN
COL_Q, COL_K, COL_V, COL_G, COL_SQ, COL_SG, COL_GR, COL_GS = 0, 1024, 2048, 4096, 6144, 7168, 8192, 10240

LOG2E = float(np.log2(np.e))
SB_Q_SCALE = D_SB**-0.5 * LOG2E

VMEM_LIMIT = 56 * 1024 * 1024

_NT = (((1,), (1,)), ((), ()))


def _silu(g):
    return g * jax.nn.sigmoid(g)


def _proj_body(x_ref, gain_ref, cos_ref, sin_ref, w_ref, p_ref, sk_ref, sv_ref, skh_ref, svh_ref, xn_ref):
    j = pl.program_id(1)

    @pl.when(j == 0)
    def _norm():
        x = x_ref[...]
        r = lax.rsqrt(jnp.mean(x * x, axis=-1, keepdims=True) + EPS_RMS)
        xn_ref[...] = (x * r * gain_ref[...]).astype(BF16)

    tm = xn_ref.shape[0]
    rows_per_group = tm // PROJ_ROW_GROUPS

    def groups():
        for c in range(PROJ_TN // PROJ_GROUP):
            for r in range(PROJ_ROW_GROUPS):
                rows = slice(r * rows_per_group, (r + 1) * rows_per_group)
                cols = slice(c * PROJ_GROUP, (c + 1) * PROJ_GROUP)
                yield r * rows_per_group, c * PROJ_GROUP, jnp.dot(xn_ref[rows, :], w_ref[:, cols], preferred_element_type=F32)

    def lane_blocks(acc):
        for h in range(PROJ_GROUP // DK_RET):
            yield h * DK_RET, acc[:, h * DK_RET : (h + 1) * DK_RET]

    @pl.when(j < COL_V // PROJ_TN)
    def _rope():
        scale = jnp.where(j == COL_K // PROJ_TN, F32(DK_RET**-0.5), F32(1.0))
        for row0, col0, acc in groups():
            cos = cos_ref[row0 : row0 + rows_per_group, :] * scale
            sin = sin_ref[row0 : row0 + rows_per_group, :] * scale
            for off, xh in lane_blocks(acc):
                y = xh * cos + pltpu.roll(xh, DK_RET // 2, axis=1) * sin
                p_ref[row0 : row0 + rows_per_group, col0 + off : col0 + off + DK_RET] = y.astype(p_ref.dtype)

    @pl.when((j >= COL_V // PROJ_TN) & (j < COL_GR // PROJ_TN))
    def _plain():
        scale = jnp.where(j == COL_SQ // PROJ_TN, F32(SB_Q_SCALE), F32(1.0))
        for row0, col0, acc in groups():
            p_ref[row0 : row0 + rows_per_group, col0 : col0 + PROJ_GROUP] = (acc * scale).astype(p_ref.dtype)

    @pl.when((j >= COL_GR // PROJ_TN) & (j < N_P_TILES))
    def _gate():
        for row0, col0, acc in groups():
            y = 0.5 * jnp.tanh(0.5 * acc) + 0.5
            p_ref[row0 : row0 + rows_per_group, col0 : col0 + PROJ_GROUP] = y.astype(p_ref.dtype)

    def heads_out(tok_ref, head_ref):
        for row0, col0, acc in groups():
            for off, ah in lane_blocks(acc):
                head = (col0 + off) // D_SB
                tok_ref[pl.ds(row0 * H_SB + head, rows_per_group, stride=H_SB), :] = ah
                head_ref[head, row0 : row0 + rows_per_group, :] = ah.astype(BF16)

    @pl.when(j == N_P_TILES)
    def _sk():
        heads_out(sk_ref, skh_ref)

    @pl.when(j == N_P_TILES + 1)
    def _sv():
        heads_out(sv_ref, svh_ref)


def _w_tile(j):
    first_kv = COL_SQ // PROJ_TN + 1
    n_kv = 2 * W_SB // PROJ_TN
    return jnp.where(j < first_kv, j, jnp.where(j < N_P_TILES, j + n_kv, j - (N_P_TILES - first_kv)))


def _project(x, gain, cos, sin, w, *, tm, rope_period_tiles, p_dtype):
    m = x.shape[0]
    n_tiles = w.shape[1] // PROJ_TN
    return pl.pallas_call(
        _proj_body,
        grid=(m // tm, n_tiles),
        in_specs=[
            pl.BlockSpec((tm, D_MODEL), lambda i, j: (i, 0)),
            pl.BlockSpec((1, D_MODEL), lambda i, j: (0, 0)),
            pl.BlockSpec((tm, DK_RET), lambda i, j: (i % rope_period_tiles, 0)),
            pl.BlockSpec((tm, DK_RET), lambda i, j: (i % rope_period_tiles, 0)),
            pl.BlockSpec((D_MODEL, PROJ_TN), lambda i, j: (0, _w_tile(j))),
        ],
        out_specs=[
            pl.BlockSpec((tm, PROJ_TN), lambda i, j: (i, jnp.minimum(j, N_P_TILES - 1))),
            pl.BlockSpec((tm * H_SB, D_SB), lambda i, j: (i, 0)),
            pl.BlockSpec((tm * H_SB, D_SB), lambda i, j: (i, 0)),
            pl.BlockSpec((H_SB, tm, D_SB), lambda i, j: (0, i, 0)),
            pl.BlockSpec((H_SB, tm, D_SB), lambda i, j: (0, i, 0)),
        ],
        out_shape=[
            jax.ShapeDtypeStruct((m, P_COLS), p_dtype),
            jax.ShapeDtypeStruct((m * H_SB, D_SB), F32),
            jax.ShapeDtypeStruct((m * H_SB, D_SB), F32),
            jax.ShapeDtypeStruct((H_SB, m, D_SB), BF16),
            jax.ShapeDtypeStruct((H_SB, m, D_SB), BF16),
        ],
        scratch_shapes=[pltpu.VMEM((tm, D_MODEL), BF16)],
        compiler_params=pltpu.CompilerParams(
            dimension_semantics=("arbitrary", "arbitrary"), vmem_limit_bytes=VMEM_LIMIT
        ),
        name="proj",
    )(x, gain, cos, sin, w)


def _group_norm_gate(o, gain, g):
    mu = jnp.mean(o, axis=-1, keepdims=True)
    d = o - mu
    var = jnp.mean(d * d, axis=-1, keepdims=True)
    return d * lax.rsqrt(var + EPS_GN) * gain * _silu(g)


def _ret_prompt_body(q_ref, k_ref, v_ref, g_ref, gain_ref, decay_ref, cd_ref, kd_ref, gs_ref, out_ref, s_ref):
    c = pl.program_id(1)

    @pl.when(c == 0)
    def _init():
        s_ref[...] = jnp.zeros_like(s_ref)

    heads = range(H_RET)
    q = [q_ref[:, h * DK_RET : (h + 1) * DK_RET] for h in heads]
    k = [k_ref[:, h * DK_RET : (h + 1) * DK_RET] for h in heads]
    v = [v_ref[:, h * DV_RET : (h + 1) * DV_RET] for h in heads]
    s = [s_ref[0, h] for h in heads]
    inner = [lax.dot_general(q[h], k[h], _NT, preferred_element_type=F32) for h in heads]
    cross = [jnp.dot(q[h], s[h].astype(BF16), preferred_element_type=F32) for h in heads]
    for h in heads:
        kdec_t = (k[h].astype(F32) * kd_ref[:, h : h + 1]).T.astype(BF16)
        s_ref[0, h] = gs_ref[h] * s[h] + jnp.dot(kdec_t, v[h], preferred_element_type=F32)
    for h in heads:
        o = jnp.dot((inner[h] * decay_ref[h]).astype(BF16), v[h], preferred_element_type=F32)
        o = o + cross[h] * cd_ref[:, h : h + 1]
        g = g_ref[:, h * DV_RET : (h + 1) * DV_RET].astype(F32)
        gain = gain_ref[:, h * DV_RET : (h + 1) * DV_RET]
        out_ref[:, h * DV_RET : (h + 1) * DV_RET] = _group_norm_gate(o, gain, g).astype(out_ref.dtype)


def _ret_tables(chunk):
    lg = np.log1p(-np.exp2(-5.0 - np.arange(H_RET, dtype=np.float64)))
    idx = np.arange(chunk, dtype=np.float64)
    diff = idx[:, None] - idx[None, :]
    decay = np.where(diff >= 0, np.exp(lg[:, None, None] * np.maximum(diff, 0.0)[None]), 0.0)
    cross = np.exp(lg[:, None] * (idx + 1.0)[None, :]).T
    kdec = np.exp(lg[:, None] * (chunk - 1.0 - idx)[None, :]).T
    sdec = np.broadcast_to(np.exp(lg * chunk)[:, None, None], (H_RET, 1, DV_RET))
    return tuple(np.asarray(a, np.float32) for a in (decay, cross, kdec, sdec))


def _ret_prompt(p, gn_gain, batch, seq):
    nc = seq // RET_CHUNK
    decay, cross, kdec, sdec = _ret_tables(RET_CHUNK)
    row = lambda b, c: b * nc + c
    const2 = lambda b, c: (0, 0)
    const3 = lambda b, c: (0, 0, 0)
    return pl.pallas_call(
        _ret_prompt_body,
        grid=(batch, nc),
        in_specs=[
            pl.BlockSpec((RET_CHUNK, W_QK), lambda b, c: (row(b, c), COL_Q // W_QK)),
            pl.BlockSpec((RET_CHUNK, W_QK), lambda b, c: (row(b, c), COL_K // W_QK)),
            pl.BlockSpec((RET_CHUNK, W_V), lambda b, c: (row(b, c), COL_V // W_V)),
            pl.BlockSpec((RET_CHUNK, W_V), lambda b, c: (row(b, c), COL_G // W_V)),
            pl.BlockSpec((1, W_V), const2),
            pl.BlockSpec((H_RET, RET_CHUNK, RET_CHUNK), const3),
            pl.BlockSpec((RET_CHUNK, H_RET), const2),
            pl.BlockSpec((RET_CHUNK, H_RET), const2),
            pl.BlockSpec((H_RET, 1, DV_RET), const3),
        ],
        out_specs=[
            pl.BlockSpec((RET_CHUNK, W_V), lambda b, c: (row(b, c), 0)),
            pl.BlockSpec((1, H_RET, DK_RET, DV_RET), lambda b, c: (b, 0, 0, 0)),
        ],
        out_shape=[
            jax.ShapeDtypeStruct((batch * seq, W_V), BF16),
            jax.ShapeDtypeStruct((batch, H_RET, DK_RET, DV_RET), F32),
        ],
        compiler_params=pltpu.CompilerParams(
            dimension_semantics=("arbitrary", "arbitrary"), vmem_limit_bytes=VMEM_LIMIT
        ),
        name="ret_prompt",
    )(p, p, p, p, gn_gain, decay, cross, kdec, sdec)


RS_SEQS = 16


def _ret_sample_body(q_ref, k_ref, v_ref, g_ref, s_ref, gain_ref, decay_ref, cd_ref, kd_ref, gs_ref, out_ref, snew_ref):
    n_seq = s_ref.shape[0]
    rows = q_ref.shape[0]
    t_len = rows // n_seq
    q = q_ref[...]
    k = k_ref[...]
    v = v_ref[...].astype(BF16)
    s = s_ref[...]
    inner = lax.dot_general(q.astype(BF16), k.astype(BF16), _NT, preferred_element_type=F32) * decay_ref[...]
    o = jnp.dot(inner.astype(BF16), v, preferred_element_type=F32)
    row_seq = lax.broadcasted_iota(jnp.int32, (rows, DK_RET), 0) // t_len
    q_exp = jnp.concatenate([jnp.where(row_seq == b, q, 0.0) for b in range(n_seq)], axis=1).astype(BF16)
    s_rows = s.reshape(n_seq * DK_RET, DV_RET).astype(BF16)
    o = o + jnp.dot(q_exp, s_rows, preferred_element_type=F32) * cd_ref[...]
    kdec_t = (k * kd_ref[...]).T
    col_seq = lax.broadcasted_iota(jnp.int32, (DK_RET, rows), 1) // t_len
    k_exp_t = jnp.concatenate([jnp.where(col_seq == b, kdec_t, 0.0) for b in range(n_seq)], axis=0).astype(BF16)
    upd = jnp.dot(k_exp_t, v, preferred_element_type=F32)
    snew_ref[...] = gs_ref[...] * s + upd.reshape(n_seq, DK_RET, DV_RET)
    out_ref[...] = _group_norm_gate(o, gain_ref[...], g_ref[...])


def _ret_sample(p, state, gn_gain, batch, t_len):
    decay, cross, kdec, sdec = _ret_tables(t_len)
    rows = RS_SEQS * t_len
    decay_bd = np.stack([np.kron(np.eye(RS_SEQS, dtype=np.float32), decay[h]) for h in range(H_RET)])
    cross_rows = np.tile(cross.T, (1, RS_SEQS))[:, :, None]
    kdec_rows = np.tile(kdec.T, (1, RS_SEQS))[:, :, None]
    qk_blk = lambda col: pl.BlockSpec((rows, DK_RET), lambda i, h: (i, col // DK_RET + h))
    v_blk = lambda col: pl.BlockSpec((rows, DV_RET), lambda i, h: (i, col // DV_RET + h))
    state_blk = pl.BlockSpec((RS_SEQS, None, DK_RET, DV_RET), lambda i, h: (i, h, 0, 0))
    return pl.pallas_call(
        _ret_sample_body,
        grid=(batch // RS_SEQS, H_RET),
        in_specs=[
            qk_blk(COL_Q),
            qk_blk(COL_K),
            v_blk(COL_V),
            v_blk(COL_G),
            state_blk,
            pl.BlockSpec((1, DV_RET), lambda i, h: (0, h)),
            pl.BlockSpec((None, rows, rows), lambda i, h: (h, 0, 0)),
            pl.BlockSpec((None, rows, 1), lambda i, h: (h, 0, 0)),
            pl.BlockSpec((None, rows, 1), lambda i, h: (h, 0, 0)),
            pl.BlockSpec((None, 1, DV_RET), lambda i, h: (h, 0, 0)),
        ],
        out_specs=[pl.BlockSpec((rows, DV_RET), lambda i, h: (i, h)), state_blk],
        out_shape=[
            jax.ShapeDtypeStruct((batch * t_len, W_V), F32),
            jax.ShapeDtypeStruct((batch, H_RET, DK_RET, DV_RET), F32),
        ],
        compiler_params=pltpu.CompilerParams(
            dimension_semantics=("arbitrary", "arbitrary"), vmem_limit_bytes=VMEM_LIMIT
        ),
        name="ret_sample",
    )(p, p, p, p, state, gn_gain, decay_bd, cross_rows, kdec_rows, sdec)


def _sb_weights(qk2, tri, bias2, carry, mask):
    drop, log2_beta = _sb_drop(qk2, bias2, mask)
    return _sb_finish(drop, log2_beta, _sb_after(drop, tri), carry, mask)


def _sb_drop(qk2, bias2, mask):
    z2 = qk2 + bias2
    drop = jnp.maximum(z2, 0.0) + jnp.log2(1.0 + jnp.exp2(-jnp.abs(z2)))
    log2_beta = z2 - drop
    if mask is not None:
        drop = jnp.where(mask, drop, 0.0)
    return drop, log2_beta


def _sb_after(drop, tri):
    hi = drop.astype(BF16)
    lo = (drop - hi.astype(F32)).astype(BF16)
    return jnp.dot(hi, tri, preferred_element_type=F32) + jnp.dot(lo, tri, preferred_element_type=F32)


def _sb_finish(drop, log2_beta, after, carry, mask):
    a = jnp.exp2(log2_beta - after - carry)
    if mask is not None:
        a = jnp.where(mask, a, 0.0)
    return a, carry + after[:, 0:1] + drop[:, 0:1]


def _tri_np(n):
    return np.tril(np.ones((n, n), np.float32), -1)


SBP_T = 256
SBP_HEADS = 8


def _sb_prompt_body(bias_ref, q_ref, k_ref, v_ref, sg_ref, tri_ref, out_ref):
    hg = pl.program_id(1)
    i = pl.program_id(2)
    heads = range(SBP_HEADS)
    bias2 = [bias_ref[hg * SBP_HEADS + g] for g in heads]
    tri = tri_ref[...]
    q = [q_ref[:, g * D_SB : (g + 1) * D_SB] for g in heads]
    r = lax.broadcasted_iota(jnp.int32, (SBP_T, SBP_T), 0)
    c = lax.broadcasted_iota(jnp.int32, (SBP_T, SBP_T), 1)

    def tile(kb, state, mask):
        off = pl.multiple_of(kb * SBP_T, SBP_T)
        qk2 = [lax.dot_general(q[g], k_ref[g, pl.ds(off, SBP_T), :], _NT, preferred_element_type=F32) for g in heads]
        dl = [_sb_drop(qk2[g], bias2[g], mask) for g in heads]
        after = [_sb_after(dl[g][0], tri) for g in heads]
        new_state = []
        for g in heads:
            carry, acc = state[g]
            a, carry = _sb_finish(dl[g][0], dl[g][1], after[g], carry, mask)
            acc = acc + jnp.dot(a.astype(BF16), v_ref[g, pl.ds(off, SBP_T), :], preferred_element_type=F32)
            new_state.append((carry, acc))
        return tuple(new_state)

    state = tuple((jnp.zeros((SBP_T, 1), F32), jnp.zeros((SBP_T, D_SB), F32)) for _ in heads)
    state = tile(i, state, c < r)
    state = lax.fori_loop(0, i, lambda n, st: tile(i - 1 - n, st, None), state)
    acc = jnp.concatenate([st[1] for st in state], axis=1)
    out_ref[...] = (acc * _silu(sg_ref[...].astype(F32))).astype(out_ref.dtype)


def _sb_prompt(p, sk_heads, sv_heads, sb_bias, batch, seq):
    nq = seq // SBP_T
    width = SBP_HEADS * D_SB
    return pl.pallas_call(
        _sb_prompt_body,
        grid=(batch, H_SB // SBP_HEADS, nq),
        in_specs=[
            pl.BlockSpec(memory_space=pltpu.SMEM),
            pl.BlockSpec((SBP_T, width), lambda b, h, i: (b * nq + i, COL_SQ // width + h)),
            pl.BlockSpec((SBP_HEADS, seq, D_SB), lambda b, h, i: (h, b, 0)),
            pl.BlockSpec((SBP_HEADS, seq, D_SB), lambda b, h, i: (h, b, 0)),
            pl.BlockSpec((SBP_T, width), lambda b, h, i: (b * nq + i, COL_SG // width + h)),
            pl.BlockSpec((SBP_T, SBP_T), lambda b, h, i: (0, 0)),
        ],
        out_specs=pl.BlockSpec((SBP_T, width), lambda b, h, i: (b * nq + i, h)),
        out_shape=jax.ShapeDtypeStruct((batch * seq, W_SB), BF16),
        compiler_params=pltpu.CompilerParams(
            dimension_semantics=("arbitrary", "arbitrary", "arbitrary"), vmem_limit_bytes=VMEM_LIMIT
        ),
        name="sb_prompt",
    )(sb_bias * LOG2E, p, sk_heads, sv_heads, p, jnp.asarray(_tri_np(SBP_T), BF16))


SBS_CHUNK_PAGES = 4


def _sb_sample_body(pt_ref, bias_ref, hm_ref, tri_ref, q_ref, kn_ref, vn_ref, sg_ref, *rest):
    n_pages = (len(rest) - 7) // 2
    k_refs = rest[:n_pages]
    v_refs = rest[n_pages : 2 * n_pages]
    out_ref, kd_ref, vd_ref, knd_ref, vnd_ref, kstage_ref, vstage_ref = rest[2 * n_pages :]
    t_len = q_ref.shape[0]
    rows = H_SB * t_len
    page = k_refs[0].shape[0] // H_SB
    chunk = SBS_CHUNK_PAGES * page
    q = q_ref[...]
    qbd = jnp.concatenate([q * hm_ref[h : h + 1, :] for h in range(H_SB)], axis=0).astype(BF16)
    bias2 = bias_ref[...]

    def densify(src_ref, dst_ref, row0, n_keys):
        for h in range(H_SB):
            dst_ref[row0 : row0 + n_keys, h * D_SB : (h + 1) * D_SB] = src_ref[pl.ds(h, n_keys, stride=H_SB), :].astype(
                dst_ref.dtype
            )

    def sweep(k, v, tri, carry, acc, mask):
        qk2 = lax.dot_general(qbd, k, _NT, preferred_element_type=F32)
        a, carry = _sb_weights(qk2, tri, bias2, carry, mask)
        return carry, acc + jnp.dot(a.astype(BF16), v, preferred_element_type=F32)

    for src_ref, stage_ref, dst_ref in ((kn_ref, kstage_ref, knd_ref), (vn_ref, vstage_ref, vnd_ref)):
        stage_ref[...] = jnp.zeros_like(stage_ref)
        densify(src_ref, stage_ref, 0, t_len)
        dst_ref[...] = stage_ref[...].astype(BF16)
    r = lax.broadcasted_iota(jnp.int32, (rows, page), 0)
    c = lax.broadcasted_iota(jnp.int32, (rows, page), 1)
    carry, acc = sweep(
        knd_ref[...], vnd_ref[...], tri_ref[0:page, 0:page],
        jnp.zeros((rows, 1), F32), jnp.zeros((rows, W_SB), F32), c < (r % t_len),
    )

    for ch in reversed(range(n_pages // SBS_CHUNK_PAGES)):
        for pg in range(ch * SBS_CHUNK_PAGES, (ch + 1) * SBS_CHUNK_PAGES):
            densify(k_refs[pg], kd_ref, pg * page, page)
            densify(v_refs[pg], vd_ref, pg * page, page)
        keys = slice(ch * chunk, (ch + 1) * chunk)
        carry, acc = sweep(kd_ref[keys, :], vd_ref[keys, :], tri_ref[...], carry, acc, None)

    o = jnp.concatenate([acc[h * t_len : (h + 1) * t_len, h * D_SB : (h + 1) * D_SB] for h in range(H_SB)], axis=1)
    out_ref[...] = o * _silu(sg_ref[...])


def _sb_sample(page_table, p, sk, sv, pool_k, pool_v, sb_bias, batch, t_len):
    n_pages = page_table.shape[1]
    assert n_pages % SBS_CHUNK_PAGES == 0
    page_rows = pool_k.shape[1]
    rows = H_SB * t_len
    page = page_rows // H_SB
    chunk = SBS_CHUNK_PAGES * page
    bias_rows = jnp.repeat(sb_bias * LOG2E, t_len)[:, None]
    head_mask = np.repeat(np.eye(H_SB, dtype=np.float32), D_SB, axis=1)
    tri = jnp.asarray(_tri_np(chunk), BF16)

    def page_spec(pg):
        return pl.BlockSpec((None, page_rows, D_SB), lambda b, pt: (pt[b * n_pages + pg], 0, 0))

    const2 = lambda b, pt: (0, 0)
    grid_spec = pltpu.PrefetchScalarGridSpec(
        num_scalar_prefetch=1,
        grid=(batch,),
        in_specs=[
            pl.BlockSpec((rows, 1), const2),
            pl.BlockSpec((H_SB, W_SB), const2),
            pl.BlockSpec((chunk, chunk), const2),
            pl.BlockSpec((t_len, W_SB), lambda b, pt: (b, COL_SQ // W_SB)),
            pl.BlockSpec((rows, D_SB), lambda b, pt: (b, 0)),
            pl.BlockSpec((rows, D_SB), lambda b, pt: (b, 0)),
            pl.BlockSpec((t_len, W_SB), lambda b, pt: (b, COL_SG // W_SB)),
        ]
        + [page_spec(pg) for pg in range(n_pages)] * 2,
        out_specs=pl.BlockSpec((t_len, W_SB), lambda b, pt: (b, 0)),
        scratch_shapes=[
            pltpu.VMEM((n_pages * page, W_SB), BF16),
            pltpu.VMEM((n_pages * page, W_SB), BF16),
            pltpu.VMEM((page, W_SB), BF16),
            pltpu.VMEM((page, W_SB), BF16),
            pltpu.VMEM((page, W_SB), F32),
            pltpu.VMEM((page, W_SB), F32),
        ],
    )
    return pl.pallas_call(
        _sb_sample_body,
        grid_spec=grid_spec,
        out_shape=jax.ShapeDtypeStruct((batch * t_len, W_SB), F32),
        compiler_params=pltpu.CompilerParams(dimension_semantics=("arbitrary",), vmem_limit_bytes=VMEM_LIMIT),
        name="sb_sample",
    )(
        page_table.reshape(-1), bias_rows, head_mask, tri, p, sk, sv, p,
        *([pool_k] * n_pages), *([pool_v] * n_pages),
    )


def _merge_body(x_ref, rb_ref, sb_ref, gr_ref, gs_ref, wr_ref, ws_ref, wo_ref, fg_ref, y_ref):
    m = gr_ref[...].astype(F32) * jnp.dot(rb_ref[...].astype(BF16), wr_ref[...], preferred_element_type=F32)
    m = m + gs_ref[...].astype(F32) * jnp.dot(sb_ref[...].astype(BF16), ws_ref[...], preferred_element_type=F32)
    hid = x_ref[...] + jnp.dot(m.astype(BF16), wo_ref[...], preferred_element_type=F32)
    r = lax.rsqrt(jnp.mean(hid * hid, axis=-1, keepdims=True) + EPS_RMS)
    y_ref[...] = hid * r * fg_ref[...]


def _merge(x, ret_branch, sb_branch, p, w_ret, w_sb, w_out, final_gain, *, tm):
    m = x.shape[0]
    resident = functools.partial(pl.BlockSpec, index_map=lambda i: (0, 0), pipeline_mode=pl.Buffered(1))
    return pl.pallas_call(
        _merge_body,
        grid=(m // tm,),
        in_specs=[
            pl.BlockSpec((tm, D_MODEL), lambda i: (i, 0)),
            pl.BlockSpec((tm, W_V), lambda i: (i, 0)),
            pl.BlockSpec((tm, W_SB), lambda i: (i, 0)),
            pl.BlockSpec((tm, D_MODEL), lambda i: (i, COL_GR // D_MODEL)),
            pl.BlockSpec((tm, D_MODEL), lambda i: (i, COL_GS // D_MODEL)),
            resident((W_V, D_MODEL)),
            resident((W_SB, D_MODEL)),
            resident((D_MODEL, D_MODEL)),
            pl.BlockSpec((1, D_MODEL), lambda i: (0, 0)),
        ],
        out_specs=pl.BlockSpec((tm, D_MODEL), lambda i: (i, 0)),
        out_shape=jax.ShapeDtypeStruct((m, D_MODEL), F32),
        compiler_params=pltpu.CompilerParams(dimension_semantics=("arbitrary",), vmem_limit_bytes=VMEM_LIMIT),
        name="merge",
    )(x, ret_branch, sb_branch, p, p, w_ret, w_sb, w_out, final_gain)


def _rope_tables(pos):
    half = DK_RET // 2
    inv = ROPE_BASE ** (-np.arange(half, dtype=np.float64) / half)
    ang = np.asarray(pos, np.float64)[:, None] * inv[None, :]
    cos, sin = np.cos(ang), np.sin(ang)
    cos_full = np.concatenate([cos, cos], axis=-1)
    sin_signed = np.concatenate([-sin, sin], axis=-1)
    return cos_full.astype(np.float32), sin_signed.astype(np.float32)


def _layer(xp, xs, state, pool_k, pool_v, page_table, norm_gain, w_in, sb_bias, gn_gain, w_ret, w_sb, w_out, final_gain):
    batch, seq, _ = xp.shape
    dec_batch, t_len, _ = xs.shape
    n_pool, page = pool_k.shape[:2]
    past_len = page_table.shape[1] * page

    w_in_b = w_in.astype(BF16)
    w_ret_b, w_sb_b, w_out_b = w_ret.astype(BF16), w_sb.astype(BF16), w_out.astype(BF16)
    gain2 = norm_gain[None, :]
    gn_gain2 = gn_gain[None, :]
    final_gain2 = final_gain[None, :]

    x2p = xp.reshape(batch * seq, D_MODEL)
    x2s = xs.reshape(dec_batch * t_len, D_MODEL)
    tm_p, tm_s = 512, 512
    cos_p, sin_p = _rope_tables(np.arange(seq))
    cos_s, sin_s = _rope_tables(past_len + np.arange(t_len))
    cos_s, sin_s = np.tile(cos_s, (tm_s // t_len, 1)), np.tile(sin_s, (tm_s // t_len, 1))

    pp, skp, svp, skp_heads, svp_heads = _project(
        x2p, gain2, cos_p, sin_p, w_in_b, tm=tm_p, rope_period_tiles=seq // tm_p, p_dtype=BF16
    )
    ps, sks, svs, _, _ = _project(x2s, gain2, cos_s, sin_s, w_in_b, tm=tm_s, rope_period_tiles=1, p_dtype=F32)

    rbp, state_p = _ret_prompt(pp, gn_gain2, batch, seq)
    rbs, state_s = _ret_sample(ps, state, gn_gain2, dec_batch, t_len)

    sbp = _sb_prompt(pp, skp_heads, svp_heads, sb_bias, batch, seq)
    sbs = _sb_sample(
        page_table, ps, sks, svs, pool_k.reshape(n_pool, page * H_SB, D_SB), pool_v.reshape(n_pool, page * H_SB, D_SB),
        sb_bias, dec_batch, t_len,
    )

    yp = _merge(x2p, rbp, sbp, pp, w_ret_b, w_sb_b, w_out_b, final_gain2, tm=256)
    ys = _merge(x2s, rbs, sbs, ps, w_ret_b, w_sb_b, w_out_b, final_gain2, tm=256)

    return (
        yp.reshape(batch, seq, D_MODEL),
        ys.reshape(dec_batch, t_len, D_MODEL),
        state_p,
        skp.reshape(batch, seq, H_SB, D_SB),
        svp.reshape(batch, seq, H_SB, D_SB),
        state_s,
        sks.reshape(dec_batch, t_len, H_SB, D_SB),
        svs.reshape(dec_batch, t_len, H_SB, D_SB),
    )


def kernel(x_prompt, x_sample, state_ret, cache_sb_k, cache_sb_v, page_table, norm_gain, w_in, sb_bias, ret_gn_gain, w_ret_proj, w_sb_proj, w_out, final_norm_gain):
    depth = w_in.shape[0]
    assert depth == 1, "single-layer trunk"
    outs = _layer(
        x_prompt, x_sample, state_ret[0], cache_sb_k[0], cache_sb_v[0], page_table, norm_gain[0], w_in[0],
        sb_bias[0], ret_gn_gain[0], w_ret_proj[0], w_sb_proj[0], w_out[0], final_norm_gain,
    )
    yp, ys, state_p, skp, svp, state_s, sks, svs = outs
    return (yp, ys, state_p[None], skp[None], svp[None], state_s[None], sks[None], svs[None])
```

```python
import functools

import jax
import jax.numpy as jnp
import numpy as np
from jax import lax
from jax.experimental import pallas as pl
from jax.experimental.pallas import tpu as pltpu

F32 = jnp.float32
BF16 = jnp.bfloat16

D_MODEL = 2048
H_RET = 8
DK_RET = 128
DV_RET = 256
RET_CHUNK = 128
H_SB = 8
D_SB = 128
ROPE_BASE = 10000.0
EPS_RMS = 1e-6
EPS_GN = 1e-5
W_QK = H_RET * DK_RET
W_V = H_RET * DV_RET
W_SB = H_SB * D_SB

PROJ_TN = 1024
PROJ_GROUP = 512
PROJ_ROW_GROUPS = 2
P_COLS = 2 * W_QK + 2 * W_V + 2 * W_SB + 2 * D_MODEL
N_P_TILES = P_COLS // PROJ_TN
COL_Q, COL_K, COL_V, COL_G, COL_SQ, COL_SG, COL_GR, COL_GS = 0, 1024, 2048, 4096, 6144, 7168, 8192, 10240

LOG2E = float(np.log2(np.e))
SB_Q_SCALE = D_SB**-0.5 * LOG2E

VMEM_LIMIT = 56 * 1024 * 1024

_NT = (((1,), (1,)), ((), ()))


def _silu(g):
    return g * jax.nn.sigmoid(g)


def _proj_body(x_ref, gain_ref, cos_ref, sin_ref, w_ref, p_ref, sk_ref, sv_ref, skh_ref, svh_ref, xn_ref):
    j = pl.program_id(1)

    @pl.when(j == 0)
    def _norm():
        x = x_ref[...]
        r = lax.rsqrt(jnp.mean(x * x, axis=-1, keepdims=True) + EPS_RMS)
        xn_ref[...] = (x * r * gain_ref[...]).astype(BF16)

    tm = xn_ref.shape[0]
    rows_per_group = tm // PROJ_ROW_GROUPS

    def groups():
        for c in range(PROJ_TN // PROJ_GROUP):
            for r in range(PROJ_ROW_GROUPS):
                rows = slice(r * rows_per_group, (r + 1) * rows_per_group)
                cols = slice(c * PROJ_GROUP, (c + 1) * PROJ_GROUP)
                yield r * rows_per_group, c * PROJ_GROUP, jnp.dot(xn_ref[rows, :], w_ref[:, cols], preferred_element_type=F32)

    def lane_blocks(acc):
        for h in range(PROJ_GROUP // DK_RET):
            yield h * DK_RET, acc[:, h * DK_RET : (h + 1) * DK_RET]

    @pl.when(j < COL_V // PROJ_TN)
    def _rope():
        scale = jnp.where(j == COL_K // PROJ_TN, F32(DK_RET**-0.5), F32(1.0))
        for row0, col0, acc in groups():
            cos = cos_ref[row0 : row0 + rows_per_group, :] * scale
            sin = sin_ref[row0 : row0 + rows_per_group, :] * scale
            for off, xh in lane_blocks(acc):
                y = xh * cos + pltpu.roll(xh, DK_RET // 2, axis=1) * sin
                p_ref[row0 : row0 + rows_per_group, col0 + off : col0 + off + DK_RET] = y.astype(p_ref.dtype)

    @pl.when((j >= COL_V // PROJ_TN) & (j < COL_GR // PROJ_TN))
    def _plain():
        scale = jnp.where(j == COL_SQ // PROJ_TN, F32(SB_Q_SCALE), F32(1.0))
        for row0, col0, acc in groups():
            p_ref[row0 : row0 + rows_per_group, col0 : col0 + PROJ_GROUP] = (acc * scale).astype(p_ref.dtype)

    @pl.when((j >= COL_GR // PROJ_TN) & (j < N_P_TILES))
    def _gate():
        for row0, col0, acc in groups():
            y = 0.5 * jnp.tanh(0.5 * acc) + 0.5
            p_ref[row0 : row0 + rows_per_group, col0 : col0 + PROJ_GROUP] = y.astype(p_ref.dtype)

    def heads_out(tok_ref, head_ref):
        for row0, col0, acc in groups():
            for off, ah in lane_blocks(acc):
                head = (col0 + off) // D_SB
                tok_ref[pl.ds(row0 * H_SB + head, rows_per_group, stride=H_SB), :] = ah
                head_ref[head, row0 : row0 + rows_per_group, :] = ah.astype(BF16)

    @pl.when(j == N_P_TILES)
    def _sk():
        heads_out(sk_ref, skh_ref)

    @pl.when(j == N_P_TILES + 1)
    def _sv():
        heads_out(sv_ref, svh_ref)


def _w_tile(j):
    first_kv = COL_SQ // PROJ_TN + 1
    n_kv = 2 * W_SB // PROJ_TN
    return jnp.where(j < first_kv, j, jnp.where(j < N_P_TILES, j + n_kv, j - (N_P_TILES - first_kv)))


def _project(x, gain, cos, sin, w, *, tm, rope_period_tiles, p_dtype):
    m = x.shape[0]
    n_tiles = w.shape[1] // PROJ_TN
    return pl.pallas_call(
        _proj_body,
        grid=(m // tm, n_tiles),
        in_specs=[
            pl.BlockSpec((tm, D_MODEL), lambda i, j: (i, 0)),
            pl.BlockSpec((1, D_MODEL), lambda i, j: (0, 0)),
            pl.BlockSpec((tm, DK_RET), lambda i, j: (i % rope_period_tiles, 0)),
            pl.BlockSpec((tm, DK_RET), lambda i, j: (i % rope_period_tiles, 0)),
            pl.BlockSpec((D_MODEL, PROJ_TN), lambda i, j: (0, _w_tile(j))),
        ],
        out_specs=[
            pl.BlockSpec((tm, PROJ_TN), lambda i, j: (i, jnp.minimum(j, N_P_TILES - 1))),
            pl.BlockSpec((tm * H_SB, D_SB), lambda i, j: (i, 0)),
            pl.BlockSpec((tm * H_SB, D_SB), lambda i, j: (i, 0)),
            pl.BlockSpec((H_SB, tm, D_SB), lambda i, j: (0, i, 0)),
            pl.BlockSpec((H_SB, tm, D_SB), lambda i, j: (0, i, 0)),
        ],
        out_shape=[
            jax.ShapeDtypeStruct((m, P_COLS), p_dtype),
            jax.ShapeDtypeStruct((m * H_SB, D_SB), F32),
            jax.ShapeDtypeStruct((m * H_SB, D_SB), F32),
            jax.ShapeDtypeStruct((H_SB, m, D_SB), BF16),
            jax.ShapeDtypeStruct((H_SB, m, D_SB), BF16),
        ],
        scratch_shapes=[pltpu.VMEM((tm, D_MODEL), BF16)],
        compiler_params=pltpu.CompilerParams(
            dimension_semantics=("arbitrary", "arbitrary"), vmem_limit_bytes=VMEM_LIMIT
        ),
        name="proj",
    )(x, gain, cos, sin, w)


def _group_norm_gate(o, gain, g):
    mu = jnp.mean(o, axis=-1, keepdims=True)
    d = o - mu
    var = jnp.mean(d * d, axis=-1, keepdims=True)
    return d * lax.rsqrt(var + EPS_GN) * gain * _silu(g)


def _ret_prompt_body(q_ref, k_ref, v_ref, g_ref, gain_ref, decay_ref, cd_ref, kd_ref, gs_ref, out_ref, s_ref):
    c = pl.program_id(1)

    @pl.when(c == 0)
    def _init():
        s_ref[...] = jnp.zeros_like(s_ref)

    heads = range(H_RET)
    q = [q_ref[:, h * DK_RET : (h + 1) * DK_RET] for h in heads]
    k = [k_ref[:, h * DK_RET : (h + 1) * DK_RET] for h in heads]
    v = [v_ref[:, h * DV_RET : (h + 1) * DV_RET] for h in heads]
    s = [s_ref[0, h] for h in heads]
    inner = [lax.dot_general(q[h], k[h], _NT, preferred_element_type=F32) for h in heads]
    cross = [jnp.dot(q[h], s[h].astype(BF16), preferred_element_type=F32) for h in heads]
    for h in heads:
        kdec_t = (k[h].astype(F32) * kd_ref[:, h : h + 1]).T.astype(BF16)
        s_ref[0, h] = gs_ref[h] * s[h] + jnp.dot(kdec_t, v[h], preferred_element_type=F32)
    for h in heads:
        o = jnp.dot((inner[h] * decay_ref[h]).astype(BF16), v[h], preferred_element_type=F32)
        o = o + cross[h] * cd_ref[:, h : h + 1]
        g = g_ref[:, h * DV_RET : (h + 1) * DV_RET].astype(F32)
        gain = gain_ref[:, h * DV_RET : (h + 1) * DV_RET]
        out_ref[:, h * DV_RET : (h + 1) * DV_RET] = _group_norm_gate(o, gain, g).astype(out_ref.dtype)


def _ret_tables(chunk):
    lg = np.log1p(-np.exp2(-5.0 - np.arange(H_RET, dtype=np.float64)))
    idx = np.arange(chunk, dtype=np.float64)
    diff = idx[:, None] - idx[None, :]
    decay = np.where(diff >= 0, np.exp(lg[:, None, None] * np.maximum(diff, 0.0)[None]), 0.0)
    cross = np.exp(lg[:, None] * (idx + 1.0)[None, :]).T
    kdec = np.exp(lg[:, None] * (chunk - 1.0 - idx)[None, :]).T
    sdec = np.broadcast_to(np.exp(lg * chunk)[:, None, None], (H_RET, 1, DV_RET))
    return tuple(np.asarray(a, np.float32) for a in (decay, cross, kdec, sdec))


def _ret_prompt(p, gn_gain, batch, seq):
    nc = seq // RET_CHUNK
    decay, cross, kdec, sdec = _ret_tables(RET_CHUNK)
    row = lambda b, c: b * nc + c
    const2 = lambda b, c: (0, 0)
    const3 = lambda b, c: (0, 0, 0)
    return pl.pallas_call(
        _ret_prompt_body,
        grid=(batch, nc),
        in_specs=[
            pl.BlockSpec((RET_CHUNK, W_QK), lambda b, c: (row(b, c), COL_Q // W_QK)),
            pl.BlockSpec((RET_CHUNK, W_QK), lambda b, c: (row(b, c), COL_K // W_QK)),
            pl.BlockSpec((RET_CHUNK, W_V), lambda b, c: (row(b, c), COL_V // W_V)),
            pl.BlockSpec((RET_CHUNK, W_V), lambda b, c: (row(b, c), COL_G // W_V)),
            pl.BlockSpec((1, W_V), const2),
            pl.BlockSpec((H_RET, RET_CHUNK, RET_CHUNK), const3),
            pl.BlockSpec((RET_CHUNK, H_RET), const2),
            pl.BlockSpec((RET_CHUNK, H_RET), const2),
            pl.BlockSpec((H_RET, 1, DV_RET), const3),
        ],
        out_specs=[
            pl.BlockSpec((RET_CHUNK, W_V), lambda b, c: (row(b, c), 0)),
            pl.BlockSpec((1, H_RET, DK_RET, DV_RET), lambda b, c: (b, 0, 0, 0)),
        ],
        out_shape=[
            jax.ShapeDtypeStruct((batch * seq, W_V), BF16),
            jax.ShapeDtypeStruct((batch, H_RET, DK_RET, DV_RET), F32),
        ],
        compiler_params=pltpu.CompilerParams(
            dimension_semantics=("arbitrary", "arbitrary"), vmem_limit_bytes=VMEM_LIMIT
        ),
        name="ret_prompt",
    )(p, p, p, p, gn_gain, decay, cross, kdec, sdec)


RS_SEQS = 16


def _ret_sample_body(q_ref, k_ref, v_ref, g_ref, s_ref, gain_ref, decay_ref, cd_ref, kd_ref, gs_ref, out_ref, snew_ref):
    n_seq = s_ref.shape[0]
    rows = q_ref.shape[0]
    t_len = rows // n_seq
    q = q_ref[...]
    k = k_ref[...]
    v = v_ref[...].astype(BF16)
    s = s_ref[...]
    inner = lax.dot_general(q.astype(BF16), k.astype(BF16), _NT, preferred_element_type=F32) * decay_ref[...]
    o = jnp.dot(inner.astype(BF16), v, preferred_element_type=F32)
    row_seq = lax.broadcasted_iota(jnp.int32, (rows, DK_RET), 0) // t_len
    q_exp = jnp.concatenate([jnp.where(row_seq == b, q, 0.0) for b in range(n_seq)], axis=1).astype(BF16)
    s_rows = s.reshape(n_seq * DK_RET, DV_RET).astype(BF16)
    o = o + jnp.dot(q_exp, s_rows, preferred_element_type=F32) * cd_ref[...]
    kdec_t = (k * kd_ref[...]).T
    col_seq = lax.broadcasted_iota(jnp.int32, (DK_RET, rows), 1) // t_len
    k_exp_t = jnp.concatenate([jnp.where(col_seq == b, kdec_t, 0.0) for b in range(n_seq)], axis=0).astype(BF16)
    upd = jnp.dot(k_exp_t, v, preferred_element_type=F32)
    snew_ref[...] = gs_ref[...] * s + upd.reshape(n_seq, DK_RET, DV_RET)
    out_ref[...] = _group_norm_gate(o, gain_ref[...], g_ref[...])


def _ret_sample(p, state, gn_gain, batch, t_len):
    decay, cross, kdec, sdec = _ret_tables(t_len)
    rows = RS_SEQS * t_len
    decay_bd = np.stack([np.kron(np.eye(RS_SEQS, dtype=np.float32), decay[h]) for h in range(H_RET)])
    cross_rows = np.tile(cross.T, (1, RS_SEQS))[:, :, None]
    kdec_rows = np.tile(kdec.T, (1, RS_SEQS))[:, :, None]
    qk_blk = lambda col: pl.BlockSpec((rows, DK_RET), lambda i, h: (i, col // DK_RET + h))
    v_blk = lambda col: pl.BlockSpec((rows, DV_RET), lambda i, h: (i, col // DV_RET + h))
    state_blk = pl.BlockSpec((RS_SEQS, None, DK_RET, DV_RET), lambda i, h: (i, h, 0, 0))
    return pl.pallas_call(
        _ret_sample_body,
        grid=(batch // RS_SEQS, H_RET),
        in_specs=[
            qk_blk(COL_Q),
            qk_blk(COL_K),
            v_blk(COL_V),
            v_blk(COL_G),
            state_blk,
            pl.BlockSpec((1, DV_RET), lambda i, h: (0, h)),
            pl.BlockSpec((None, rows, rows), lambda i, h: (h, 0, 0)),
            pl.BlockSpec((None, rows, 1), lambda i, h: (h, 0, 0)),
            pl.BlockSpec((None, rows, 1), lambda i, h: (h, 0, 0)),
            pl.BlockSpec((None, 1, DV_RET), lambda i, h: (h, 0, 0)),
        ],
        out_specs=[pl.BlockSpec((rows, DV_RET), lambda i, h: (i, h)), state_blk],
        out_shape=[
            jax.ShapeDtypeStruct((batch * t_len, W_V), F32),
            jax.ShapeDtypeStruct((batch, H_RET, DK_RET, DV_RET), F32),
        ],
        compiler_params=pltpu.CompilerParams(
            dimension_semantics=("arbitrary", "arbitrary"), vmem_limit_bytes=VMEM_LIMIT
        ),
        name="ret_sample",
    )(p, p, p, p, state, gn_gain, decay_bd, cross_rows, kdec_rows, sdec)


def _sb_weights(qk2, tri, bias2, carry, mask):
    drop, log2_beta = _sb_drop(qk2, bias2, mask)
    return _sb_finish(drop, log2_beta, _sb_after(drop, tri), carry, mask)


def _sb_drop(qk2, bias2, mask):
    z2 = qk2 + bias2
    drop = jnp.maximum(z2, 0.0) + jnp.log2(1.0 + jnp.exp2(-jnp.abs(z2)))
    log2_beta = z2 - drop
    if mask is not None:
        drop = jnp.where(mask, drop, 0.0)
    return drop, log2_beta


def _sb_after(drop, tri):
    hi = drop.astype(BF16)
    lo = (drop - hi.astype(F32)).astype(BF16)
    return jnp.dot(hi, tri, preferred_element_type=F32) + jnp.dot(lo, tri, preferred_element_type=F32)


def _sb_finish(drop, log2_beta, after, carry, mask):
    a = jnp.exp2(log2_beta - after - carry)
    if mask is not None:
        a = jnp.where(mask, a, 0.0)
    return a, carry + after[:, 0:1] + drop[:, 0:1]


def _tri_np(n):
    return np.tril(np.ones((n, n), np.float32), -1)


SBP_T = 256
SBP_HEADS = 8


def _sb_prompt_body(bias_ref, q_ref, k_ref, v_ref, sg_ref, tri_ref, out_ref):
    hg = pl.program_id(1)
    i = pl.program_id(2)
    heads = range(SBP_HEADS)
    bias2 = [bias_ref[hg * SBP_HEADS + g] for g in heads]
    tri = tri_ref[...]
    q = [q_ref[:, g * D_SB : (g + 1) * D_SB] for g in heads]
    r = lax.broadcasted_iota(jnp.int32, (SBP_T, SBP_T), 0)
    c = lax.broadcasted_iota(jnp.int32, (SBP_T, SBP_T), 1)

    def tile(kb, state, mask):
        off = pl.multiple_of(kb * SBP_T, SBP_T)
        qk2 = [lax.dot_general(q[g], k_ref[g, pl.ds(off, SBP_T), :], _NT, preferred_element_type=F32) for g in heads]
        dl = [_sb_drop(qk2[g], bias2[g], mask) for g in heads]
        after = [_sb_after(dl[g][0], tri) for g in heads]
        new_state = []
        for g in heads:
            carry, acc = state[g]
            a, carry = _sb_finish(dl[g][0], dl[g][1], after[g], carry, mask)
            acc = acc + jnp.dot(a.astype(BF16), v_ref[g, pl.ds(off, SBP_T), :], preferred_element_type=F32)
            new_state.append((carry, acc))
        return tuple(new_state)

    state = tuple((jnp.zeros((SBP_T, 1), F32), jnp.zeros((SBP_T, D_SB), F32)) for _ in heads)
    state = tile(i, state, c < r)
    state = lax.fori_loop(0, i, lambda n, st: tile(i - 1 - n, st, None), state)
    acc = jnp.concatenate([st[1] for st in state], axis=1)
    out_ref[...] = (acc * _silu(sg_ref[...].astype(F32))).astype(out_ref.dtype)


def _sb_prompt(p, sk_heads, sv_heads, sb_bias, batch, seq):
    nq = seq // SBP_T
    width = SBP_HEADS * D_SB
    return pl.pallas_call(
        _sb_prompt_body,
        grid=(batch, H_SB // SBP_HEADS, nq),
        in_specs=[
            pl.BlockSpec(memory_space=pltpu.SMEM),
            pl.BlockSpec((SBP_T, width), lambda b, h, i: (b * nq + i, COL_SQ // width + h)),
            pl.BlockSpec((SBP_HEADS, seq, D_SB), lambda b, h, i: (h, b, 0)),
            pl.BlockSpec((SBP_HEADS, seq, D_SB), lambda b, h, i: (h, b, 0)),
            pl.BlockSpec((SBP_T, width), lambda b, h, i: (b * nq + i, COL_SG // width + h)),
            pl.BlockSpec((SBP_T, SBP_T), lambda b, h, i: (0, 0)),
        ],
        out_specs=pl.BlockSpec((SBP_T, width), lambda b, h, i: (b * nq + i, h)),
        out_shape=jax.ShapeDtypeStruct((batch * seq, W_SB), BF16),
        compiler_params=pltpu.CompilerParams(
            dimension_semantics=("arbitrary", "arbitrary", "arbitrary"), vmem_limit_bytes=VMEM_LIMIT
        ),
        name="sb_prompt",
    )(sb_bias * LOG2E, p, sk_heads, sv_heads, p, jnp.asarray(_tri_np(SBP_T), BF16))


SBS_CHUNK_PAGES = 4


SBS_SLOTS = 4


def _sb_sample_body(
    pt_ref, bias_ref, hm_ref, tri_ref, q_ref, kn_ref, vn_ref, sg_ref, kpool_ref, vpool_ref,
    out_ref, kbuf_ref, vbuf_ref, sem_ref, kd0_ref, kd1_ref, vd0_ref, vd1_ref, knd_ref, vnd_ref, kstage_ref, vstage_ref,
    *, chunks_per_seq,
):
    b = pl.program_id(0)
    n_seq = pl.num_programs(0)
    t_len = q_ref.shape[0]
    rows = H_SB * t_len
    page = kbuf_ref.shape[2] // H_SB
    chunk = SBS_CHUNK_PAGES * page
    lookahead = SBS_SLOTS - 1
    kd_refs, vd_refs = (kd0_ref, kd1_ref), (vd0_ref, vd1_ref)

    def chunk_copies(n, slot):
        copies = []
        for pg in range(SBS_CHUNK_PAGES):
            page_id = pt_ref[n * SBS_CHUNK_PAGES + pg]
            copies.append(pltpu.make_async_copy(kpool_ref.at[page_id], kbuf_ref.at[slot, pg], sem_ref.at[0, slot]))
            copies.append(pltpu.make_async_copy(vpool_ref.at[page_id], vbuf_ref.at[slot, pg], sem_ref.at[1, slot]))
        return copies

    def densify(src_ref, dst_ref, row0, n_keys):
        for h in range(H_SB):
            dst_ref[row0 : row0 + n_keys, h * D_SB : (h + 1) * D_SB] = src_ref[pl.ds(h, n_keys, stride=H_SB), :].astype(
                dst_ref.dtype
            )

    @pl.when(b == 0)
    def _prime():
        for n in range(lookahead):
            for cp in chunk_copies(n, n % SBS_SLOTS):
                cp.start()

    q = q_ref[...]
    qbd = jnp.concatenate([q * hm_ref[h : h + 1, :] for h in range(H_SB)], axis=0).astype(BF16)
    bias2 = bias_ref[...]

    def sweep(k, v, tri, carry, acc, mask):
        qk2 = lax.dot_general(qbd, k, _NT, preferred_element_type=F32)
        a, carry = _sb_weights(qk2, tri, bias2, carry, mask)
        return carry, acc + jnp.dot(a.astype(BF16), v, preferred_element_type=F32)

    for src_ref, stage_ref, dst_ref in ((kn_ref, kstage_ref, knd_ref), (vn_ref, vstage_ref, vnd_ref)):
        stage_ref[...] = jnp.zeros_like(stage_ref)
        densify(src_ref, stage_ref, 0, t_len)
        dst_ref[...] = stage_ref[...].astype(BF16)
    r = lax.broadcasted_iota(jnp.int32, (rows, page), 0)
    c = lax.broadcasted_iota(jnp.int32, (rows, page), 1)
    carry, acc = sweep(
        knd_ref[...], vnd_ref[...], tri_ref[0:page, 0:page],
        jnp.zeros((rows, 1), F32), jnp.zeros((rows, W_SB), F32), c < (r % t_len),
    )

    for c in range(chunks_per_seq):
        n = b * chunks_per_seq + c
        slot = c % SBS_SLOTS
        for cp in chunk_copies(n, slot):
            cp.wait()
        for cp in chunk_copies(n + lookahead, (c + lookahead) % SBS_SLOTS):
            cp.start()
        kd_ref, vd_ref = kd_refs[c % 2], vd_refs[c % 2]
        for pg in range(SBS_CHUNK_PAGES):
            densify(kbuf_ref.at[slot, pg], kd_ref, pg * page, page)
            densify(vbuf_ref.at[slot, pg], vd_ref, pg * page, page)
        carry, acc = sweep(kd_ref[...], vd_ref[...], tri_ref[...], carry, acc, None)

    o = jnp.concatenate([acc[h * t_len : (h + 1) * t_len, h * D_SB : (h + 1) * D_SB] for h in range(H_SB)], axis=1)
    out_ref[...] = o * _silu(sg_ref[...])

    @pl.when(b == n_seq - 1)
    def _drain():
        for i in range(lookahead):
            for cp in chunk_copies(n_seq * chunks_per_seq + i, i % SBS_SLOTS):
                cp.wait()


def _sb_sample(page_table, p, sk, sv, pool_k, pool_v, sb_bias, batch, t_len):
    n_pages = page_table.shape[1]
    chunks_per_seq = n_pages // SBS_CHUNK_PAGES
    assert n_pages % SBS_CHUNK_PAGES == 0 and chunks_per_seq % SBS_SLOTS == 0
    page_rows = pool_k.shape[1]
    rows = H_SB * t_len
    page = page_rows // H_SB
    chunk = SBS_CHUNK_PAGES * page
    bias_rows = jnp.repeat(sb_bias * LOG2E, t_len)[:, None]
    head_mask = np.repeat(np.eye(H_SB, dtype=np.float32), D_SB, axis=1)
    tri = jnp.asarray(_tri_np(chunk), BF16)
    fetch = page_table.reshape(batch, chunks_per_seq, SBS_CHUNK_PAGES)[:, ::-1, :].reshape(-1)
    fetch = jnp.concatenate([fetch, fetch[: (SBS_SLOTS - 1) * SBS_CHUNK_PAGES]])

    const2 = lambda b, pt: (0, 0)
    grid_spec = pltpu.PrefetchScalarGridSpec(
        num_scalar_prefetch=1,
        grid=(batch,),
        in_specs=[
            pl.BlockSpec((rows, 1), const2),
            pl.BlockSpec((H_SB, W_SB), const2),
            pl.BlockSpec((chunk, chunk), const2),
            pl.BlockSpec((t_len, W_SB), lambda b, pt: (b, COL_SQ // W_SB)),
            pl.BlockSpec((rows, D_SB), lambda b, pt: (b, 0)),
            pl.BlockSpec((rows, D_SB), lambda b, pt: (b, 0)),
            pl.BlockSpec((t_len, W_SB), lambda b, pt: (b, COL_SG // W_SB)),
            pl.BlockSpec(memory_space=pl.ANY),
            pl.BlockSpec(memory_space=pl.ANY),
        ],
        out_specs=pl.BlockSpec((t_len, W_SB), lambda b, pt: (b, 0)),
        scratch_shapes=[
            pltpu.VMEM((SBS_SLOTS, SBS_CHUNK_PAGES, page_rows, D_SB), F32),
            pltpu.VMEM((SBS_SLOTS, SBS_CHUNK_PAGES, page_rows, D_SB), F32),
            pltpu.SemaphoreType.DMA((2, SBS_SLOTS)),
            pltpu.VMEM((chunk, W_SB), BF16),
            pltpu.VMEM((chunk, W_SB), BF16),
            pltpu.VMEM((chunk, W_SB), BF16),
            pltpu.VMEM((chunk, W_SB), BF16),
            pltpu.VMEM((page, W_SB), BF16),
            pltpu.VMEM((page, W_SB), BF16),
            pltpu.VMEM((page, W_SB), F32),
            pltpu.VMEM((page, W_SB), F32),
        ],
    )
    return pl.pallas_call(
        functools.partial(_sb_sample_body, chunks_per_seq=chunks_per_seq),
        grid_spec=grid_spec,
        out_shape=jax.ShapeDtypeStruct((batch * t_len, W_SB), F32),
        compiler_params=pltpu.CompilerParams(dimension_semantics=("arbitrary",), vmem_limit_bytes=VMEM_LIMIT),
        name="sb_sample",
    )(fetch, bias_rows, head_mask, tri, p, sk, sv, p, pool_k, pool_v)


def _merge_body(x_ref, rb_ref, sb_ref, gr_ref, gs_ref, wr_ref, ws_ref, wo_ref, fg_ref, y_ref):
    m = gr_ref[...].astype(F32) * jnp.dot(rb_ref[...].astype(BF16), wr_ref[...], preferred_element_type=F32)
    m = m + gs_ref[...].astype(F32) * jnp.dot(sb_ref[...].astype(BF16), ws_ref[...], preferred_element_type=F32)
    hid = x_ref[...] + jnp.dot(m.astype(BF16), wo_ref[...], preferred_element_type=F32)
    r = lax.rsqrt(jnp.mean(hid * hid, axis=-1, keepdims=True) + EPS_RMS)
    y_ref[...] = hid * r * fg_ref[...]


def _merge(x, ret_branch, sb_branch, p, w_ret, w_sb, w_out, final_gain, *, tm):
    m = x.shape[0]
    resident = functools.partial(pl.BlockSpec, index_map=lambda i: (0, 0), pipeline_mode=pl.Buffered(1))
    return pl.pallas_call(
        _merge_body,
        grid=(m // tm,),
        in_specs=[
            pl.BlockSpec((tm, D_MODEL), lambda i: (i, 0)),
            pl.BlockSpec((tm, W_V), lambda i: (i, 0)),
            pl.BlockSpec((tm, W_SB), lambda i: (i, 0)),
            pl.BlockSpec((tm, D_MODEL), lambda i: (i, COL_GR // D_MODEL)),
            pl.BlockSpec((tm, D_MODEL), lambda i: (i, COL_GS // D_MODEL)),
            resident((W_V, D_MODEL)),
            resident((W_SB, D_MODEL)),
            resident((D_MODEL, D_MODEL)),
            pl.BlockSpec((1, D_MODEL), lambda i: (0, 0)),
        ],
        out_specs=pl.BlockSpec((tm, D_MODEL), lambda i: (i, 0)),
        out_shape=jax.ShapeDtypeStruct((m, D_MODEL), F32),
        compiler_params=pltpu.CompilerParams(dimension_semantics=("arbitrary",), vmem_limit_bytes=VMEM_LIMIT),
        name="merge",
    )(x, ret_branch, sb_branch, p, p, w_ret, w_sb, w_out, final_gain)


def _rope_tables(pos):
    half = DK_RET // 2
    inv = ROPE_BASE ** (-np.arange(half, dtype=np.float64) / half)
    ang = np.asarray(pos, np.float64)[:, None] * inv[None, :]
    cos, sin = np.cos(ang), np.sin(ang)
    cos_full = np.concatenate([cos, cos], axis=-1)
    sin_signed = np.concatenate([-sin, sin], axis=-1)
    return cos_full.astype(np.float32), sin_signed.astype(np.float32)


def _layer(xp, xs, state, pool_k, pool_v, page_table, norm_gain, w_in, sb_bias, gn_gain, w_ret, w_sb, w_out, final_gain):
    batch, seq, _ = xp.shape
    dec_batch, t_len, _ = xs.shape
    n_pool, page = pool_k.shape[:2]
    past_len = page_table.shape[1] * page

    w_in_b = w_in.astype(BF16)
    w_ret_b, w_sb_b, w_out_b = w_ret.astype(BF16), w_sb.astype(BF16), w_out.astype(BF16)
    gain2 = norm_gain[None, :]
    gn_gain2 = gn_gain[None, :]
    final_gain2 = final_gain[None, :]

    x2p = xp.reshape(batch * seq, D_MODEL)
    x2s = xs.reshape(dec_batch * t_len, D_MODEL)
    tm_p, tm_s = 512, 512
    cos_p, sin_p = _rope_tables(np.arange(seq))
    cos_s, sin_s = _rope_tables(past_len + np.arange(t_len))
    cos_s, sin_s = np.tile(cos_s, (tm_s // t_len, 1)), np.tile(sin_s, (tm_s // t_len, 1))

    pp, skp, svp, skp_heads, svp_heads = _project(
        x2p, gain2, cos_p, sin_p, w_in_b, tm=tm_p, rope_period_tiles=seq // tm_p, p_dtype=BF16
    )
    ps, sks, svs, _, _ = _project(x2s, gain2, cos_s, sin_s, w_in_b, tm=tm_s, rope_period_tiles=1, p_dtype=F32)

    rbp, state_p = _ret_prompt(pp, gn_gain2, batch, seq)
    rbs, state_s = _ret_sample(ps, state, gn_gain2, dec_batch, t_len)

    sbp = _sb_prompt(pp, skp_heads, svp_heads, sb_bias, batch, seq)
    sbs = _sb_sample(
        page_table, ps, sks, svs, pool_k.reshape(n_pool, page * H_SB, D_SB), pool_v.reshape(n_pool, page * H_SB, D_SB),
        sb_bias, dec_batch, t_len,
    )

    yp = _merge(x2p, rbp, sbp, pp, w_ret_b, w_sb_b, w_out_b, final_gain2, tm=256)
    ys = _merge(x2s, rbs, sbs, ps, w_ret_b, w_sb_b, w_out_b, final_gain2, tm=256)

    return (
        yp.reshape(batch, seq, D_MODEL),
        ys.reshape(dec_batch, t_len, D_MODEL),
        state_p,
        skp.reshape(batch, seq, H_SB, D_SB),
        svp.reshape(batch, seq, H_SB, D_SB),
        state_s,
        sks.reshape(dec_batch, t_len, H_SB, D_SB),
        svs.reshape(dec_batch, t_len, H_SB, D_SB),
    )


def kernel(x_prompt, x_sample, state_ret, cache_sb_k, cache_sb_v, page_table, norm_gain, w_in, sb_bias, ret_gn_gain, w_ret_proj, w_sb_proj, w_out, final_norm_gain):
    depth = w_in.shape[0]
    assert depth == 1, "single-layer trunk"
    outs = _layer(
        x_prompt, x_sample, state_ret[0], cache_sb_k[0], cache_sb_v[0], page_table, norm_gain[0], w_in[0],
        sb_bias[0], ret_gn_gain[0], w_ret_proj[0], w_sb_proj[0], w_out[0], final_norm_gain,
    )
    yp, ys, state_p, skp, svp, state_s, sks, svs = outs
    return (yp, ys, state_p[None], skp[None], svp[None], state_s[None], sks[None], svs[None])
```

```python
import functools

import jax
import jax.numpy as jnp
import numpy as np
from jax import lax
from jax.experimental import pallas as pl
from jax.experimental.pallas import tpu as pltpu

F32 = jnp.float32
BF16 = jnp.bfloat16

D_MODEL = 2048
H_RET = 8
DK_RET = 128
DV_RET = 256
RET_CHUNK = 128
H_SB = 8
D_SB = 128
ROPE_BASE = 10000.0
EPS_RMS = 1e-6
EPS_GN = 1e-5
W_QK = H_RET * DK_RET
W_V = H_RET * DV_RET
W_SB = H_SB * D_SB

PROJ_TN = 1024
PROJ_GROUP = 512
PROJ_ROW_GROUPS = 2
P_COLS = 2 * W_QK + 2 * W_V + 2 * W_SB + 2 * D_MODEL
N_P_TILES = P_COLS // PROJ_TN
COL_Q, COL_K, COL_V, COL_G, COL_SQ, COL_SG, COL_GR, COL_GS = 0, 1024, 2048, 4096, 6144, 7168, 8192, 10240

LOG2E = float(np.log2(np.e))
SB_Q_SCALE = D_SB**-0.5 * LOG2E

VMEM_LIMIT = 56 * 1024 * 1024

_NT = (((1,), (1,)), ((), ()))


def _silu(g):
    return g * jax.nn.sigmoid(g)


def _proj_body(x_ref, gain_ref, cos_ref, sin_ref, w_ref, p_ref, sk_ref, sv_ref, skh_ref, svh_ref, xn_ref):
    j = pl.program_id(1)

    @pl.when(j == 0)
    def _norm():
        x = x_ref[...]
        r = lax.rsqrt(jnp.mean(x * x, axis=-1, keepdims=True) + EPS_RMS)
        xn_ref[...] = (x * r * gain_ref[...]).astype(BF16)

    tm = xn_ref.shape[0]
    rows_per_group = tm // PROJ_ROW_GROUPS

    def groups():
        for c in range(PROJ_TN // PROJ_GROUP):
            for r in range(PROJ_ROW_GROUPS):
                rows = slice(r * rows_per_group, (r + 1) * rows_per_group)
                cols = slice(c * PROJ_GROUP, (c + 1) * PROJ_GROUP)
                yield r * rows_per_group, c * PROJ_GROUP, jnp.dot(xn_ref[rows, :], w_ref[:, cols], preferred_element_type=F32)

    def lane_blocks(acc):
        for h in range(PROJ_GROUP // DK_RET):
            yield h * DK_RET, acc[:, h * DK_RET : (h + 1) * DK_RET]

    @pl.when(j < COL_V // PROJ_TN)
    def _rope():
        scale = jnp.where(j == COL_K // PROJ_TN, F32(DK_RET**-0.5), F32(1.0))
        for row0, col0, acc in groups():
            cos = cos_ref[row0 : row0 + rows_per_group, :] * scale
            sin = sin_ref[row0 : row0 + rows_per_group, :] * scale
            for off, xh in lane_blocks(acc):
                y = xh * cos + pltpu.roll(xh, DK_RET // 2, axis=1) * sin
                p_ref[row0 : row0 + rows_per_group, col0 + off : col0 + off + DK_RET] = y.astype(p_ref.dtype)

    @pl.when((j >= COL_V // PROJ_TN) & (j < COL_GR // PROJ_TN))
    def _plain():
        scale = jnp.where(j == COL_SQ // PROJ_TN, F32(SB_Q_SCALE), F32(1.0))
        for row0, col0, acc in groups():
            p_ref[row0 : row0 + rows_per_group, col0 : col0 + PROJ_GROUP] = (acc * scale).astype(p_ref.dtype)

    @pl.when((j >= COL_GR // PROJ_TN) & (j < N_P_TILES))
    def _gate():
        for row0, col0, acc in groups():
            y = 0.5 * jnp.tanh(0.5 * acc) + 0.5
            p_ref[row0 : row0 + rows_per_group, col0 : col0 + PROJ_GROUP] = y.astype(p_ref.dtype)

    def heads_out(tok_ref, head_ref):
        for row0, col0, acc in groups():
            for off, ah in lane_blocks(acc):
                head = (col0 + off) // D_SB
                tok_ref[pl.ds(row0 * H_SB + head, rows_per_group, stride=H_SB), :] = ah
                head_ref[head, row0 : row0 + rows_per_group, :] = ah.astype(BF16)

    @pl.when(j == N_P_TILES)
    def _sk():
        heads_out(sk_ref, skh_ref)

    @pl.when(j == N_P_TILES + 1)
    def _sv():
        heads_out(sv_ref, svh_ref)


def _w_tile(j):
    first_kv = COL_SQ // PROJ_TN + 1
    n_kv = 2 * W_SB // PROJ_TN
    return jnp.where(j < first_kv, j, jnp.where(j < N_P_TILES, j + n_kv, j - (N_P_TILES - first_kv)))


def _project(x, gain, cos, sin, w, *, tm, rope_period_tiles, p_dtype):
    m = x.shape[0]
    n_tiles = w.shape[1] // PROJ_TN
    return pl.pallas_call(
        _proj_body,
        grid=(m // tm, n_tiles),
        in_specs=[
            pl.BlockSpec((tm, D_MODEL), lambda i, j: (i, 0)),
            pl.BlockSpec((1, D_MODEL), lambda i, j: (0, 0)),
            pl.BlockSpec((tm, DK_RET), lambda i, j: (i % rope_period_tiles, 0)),
            pl.BlockSpec((tm, DK_RET), lambda i, j: (i % rope_period_tiles, 0)),
            pl.BlockSpec((D_MODEL, PROJ_TN), lambda i, j: (0, _w_tile(j))),
        ],
        out_specs=[
            pl.BlockSpec((tm, PROJ_TN), lambda i, j: (i, jnp.minimum(j, N_P_TILES - 1))),
            pl.BlockSpec((tm * H_SB, D_SB), lambda i, j: (i, 0)),
            pl.BlockSpec((tm * H_SB, D_SB), lambda i, j: (i, 0)),
            pl.BlockSpec((H_SB, tm, D_SB), lambda i, j: (0, i, 0)),
            pl.BlockSpec((H_SB, tm, D_SB), lambda i, j: (0, i, 0)),
        ],
        out_shape=[
            jax.ShapeDtypeStruct((m, P_COLS), p_dtype),
            jax.ShapeDtypeStruct((m * H_SB, D_SB), F32),
            jax.ShapeDtypeStruct((m * H_SB, D_SB), F32),
            jax.ShapeDtypeStruct((H_SB, m, D_SB), BF16),
            jax.ShapeDtypeStruct((H_SB, m, D_SB), BF16),
        ],
        scratch_shapes=[pltpu.VMEM((tm, D_MODEL), BF16)],
        compiler_params=pltpu.CompilerParams(
            dimension_semantics=("arbitrary", "arbitrary"), vmem_limit_bytes=VMEM_LIMIT
        ),
        name="proj",
    )(x, gain, cos, sin, w)


def _group_norm_gate(o, gain, g):
    mu = jnp.mean(o, axis=-1, keepdims=True)
    d = o - mu
    var = jnp.mean(d * d, axis=-1, keepdims=True)
    return d * lax.rsqrt(var + EPS_GN) * gain * _silu(g)


def _ret_prompt_body(q_ref, k_ref, v_ref, g_ref, gain_ref, decay_ref, cd_ref, kd_ref, gs_ref, out_ref, s_ref):
    c = pl.program_id(1)

    @pl.when(c == 0)
    def _init():
        s_ref[...] = jnp.zeros_like(s_ref)

    heads = range(H_RET)
    q = [q_ref[:, h * DK_RET : (h + 1) * DK_RET] for h in heads]
    k = [k_ref[:, h * DK_RET : (h + 1) * DK_RET] for h in heads]
    v = [v_ref[:, h * DV_RET : (h + 1) * DV_RET] for h in heads]
    s = [s_ref[0, h] for h in heads]
    inner = [lax.dot_general(q[h], k[h], _NT, preferred_element_type=F32) for h in heads]
    cross = [jnp.dot(q[h], s[h].astype(BF16), preferred_element_type=F32) for h in heads]
    for h in heads:
        kdec_t = (k[h].astype(F32) * kd_ref[:, h : h + 1]).T.astype(BF16)
        s_ref[0, h] = gs_ref[h] * s[h] + jnp.dot(kdec_t, v[h], preferred_element_type=F32)
    for h in heads:
        o = jnp.dot((inner[h] * decay_ref[h]).astype(BF16), v[h], preferred_element_type=F32)
        o = o + cross[h] * cd_ref[:, h : h + 1]
        g = g_ref[:, h * DV_RET : (h + 1) * DV_RET].astype(F32)
        gain = gain_ref[:, h * DV_RET : (h + 1) * DV_RET]
        out_ref[:, h * DV_RET : (h + 1) * DV_RET] = _group_norm_gate(o, gain, g).astype(out_ref.dtype)


def _ret_tables(chunk):
    lg = np.log1p(-np.exp2(-5.0 - np.arange(H_RET, dtype=np.float64)))
    idx = np.arange(chunk, dtype=np.float64)
    diff = idx[:, None] - idx[None, :]
    decay = np.where(diff >= 0, np.exp(lg[:, None, None] * np.maximum(diff, 0.0)[None]), 0.0)
    cross = np.exp(lg[:, None] * (idx + 1.0)[None, :]).T
    kdec = np.exp(lg[:, None] * (chunk - 1.0 - idx)[None, :]).T
    sdec = np.broadcast_to(np.exp(lg * chunk)[:, None, None], (H_RET, 1, DV_RET))
    return tuple(np.asarray(a, np.float32) for a in (decay, cross, kdec, sdec))


def _ret_prompt(p, gn_gain, batch, seq):
    nc = seq // RET_CHUNK
    decay, cross, kdec, sdec = _ret_tables(RET_CHUNK)
    row = lambda b, c: b * nc + c
    const2 = lambda b, c: (0, 0)
    const3 = lambda b, c: (0, 0, 0)
    return pl.pallas_call(
        _ret_prompt_body,
        grid=(batch, nc),
        in_specs=[
            pl.BlockSpec((RET_CHUNK, W_QK), lambda b, c: (row(b, c), COL_Q // W_QK)),
            pl.BlockSpec((RET_CHUNK, W_QK), lambda b, c: (row(b, c), COL_K // W_QK)),
            pl.BlockSpec((RET_CHUNK, W_V), lambda b, c: (row(b, c), COL_V // W_V)),
            pl.BlockSpec((RET_CHUNK, W_V), lambda b, c: (row(b, c), COL_G // W_V)),
            pl.BlockSpec((1, W_V), const2),
            pl.BlockSpec((H_RET, RET_CHUNK, RET_CHUNK), const3),
            pl.BlockSpec((RET_CHUNK, H_RET), const2),
            pl.BlockSpec((RET_CHUNK, H_RET), const2),
            pl.BlockSpec((H_RET, 1, DV_RET), const3),
        ],
        out_specs=[
            pl.BlockSpec((RET_CHUNK, W_V), lambda b, c: (row(b, c), 0)),
            pl.BlockSpec((1, H_RET, DK_RET, DV_RET), lambda b, c: (b, 0, 0, 0)),
        ],
        out_shape=[
            jax.ShapeDtypeStruct((batch * seq, W_V), BF16),
            jax.ShapeDtypeStruct((batch, H_RET, DK_RET, DV_RET), F32),
        ],
        compiler_params=pltpu.CompilerParams(
            dimension_semantics=("arbitrary", "arbitrary"), vmem_limit_bytes=VMEM_LIMIT
        ),
        name="ret_prompt",
    )(p, p, p, p, gn_gain, decay, cross, kdec, sdec)


RS_SEQS = 16


def _ret_sample_body(q_ref, k_ref, v_ref, g_ref, s_ref, gain_ref, decay_ref, cd_ref, kd_ref, gs_ref, out_ref, snew_ref):
    n_seq = s_ref.shape[0]
    rows = q_ref.shape[0]
    t_len = rows // n_seq
    q = q_ref[...]
    k = k_ref[...]
    v = v_ref[...].astype(BF16)
    s = s_ref[...]
    inner = lax.dot_general(q.astype(BF16), k.astype(BF16), _NT, preferred_element_type=F32) * decay_ref[...]
    o = jnp.dot(inner.astype(BF16), v, preferred_element_type=F32)
    row_seq = lax.broadcasted_iota(jnp.int32, (rows, DK_RET), 0) // t_len
    q_exp = jnp.concatenate([jnp.where(row_seq == b, q, 0.0) for b in range(n_seq)], axis=1).astype(BF16)
    s_rows = s.reshape(n_seq * DK_RET, DV_RET).astype(BF16)
    o = o + jnp.dot(q_exp, s_rows, preferred_element_type=F32) * cd_ref[...]
    kdec_t = (k * kd_ref[...]).T
    col_seq = lax.broadcasted_iota(jnp.int32, (DK_RET, rows), 1) // t_len
    k_exp_t = jnp.concatenate([jnp.where(col_seq == b, kdec_t, 0.0) for b in range(n_seq)], axis=0).astype(BF16)
    upd = jnp.dot(k_exp_t, v, preferred_element_type=F32)
    snew_ref[...] = gs_ref[...] * s + upd.reshape(n_seq, DK_RET, DV_RET)
    out_ref[...] = _group_norm_gate(o, gain_ref[...], g_ref[...])


def _ret_sample(p, state, gn_gain, batch, t_len):
    decay, cross, kdec, sdec = _ret_tables(t_len)
    rows = RS_SEQS * t_len
    decay_bd = np.stack([np.kron(np.eye(RS_SEQS, dtype=np.float32), decay[h]) for h in range(H_RET)])
    cross_rows = np.tile(cross.T, (1, RS_SEQS))[:, :, None]
    kdec_rows = np.tile(kdec.T, (1, RS_SEQS))[:, :, None]
    qk_blk = lambda col: pl.BlockSpec((rows, DK_RET), lambda i, h: (i, col // DK_RET + h))
    v_blk = lambda col: pl.BlockSpec((rows, DV_RET), lambda i, h: (i, col // DV_RET + h))
    state_blk = pl.BlockSpec((RS_SEQS, None, DK_RET, DV_RET), lambda i, h: (i, h, 0, 0))
    return pl.pallas_call(
        _ret_sample_body,
        grid=(batch // RS_SEQS, H_RET),
        in_specs=[
            qk_blk(COL_Q),
            qk_blk(COL_K),
            v_blk(COL_V),
            v_blk(COL_G),
            state_blk,
            pl.BlockSpec((1, DV_RET), lambda i, h: (0, h)),
            pl.BlockSpec((None, rows, rows), lambda i, h: (h, 0, 0)),
            pl.BlockSpec((None, rows, 1), lambda i, h: (h, 0, 0)),
            pl.BlockSpec((None, rows, 1), lambda i, h: (h, 0, 0)),
            pl.BlockSpec((None, 1, DV_RET), lambda i, h: (h, 0, 0)),
        ],
        out_specs=[pl.BlockSpec((rows, DV_RET), lambda i, h: (i, h)), state_blk],
        out_shape=[
            jax.ShapeDtypeStruct((batch * t_len, W_V), F32),
            jax.ShapeDtypeStruct((batch, H_RET, DK_RET, DV_RET), F32),
        ],
        compiler_params=pltpu.CompilerParams(
            dimension_semantics=("arbitrary", "arbitrary"), vmem_limit_bytes=VMEM_LIMIT
        ),
        name="ret_sample",
    )(p, p, p, p, state, gn_gain, decay_bd, cross_rows, kdec_rows, sdec)


def _sb_weights(qk2, tri, bias2, carry, mask):
    drop, log2_beta = _sb_drop(qk2, bias2, mask)
    return _sb_finish(drop, log2_beta, _sb_after(drop, tri), carry, mask)


def _sb_drop(qk2, bias2, mask):
    z2 = qk2 + bias2
    drop = jnp.maximum(z2, 0.0) + jnp.log2(1.0 + jnp.exp2(-jnp.abs(z2)))
    log2_beta = z2 - drop
    if mask is not None:
        drop = jnp.where(mask, drop, 0.0)
    return drop, log2_beta


def _sb_after(drop, tri):
    hi = drop.astype(BF16)
    lo = (drop - hi.astype(F32)).astype(BF16)
    return jnp.dot(hi, tri, preferred_element_type=F32) + jnp.dot(lo, tri, preferred_element_type=F32)


def _sb_finish(drop, log2_beta, after, carry, mask):
    a = jnp.exp2(log2_beta - after - carry)
    if mask is not None:
        a = jnp.where(mask, a, 0.0)
    return a, carry + after[:, 0:1] + drop[:, 0:1]


def _tri_np(n):
    return np.tril(np.ones((n, n), np.float32), -1)


SBP_T = 256
SBP_HEADS = 8


def _sb_prompt_body(bias_ref, q_ref, k_ref, v_ref, sg_ref, tri_ref, out_ref):
    hg = pl.program_id(1)
    i = pl.program_id(2)
    heads = range(SBP_HEADS)
    bias2 = [bias_ref[hg * SBP_HEADS + g] for g in heads]
    tri = tri_ref[...]
    q = [q_ref[:, g * D_SB : (g + 1) * D_SB] for g in heads]
    r = lax.broadcasted_iota(jnp.int32, (SBP_T, SBP_T), 0)
    c = lax.broadcasted_iota(jnp.int32, (SBP_T, SBP_T), 1)

    def tile(kb, state, mask):
        off = pl.multiple_of(kb * SBP_T, SBP_T)
        qk2 = [lax.dot_general(q[g], k_ref[g, pl.ds(off, SBP_T), :], _NT, preferred_element_type=F32) for g in heads]
        dl = [_sb_drop(qk2[g], bias2[g], mask) for g in heads]
        after = [_sb_after(dl[g][0], tri) for g in heads]
        new_state = []
        for g in heads:
            carry, acc = state[g]
            a, carry = _sb_finish(dl[g][0], dl[g][1], after[g], carry, mask)
            acc = acc + jnp.dot(a.astype(BF16), v_ref[g, pl.ds(off, SBP_T), :], preferred_element_type=F32)
            new_state.append((carry, acc))
        return tuple(new_state)

    state = tuple((jnp.zeros((SBP_T, 1), F32), jnp.zeros((SBP_T, D_SB), F32)) for _ in heads)
    state = tile(i, state, c < r)
    state = lax.fori_loop(0, i, lambda n, st: tile(i - 1 - n, st, None), state)
    acc = jnp.concatenate([st[1] for st in state], axis=1)
    out_ref[...] = (acc * _silu(sg_ref[...].astype(F32))).astype(out_ref.dtype)


def _sb_prompt(p, sk_heads, sv_heads, sb_bias, batch, seq):
    nq = seq // SBP_T
    width = SBP_HEADS * D_SB
    return pl.pallas_call(
        _sb_prompt_body,
        grid=(batch, H_SB // SBP_HEADS, nq),
        in_specs=[
            pl.BlockSpec(memory_space=pltpu.SMEM),
            pl.BlockSpec((SBP_T, width), lambda b, h, i: (b * nq + i, COL_SQ // width + h)),
            pl.BlockSpec((SBP_HEADS, seq, D_SB), lambda b, h, i: (h, b, 0)),
            pl.BlockSpec((SBP_HEADS, seq, D_SB), lambda b, h, i: (h, b, 0)),
            pl.BlockSpec((SBP_T, width), lambda b, h, i: (b * nq + i, COL_SG // width + h)),
            pl.BlockSpec((SBP_T, SBP_T), lambda b, h, i: (0, 0)),
        ],
        out_specs=pl.BlockSpec((SBP_T, width), lambda b, h, i: (b * nq + i, h)),
        out_shape=jax.ShapeDtypeStruct((batch * seq, W_SB), BF16),
        compiler_params=pltpu.CompilerParams(
            dimension_semantics=("arbitrary", "arbitrary", "arbitrary"), vmem_limit_bytes=VMEM_LIMIT
        ),
        name="sb_prompt",
    )(sb_bias * LOG2E, p, sk_heads, sv_heads, p, jnp.asarray(_tri_np(SBP_T), BF16))


SBS_CHUNK_PAGES = 4


SBS_SLOTS = 4


def _sb_sample_body(
    pt_ref, bias_ref, tri_ref, q_ref, kn_ref, vn_ref, sg_ref, kpool_ref, vpool_ref,
    out_ref, kbuf_ref, vbuf_ref, sem_ref, kd0_ref, kd1_ref, vd0_ref, vd1_ref, knd_ref, vnd_ref, kstage_ref, vstage_ref,
    *, chunks_per_seq,
):
    b = pl.program_id(0)
    n_seq = pl.num_programs(0)
    t_len = q_ref.shape[0]
    rows = H_SB * t_len
    half = H_SB // 2
    ppage = kbuf_ref.shape[2] // half
    lookahead = SBS_SLOTS - 1
    kd_refs, vd_refs = (kd0_ref, kd1_ref), (vd0_ref, vd1_ref)

    def chunk_copies(n, slot):
        copies = []
        for pg in range(SBS_CHUNK_PAGES):
            page_id = pt_ref[n * SBS_CHUNK_PAGES + pg]
            copies.append(pltpu.make_async_copy(kpool_ref.at[page_id], kbuf_ref.at[slot, pg], sem_ref.at[0, slot]))
            copies.append(pltpu.make_async_copy(vpool_ref.at[page_id], vbuf_ref.at[slot, pg], sem_ref.at[1, slot]))
        return copies

    def densify(src_ref, dst_ref, row0, n_pseudo):
        for j in range(half):
            dst_ref[row0 : row0 + n_pseudo, j * D_SB : (j + 1) * D_SB] = src_ref[
                pl.ds(j, n_pseudo, stride=half), :
            ].astype(dst_ref.dtype)

    @pl.when(b == 0)
    def _prime():
        for n in range(lookahead):
            for cp in chunk_copies(n, n % SBS_SLOTS):
                cp.start()

    q = q_ref[...]
    zeros = jnp.zeros((t_len, D_SB), F32)
    qbd = jnp.concatenate(
        [
            jnp.concatenate([q[:, h * D_SB : (h + 1) * D_SB] if j == h % half else zeros for j in range(half)], axis=1)
            for h in range(H_SB)
        ],
        axis=0,
    ).astype(BF16)
    bias2 = bias_ref[...]
    tri = tri_ref[...]

    def parity_ok(n_pseudo):
        r = lax.broadcasted_iota(jnp.int32, (rows, n_pseudo), 0)
        p = lax.broadcasted_iota(jnp.int32, (rows, n_pseudo), 1)
        return (p % 2) == (r // (half * t_len)), r, p

    def sweep(k, v, n_pages, carry, acc, mask):
        qk2 = lax.dot_general(qbd, k, _NT, preferred_element_type=F32)
        drop, log2_beta = _sb_drop(qk2, bias2, mask)
        pages = [slice(pg * ppage, (pg + 1) * ppage) for pg in range(n_pages)]
        stacked = jnp.concatenate([drop[:, cols] for cols in pages], axis=0)
        after_stacked = _sb_after(stacked, tri)
        afters = [after_stacked[pg * rows : (pg + 1) * rows] for pg in range(n_pages)]
        shifted = [None] * n_pages
        for pg in reversed(range(n_pages)):
            shifted[pg] = afters[pg] + carry
            carry = carry + afters[pg][:, 0:1] + drop[:, pg * ppage : pg * ppage + 1]
        a = jnp.where(mask, jnp.exp2(log2_beta - jnp.concatenate(shifted, axis=1)), 0.0)
        return carry, acc + jnp.dot(a.astype(BF16), v, preferred_element_type=F32)

    for src_ref, stage_ref, dst_ref in ((kn_ref, kstage_ref, knd_ref), (vn_ref, vstage_ref, vnd_ref)):
        stage_ref[...] = jnp.zeros_like(stage_ref)
        densify(src_ref, stage_ref, 0, 2 * t_len)
        dst_ref[...] = stage_ref[...].astype(BF16)
    ok, r, p = parity_ok(ppage)
    carry, acc = sweep(
        knd_ref[...], vnd_ref[...], 1,
        jnp.zeros((rows, 1), F32), jnp.zeros((rows, half * D_SB), F32), ok & ((p // 2) < (r % t_len)),
    )

    chunk_ok, _, _ = parity_ok(SBS_CHUNK_PAGES * ppage)
    for c in range(chunks_per_seq):
        n = b * chunks_per_seq + c
        slot = c % SBS_SLOTS
        for cp in chunk_copies(n, slot):
            cp.wait()
        for cp in chunk_copies(n + lookahead, (c + lookahead) % SBS_SLOTS):
            cp.start()
        kd_ref, vd_ref = kd_refs[c % 2], vd_refs[c % 2]
        for pg in range(SBS_CHUNK_PAGES):
            densify(kbuf_ref.at[slot, pg], kd_ref, pg * ppage, ppage)
            densify(vbuf_ref.at[slot, pg], vd_ref, pg * ppage, ppage)
        carry, acc = sweep(kd_ref[...], vd_ref[...], SBS_CHUNK_PAGES, carry, acc, chunk_ok)

    o = jnp.concatenate(
        [acc[h * t_len : (h + 1) * t_len, (h % half) * D_SB : (h % half + 1) * D_SB] for h in range(H_SB)], axis=1
    )
    out_ref[...] = o * _silu(sg_ref[...])

    @pl.when(b == n_seq - 1)
    def _drain():
        for i in range(lookahead):
            for cp in chunk_copies(n_seq * chunks_per_seq + i, i % SBS_SLOTS):
                cp.wait()


def _sb_sample(page_table, p, sk, sv, pool_k, pool_v, sb_bias, batch, t_len):
    n_pages = page_table.shape[1]
    chunks_per_seq = n_pages // SBS_CHUNK_PAGES
    assert n_pages % SBS_CHUNK_PAGES == 0 and chunks_per_seq % SBS_SLOTS == 0
    page_rows = pool_k.shape[1]
    rows = H_SB * t_len
    half = H_SB // 2
    ppage = page_rows // half
    width = half * D_SB
    bias_rows = jnp.repeat(sb_bias * LOG2E, t_len)[:, None]
    tri = jnp.asarray(_tri_np(ppage), BF16)
    fetch = page_table.reshape(batch, chunks_per_seq, SBS_CHUNK_PAGES)[:, ::-1, :].reshape(-1)
    fetch = jnp.concatenate([fetch, fetch[: (SBS_SLOTS - 1) * SBS_CHUNK_PAGES]])

    const2 = lambda b, pt: (0, 0)
    grid_spec = pltpu.PrefetchScalarGridSpec(
        num_scalar_prefetch=1,
        grid=(batch,),
        in_specs=[
            pl.BlockSpec((rows, 1), const2),
            pl.BlockSpec((ppage, ppage), const2),
            pl.BlockSpec((t_len, W_SB), lambda b, pt: (b, COL_SQ // W_SB)),
            pl.BlockSpec((rows, D_SB), lambda b, pt: (b, 0)),
            pl.BlockSpec((rows, D_SB), lambda b, pt: (b, 0)),
            pl.BlockSpec((t_len, W_SB), lambda b, pt: (b, COL_SG // W_SB)),
            pl.BlockSpec(memory_space=pl.ANY),
            pl.BlockSpec(memory_space=pl.ANY),
        ],
        out_specs=pl.BlockSpec((t_len, W_SB), lambda b, pt: (b, 0)),
        scratch_shapes=[
            pltpu.VMEM((SBS_SLOTS, SBS_CHUNK_PAGES, page_rows, D_SB), F32),
            pltpu.VMEM((SBS_SLOTS, SBS_CHUNK_PAGES, page_rows, D_SB), F32),
            pltpu.SemaphoreType.DMA((2, SBS_SLOTS)),
            pltpu.VMEM((SBS_CHUNK_PAGES * ppage, width), BF16),
            pltpu.VMEM((SBS_CHUNK_PAGES * ppage, width), BF16),
            pltpu.VMEM((SBS_CHUNK_PAGES * ppage, width), BF16),
            pltpu.VMEM((SBS_CHUNK_PAGES * ppage, width), BF16),
            pltpu.VMEM((ppage, width), BF16),
            pltpu.VMEM((ppage, width), BF16),
            pltpu.VMEM((ppage, width), F32),
            pltpu.VMEM((ppage, width), F32),
        ],
    )
    return pl.pallas_call(
        functools.partial(_sb_sample_body, chunks_per_seq=chunks_per_seq),
        grid_spec=grid_spec,
        out_shape=jax.ShapeDtypeStruct((batch * t_len, W_SB), F32),
        compiler_params=pltpu.CompilerParams(dimension_semantics=("arbitrary",), vmem_limit_bytes=VMEM_LIMIT),
        name="sb_sample",
    )(fetch, bias_rows, tri, p, sk, sv, p, pool_k, pool_v)


def _merge_body(x_ref, rb_ref, sb_ref, gr_ref, gs_ref, wr_ref, ws_ref, wo_ref, fg_ref, y_ref):
    m = gr_ref[...].astype(F32) * jnp.dot(rb_ref[...].astype(BF16), wr_ref[...], preferred_element_type=F32)
    m = m + gs_ref[...].astype(F32) * jnp.dot(sb_ref[...].astype(BF16), ws_ref[...], preferred_element_type=F32)
    hid = x_ref[...] + jnp.dot(m.astype(BF16), wo_ref[...], preferred_element_type=F32)
    r = lax.rsqrt(jnp.mean(hid * hid, axis=-1, keepdims=True) + EPS_RMS)
    y_ref[...] = hid * r * fg_ref[...]


def _merge(x, ret_branch, sb_branch, p, w_ret, w_sb, w_out, final_gain, *, tm):
    m = x.shape[0]
    resident = functools.partial(pl.BlockSpec, index_map=lambda i: (0, 0), pipeline_mode=pl.Buffered(1))
    return pl.pallas_call(
        _merge_body,
        grid=(m // tm,),
        in_specs=[
            pl.BlockSpec((tm, D_MODEL), lambda i: (i, 0)),
            pl.BlockSpec((tm, W_V), lambda i: (i, 0)),
            pl.BlockSpec((tm, W_SB), lambda i: (i, 0)),
            pl.BlockSpec((tm, D_MODEL), lambda i: (i, COL_GR // D_MODEL)),
            pl.BlockSpec((tm, D_MODEL), lambda i: (i, COL_GS // D_MODEL)),
            resident((W_V, D_MODEL)),
            resident((W_SB, D_MODEL)),
            resident((D_MODEL, D_MODEL)),
            pl.BlockSpec((1, D_MODEL), lambda i: (0, 0)),
        ],
        out_specs=pl.BlockSpec((tm, D_MODEL), lambda i: (i, 0)),
        out_shape=jax.ShapeDtypeStruct((m, D_MODEL), F32),
        compiler_params=pltpu.CompilerParams(dimension_semantics=("arbitrary",), vmem_limit_bytes=VMEM_LIMIT),
        name="merge",
    )(x, ret_branch, sb_branch, p, p, w_ret, w_sb, w_out, final_gain)


def _rope_tables(pos):
    half = DK_RET // 2
    inv = ROPE_BASE ** (-np.arange(half, dtype=np.float64) / half)
    ang = np.asarray(pos, np.float64)[:, None] * inv[None, :]
    cos, sin = np.cos(ang), np.sin(ang)
    cos_full = np.concatenate([cos, cos], axis=-1)
    sin_signed = np.concatenate([-sin, sin], axis=-1)
    return cos_full.astype(np.float32), sin_signed.astype(np.float32)


def _layer(xp, xs, state, pool_k, pool_v, page_table, norm_gain, w_in, sb_bias, gn_gain, w_ret, w_sb, w_out, final_gain):
    batch, seq, _ = xp.shape
    dec_batch, t_len, _ = xs.shape
    n_pool, page = pool_k.shape[:2]
    past_len = page_table.shape[1] * page

    w_in_b = w_in.astype(BF16)
    w_ret_b, w_sb_b, w_out_b = w_ret.astype(BF16), w_sb.astype(BF16), w_out.astype(BF16)
    gain2 = norm_gain[None, :]
    gn_gain2 = gn_gain[None, :]
    final_gain2 = final_gain[None, :]

    x2p = xp.reshape(batch * seq, D_MODEL)
    x2s = xs.reshape(dec_batch * t_len, D_MODEL)
    tm_p, tm_s = 512, 512
    cos_p, sin_p = _rope_tables(np.arange(seq))
    cos_s, sin_s = _rope_tables(past_len + np.arange(t_len))
    cos_s, sin_s = np.tile(cos_s, (tm_s // t_len, 1)), np.tile(sin_s, (tm_s // t_len, 1))

    pp, skp, svp, skp_heads, svp_heads = _project(
        x2p, gain2, cos_p, sin_p, w_in_b, tm=tm_p, rope_period_tiles=seq // tm_p, p_dtype=BF16
    )
    ps, sks, svs, _, _ = _project(x2s, gain2, cos_s, sin_s, w_in_b, tm=tm_s, rope_period_tiles=1, p_dtype=F32)

    rbp, state_p = _ret_prompt(pp, gn_gain2, batch, seq)
    rbs, state_s = _ret_sample(ps, state, gn_gain2, dec_batch, t_len)

    sbp = _sb_prompt(pp, skp_heads, svp_heads, sb_bias, batch, seq)
    sbs = _sb_sample(
        page_table, ps, sks, svs, pool_k.reshape(n_pool, page * H_SB, D_SB), pool_v.reshape(n_pool, page * H_SB, D_SB),
        sb_bias, dec_batch, t_len,
    )

    yp = _merge(x2p, rbp, sbp, pp, w_ret_b, w_sb_b, w_out_b, final_gain2, tm=256)
    ys = _merge(x2s, rbs, sbs, ps, w_ret_b, w_sb_b, w_out_b, final_gain2, tm=256)

    return (
        yp.reshape(batch, seq, D_MODEL),
        ys.reshape(dec_batch, t_len, D_MODEL),
        state_p,
        skp.reshape(batch, seq, H_SB, D_SB),
        svp.reshape(batch, seq, H_SB, D_SB),
        state_s,
        sks.reshape(dec_batch, t_len, H_SB, D_SB),
        svs.reshape(dec_batch, t_len, H_SB, D_SB),
    )


def kernel(x_prompt, x_sample, state_ret, cache_sb_k, cache_sb_v, page_table, norm_gain, w_in, sb_bias, ret_gn_gain, w_ret_proj, w_sb_proj, w_out, final_norm_gain):
    depth = w_in.shape[0]
    assert depth == 1, "single-layer trunk"
    outs = _layer(
        x_prompt, x_sample, state_ret[0], cache_sb_k[0], cache_sb_v[0], page_table, norm_gain[0], w_in[0],
        sb_bias[0], ret_gn_gain[0], w_ret_proj[0], w_sb_proj[0], w_out[0], final_norm_gain,
    )
    yp, ys, state_p, skp, svp, state_s, sks, svs = outs
    return (yp, ys, state_p[None], skp[None], svp[None], state_s[None], sks[None], svs[None])
```

```python
import functools

import jax
import jax.numpy as jnp
import numpy as np
from jax import lax
from jax.experimental import pallas as pl
from jax.experimental.pallas import tpu as pltpu

F32 = jnp.float32
BF16 = jnp.bfloat16

D_MODEL = 2048
H_RET = 8
DK_RET = 128
DV_RET = 256
RET_CHUNK = 128
H_SB = 8
D_SB = 128
ROPE_BASE = 10000.0
EPS_RMS = 1e-6
EPS_GN = 1e-5
W_QK = H_RET * DK_RET
W_V = H_RET * DV_RET
W_SB = H_SB * D_SB

PROJ_TN = 1024
PROJ_GROUP = 512
PROJ_GROUP_ROWS = 256
W_IN_COL_SK = 2 * W_QK + 2 * W_V + W_SB
W_IN_COL_SV = W_IN_COL_SK + W_SB
P_COLS = 2 * W_QK + 2 * W_V + 2 * W_SB + 2 * D_MODEL
N_P_TILES = P_COLS // PROJ_TN
COL_Q, COL_K, COL_V, COL_G, COL_SQ, COL_SG, COL_GR, COL_GS = 0, 1024, 2048, 4096, 6144, 7168, 8192, 10240

LOG2E = float(np.log2(np.e))
SB_Q_SCALE = D_SB**-0.5 * LOG2E

VMEM_LIMIT = 56 * 1024 * 1024

_NT = (((1,), (1,)), ((), ()))


def _silu(g):
    return g * jax.nn.sigmoid(g)


def _rmsnorm_body(x_ref, gain_ref, xn_ref):
    x = x_ref[...]
    r = lax.rsqrt(jnp.mean(x * x, axis=-1, keepdims=True) + EPS_RMS)
    xn_ref[...] = (x * r * gain_ref[...]).astype(xn_ref.dtype)


def _rmsnorm_bf16(x, gain, *, tm):
    m = x.shape[0]
    return pl.pallas_call(
        _rmsnorm_body,
        grid=(m // tm,),
        in_specs=[pl.BlockSpec((tm, D_MODEL), lambda i: (i, 0)), pl.BlockSpec((1, D_MODEL), lambda i: (0, 0))],
        out_specs=pl.BlockSpec((tm, D_MODEL), lambda i: (i, 0)),
        out_shape=jax.ShapeDtypeStruct((m, D_MODEL), BF16),
        compiler_params=pltpu.CompilerParams(dimension_semantics=("arbitrary",), vmem_limit_bytes=VMEM_LIMIT),
        name="rmsnorm",
    )(x, gain)


def _proj_groups(xn_ref, wb_ref, row_groups):
    rows = xn_ref.shape[0] // row_groups
    for c in range(PROJ_TN // PROJ_GROUP):
        for r in range(row_groups):
            acc = jnp.dot(
                xn_ref[r * rows : (r + 1) * rows, :],
                wb_ref[:, c * PROJ_GROUP : (c + 1) * PROJ_GROUP],
                preferred_element_type=F32,
            )
            yield r * rows, c * PROJ_GROUP, rows, acc


def _lane_blocks(acc):
    for h in range(PROJ_GROUP // DK_RET):
        yield h * DK_RET, acc[:, h * DK_RET : (h + 1) * DK_RET]


def _proj_slab_body(xn_ref, cos_ref, sin_ref, w_ref, p_ref, wb_ref, *, row_groups):
    j = pl.program_id(0)

    @pl.when(pl.program_id(1) == 0)
    def _cast():
        wb_ref[...] = w_ref[...].astype(BF16)

    groups = functools.partial(_proj_groups, xn_ref, wb_ref, row_groups)

    @pl.when(j < COL_V // PROJ_TN)
    def _rope():
        scale = jnp.where(j == COL_K // PROJ_TN, F32(DK_RET**-0.5), F32(1.0))
        for row0, col0, rows, acc in groups():
            cos = cos_ref[row0 : row0 + rows, :] * scale
            sin = sin_ref[row0 : row0 + rows, :] * scale
            for off, xh in _lane_blocks(acc):
                y = xh * cos + pltpu.roll(xh, DK_RET // 2, axis=1) * sin
                p_ref[row0 : row0 + rows, col0 + off : col0 + off + DK_RET] = y.astype(p_ref.dtype)

    @pl.when((j >= COL_V // PROJ_TN) & (j < COL_GR // PROJ_TN))
    def _plain():
        scale = jnp.where(j == COL_SQ // PROJ_TN, F32(SB_Q_SCALE), F32(1.0))
        for row0, col0, rows, acc in groups():
            p_ref[row0 : row0 + rows, col0 : col0 + PROJ_GROUP] = (acc * scale).astype(p_ref.dtype)

    @pl.when(j >= COL_GR // PROJ_TN)
    def _gate():
        for row0, col0, rows, acc in groups():
            y = 0.5 * jnp.tanh(0.5 * acc) + 0.5
            p_ref[row0 : row0 + rows, col0 : col0 + PROJ_GROUP] = y.astype(p_ref.dtype)


def _w_tile(j):
    first_kv = COL_SQ // PROJ_TN + 1
    return jnp.where(j < first_kv, j, j + 2 * W_SB // PROJ_TN)


def _project_slab(xn, cos, sin, w_in, *, tm, rope_period_tiles, p_dtype):
    m = xn.shape[0]
    return pl.pallas_call(
        functools.partial(_proj_slab_body, row_groups=tm // PROJ_GROUP_ROWS),
        grid=(N_P_TILES, m // tm),
        in_specs=[
            pl.BlockSpec((tm, D_MODEL), lambda j, i: (i, 0)),
            pl.BlockSpec((tm, DK_RET), lambda j, i: (i % rope_period_tiles, 0)),
            pl.BlockSpec((tm, DK_RET), lambda j, i: (i % rope_period_tiles, 0)),
            pl.BlockSpec((D_MODEL, PROJ_TN), lambda j, i: (0, _w_tile(j))),
        ],
        out_specs=pl.BlockSpec((tm, PROJ_TN), lambda j, i: (i, j)),
        out_shape=jax.ShapeDtypeStruct((m, P_COLS), p_dtype),
        scratch_shapes=[pltpu.VMEM((D_MODEL, PROJ_TN), BF16)],
        compiler_params=pltpu.CompilerParams(
            dimension_semantics=("arbitrary", "arbitrary"), vmem_limit_bytes=VMEM_LIMIT
        ),
        name="proj",
    )(xn, cos, sin, w_in)


def _proj_heads_body(xn_ref, w_ref, tok_ref, head_ref, wb_ref, *, row_groups):
    @pl.when(pl.program_id(0) == 0)
    def _cast():
        wb_ref[...] = w_ref[...].astype(BF16)

    for row0, col0, rows, acc in _proj_groups(xn_ref, wb_ref, row_groups):
        for off, ah in _lane_blocks(acc):
            head = (col0 + off) // D_SB
            tok_ref[pl.ds(row0 * H_SB + head, rows, stride=H_SB), :] = ah
            head_ref[head, row0 : row0 + rows, :] = ah.astype(BF16)


def _project_heads(xn, w_in, col, *, tm):
    m = xn.shape[0]
    return pl.pallas_call(
        functools.partial(_proj_heads_body, row_groups=tm // PROJ_GROUP_ROWS),
        grid=(m // tm,),
        in_specs=[
            pl.BlockSpec((tm, D_MODEL), lambda i: (i, 0)),
            pl.BlockSpec((D_MODEL, W_SB), lambda i: (0, col // W_SB)),
        ],
        out_specs=[
            pl.BlockSpec((tm * H_SB, D_SB), lambda i: (i, 0)),
            pl.BlockSpec((H_SB, tm, D_SB), lambda i: (0, i, 0)),
        ],
        out_shape=[
            jax.ShapeDtypeStruct((m * H_SB, D_SB), F32),
            jax.ShapeDtypeStruct((H_SB, m, D_SB), BF16),
        ],
        scratch_shapes=[pltpu.VMEM((D_MODEL, W_SB), BF16)],
        compiler_params=pltpu.CompilerParams(dimension_semantics=("arbitrary",), vmem_limit_bytes=VMEM_LIMIT),
        name="proj_heads",
    )(xn, w_in)


def _group_norm_gate(o, gain, g):
    mu = jnp.mean(o, axis=-1, keepdims=True)
    d = o - mu
    var = jnp.mean(d * d, axis=-1, keepdims=True)
    return d * lax.rsqrt(var + EPS_GN) * gain * _silu(g)


def _ret_prompt_body(q_ref, k_ref, v_ref, g_ref, gain_ref, decay_ref, cd_ref, kd_ref, gs_ref, out_ref, s_ref):
    c = pl.program_id(1)

    @pl.when(c == 0)
    def _init():
        s_ref[...] = jnp.zeros_like(s_ref)

    heads = range(H_RET)
    q = [q_ref[:, h * DK_RET : (h + 1) * DK_RET] for h in heads]
    k = [k_ref[:, h * DK_RET : (h + 1) * DK_RET] for h in heads]
    v = [v_ref[:, h * DV_RET : (h + 1) * DV_RET] for h in heads]
    s = [s_ref[0, h] for h in heads]
    inner = [lax.dot_general(q[h], k[h], _NT, preferred_element_type=F32) for h in heads]
    cross = [jnp.dot(q[h], s[h].astype(BF16), preferred_element_type=F32) for h in heads]
    for h in heads:
        kdec_t = (k[h].astype(F32) * kd_ref[:, h : h + 1]).T.astype(BF16)
        s_ref[0, h] = gs_ref[h] * s[h] + jnp.dot(kdec_t, v[h], preferred_element_type=F32)
    for h in heads:
        o = jnp.dot((inner[h] * decay_ref[h]).astype(BF16), v[h], preferred_element_type=F32)
        o = o + cross[h] * cd_ref[:, h : h + 1]
        g = g_ref[:, h * DV_RET : (h + 1) * DV_RET].astype(F32)
        gain = gain_ref[:, h * DV_RET : (h + 1) * DV_RET]
        out_ref[:, h * DV_RET : (h + 1) * DV_RET] = _group_norm_gate(o, gain, g).astype(out_ref.dtype)


def _ret_tables(chunk):
    lg = np.log1p(-np.exp2(-5.0 - np.arange(H_RET, dtype=np.float64)))
    idx = np.arange(chunk, dtype=np.float64)
    diff = idx[:, None] - idx[None, :]
    decay = np.where(diff >= 0, np.exp(lg[:, None, None] * np.maximum(diff, 0.0)[None]), 0.0)
    cross = np.exp(lg[:, None] * (idx + 1.0)[None, :]).T
    kdec = np.exp(lg[:, None] * (chunk - 1.0 - idx)[None, :]).T
    sdec = np.broadcast_to(np.exp(lg * chunk)[:, None, None], (H_RET, 1, DV_RET))
    return tuple(np.asarray(a, np.float32) for a in (decay, cross, kdec, sdec))


def _ret_prompt(p, gn_gain, batch, seq):
    nc = seq // RET_CHUNK
    decay, cross, kdec, sdec = _ret_tables(RET_CHUNK)
    row = lambda b, c: b * nc + c
    const2 = lambda b, c: (0, 0)
    const3 = lambda b, c: (0, 0, 0)
    return pl.pallas_call(
        _ret_prompt_body,
        grid=(batch, nc),
        in_specs=[
            pl.BlockSpec((RET_CHUNK, W_QK), lambda b, c: (row(b, c), COL_Q // W_QK)),
            pl.BlockSpec((RET_CHUNK, W_QK), lambda b, c: (row(b, c), COL_K // W_QK)),
            pl.BlockSpec((RET_CHUNK, W_V), lambda b, c: (row(b, c), COL_V // W_V)),
            pl.BlockSpec((RET_CHUNK, W_V), lambda b, c: (row(b, c), COL_G // W_V)),
            pl.BlockSpec((1, W_V), const2),
            pl.BlockSpec((H_RET, RET_CHUNK, RET_CHUNK), const3),
            pl.BlockSpec((RET_CHUNK, H_RET), const2),
            pl.BlockSpec((RET_CHUNK, H_RET), const2),
            pl.BlockSpec((H_RET, 1, DV_RET), const3),
        ],
        out_specs=[
            pl.BlockSpec((RET_CHUNK, W_V), lambda b, c: (row(b, c), 0)),
            pl.BlockSpec((1, H_RET, DK_RET, DV_RET), lambda b, c: (b, 0, 0, 0)),
        ],
        out_shape=[
            jax.ShapeDtypeStruct((batch * seq, W_V), BF16),
            jax.ShapeDtypeStruct((batch, H_RET, DK_RET, DV_RET), F32),
        ],
        compiler_params=pltpu.CompilerParams(
            dimension_semantics=("arbitrary", "arbitrary"), vmem_limit_bytes=VMEM_LIMIT
        ),
        name="ret_prompt",
    )(p, p, p, p, gn_gain, decay, cross, kdec, sdec)


RS_SEQS = 16


def _ret_sample_body(q_ref, k_ref, v_ref, g_ref, s_ref, gain_ref, decay_ref, cd_ref, kd_ref, gs_ref, out_ref, snew_ref):
    n_seq = s_ref.shape[0]
    rows = q_ref.shape[0]
    t_len = rows // n_seq
    q = q_ref[...]
    k = k_ref[...]
    v = v_ref[...].astype(BF16)
    s = s_ref[...]
    inner = lax.dot_general(q.astype(BF16), k.astype(BF16), _NT, preferred_element_type=F32) * decay_ref[...]
    o = jnp.dot(inner.astype(BF16), v, preferred_element_type=F32)
    row_seq = lax.broadcasted_iota(jnp.int32, (rows, DK_RET), 0) // t_len
    q_exp = jnp.concatenate([jnp.where(row_seq == b, q, 0.0) for b in range(n_seq)], axis=1).astype(BF16)
    s_rows = s.reshape(n_seq * DK_RET, DV_RET).astype(BF16)
    o = o + jnp.dot(q_exp, s_rows, preferred_element_type=F32) * cd_ref[...]
    kdec_t = (k * kd_ref[...]).T
    col_seq = lax.broadcasted_iota(jnp.int32, (DK_RET, rows), 1) // t_len
    k_exp_t = jnp.concatenate([jnp.where(col_seq == b, kdec_t, 0.0) for b in range(n_seq)], axis=0).astype(BF16)
    upd = jnp.dot(k_exp_t, v, preferred_element_type=F32)
    snew_ref[...] = gs_ref[...] * s + upd.reshape(n_seq, DK_RET, DV_RET)
    out_ref[...] = _group_norm_gate(o, gain_ref[...], g_ref[...])


def _ret_sample(p, state, gn_gain, batch, t_len):
    decay, cross, kdec, sdec = _ret_tables(t_len)
    rows = RS_SEQS * t_len
    decay_bd = np.stack([np.kron(np.eye(RS_SEQS, dtype=np.float32), decay[h]) for h in range(H_RET)])
    cross_rows = np.tile(cross.T, (1, RS_SEQS))[:, :, None]
    kdec_rows = np.tile(kdec.T, (1, RS_SEQS))[:, :, None]
    qk_blk = lambda col: pl.BlockSpec((rows, DK_RET), lambda i, h: (i, col // DK_RET + h))
    v_blk = lambda col: pl.BlockSpec((rows, DV_RET), lambda i, h: (i, col // DV_RET + h))
    state_blk = pl.BlockSpec((RS_SEQS, None, DK_RET, DV_RET), lambda i, h: (i, h, 0, 0))
    return pl.pallas_call(
        _ret_sample_body,
        grid=(batch // RS_SEQS, H_RET),
        in_specs=[
            qk_blk(COL_Q),
            qk_blk(COL_K),
            v_blk(COL_V),
            v_blk(COL_G),
            state_blk,
            pl.BlockSpec((1, DV_RET), lambda i, h: (0, h)),
            pl.BlockSpec((None, rows, rows), lambda i, h: (h, 0, 0)),
            pl.BlockSpec((None, rows, 1), lambda i, h: (h, 0, 0)),
            pl.BlockSpec((None, rows, 1), lambda i, h: (h, 0, 0)),
            pl.BlockSpec((None, 1, DV_RET), lambda i, h: (h, 0, 0)),
        ],
        out_specs=[pl.BlockSpec((rows, DV_RET), lambda i, h: (i, h)), state_blk],
        out_shape=[
            jax.ShapeDtypeStruct((batch * t_len, W_V), F32),
            jax.ShapeDtypeStruct((batch, H_RET, DK_RET, DV_RET), F32),
        ],
        compiler_params=pltpu.CompilerParams(
            dimension_semantics=("arbitrary", "arbitrary"), vmem_limit_bytes=VMEM_LIMIT
        ),
        name="ret_sample",
    )(p, p, p, p, state, gn_gain, decay_bd, cross_rows, kdec_rows, sdec)


def _sb_weights(qk2, tri, bias2, carry, mask):
    drop, log2_beta = _sb_drop(qk2, bias2, mask)
    return _sb_finish(drop, log2_beta, _sb_after(drop, tri), carry, mask)


def _sb_drop(qk2, bias2, mask):
    z2 = qk2 + bias2
    drop = jnp.maximum(z2, 0.0) + jnp.log2(1.0 + jnp.exp2(-jnp.abs(z2)))
    log2_beta = z2 - drop
    if mask is not None:
        drop = jnp.where(mask, drop, 0.0)
    return drop, log2_beta


def _sb_after(drop, tri):
    hi = drop.astype(BF16)
    lo = (drop - hi.astype(F32)).astype(BF16)
    return jnp.dot(hi, tri, preferred_element_type=F32) + jnp.dot(lo, tri, preferred_element_type=F32)


def _sb_finish(drop, log2_beta, after, carry, mask):
    a = jnp.exp2(log2_beta - after - carry)
    if mask is not None:
        a = jnp.where(mask, a, 0.0)
    return a, carry + after[:, 0:1] + drop[:, 0:1]


def _tri_np(n):
    return np.tril(np.ones((n, n), np.float32), -1)


SBP_T = 256
SBP_HEADS = 8


def _sb_prompt_body(bias_ref, q_ref, k_ref, v_ref, sg_ref, tri_ref, out_ref):
    hg = pl.program_id(1)
    i = pl.program_id(2)
    heads = range(SBP_HEADS)
    bias2 = [bias_ref[hg * SBP_HEADS + g] for g in heads]
    tri = tri_ref[...]
    q = [q_ref[:, g * D_SB : (g + 1) * D_SB] for g in heads]
    r = lax.broadcasted_iota(jnp.int32, (SBP_T, SBP_T), 0)
    c = lax.broadcasted_iota(jnp.int32, (SBP_T, SBP_T), 1)

    def tile(kb, state, mask):
        off = pl.multiple_of(kb * SBP_T, SBP_T)
        qk2 = [lax.dot_general(q[g], k_ref[g, pl.ds(off, SBP_T), :], _NT, preferred_element_type=F32) for g in heads]
        dl = [_sb_drop(qk2[g], bias2[g], mask) for g in heads]
        after = [_sb_after(dl[g][0], tri) for g in heads]
        new_state = []
        for g in heads:
            carry, acc = state[g]
            a, carry = _sb_finish(dl[g][0], dl[g][1], after[g], carry, mask)
            acc = acc + jnp.dot(a.astype(BF16), v_ref[g, pl.ds(off, SBP_T), :], preferred_element_type=F32)
            new_state.append((carry, acc))
        return tuple(new_state)

    state = tuple((jnp.zeros((SBP_T, 1), F32), jnp.zeros((SBP_T, D_SB), F32)) for _ in heads)
    state = tile(i, state, c < r)
    state = lax.fori_loop(0, i, lambda n, st: tile(i - 1 - n, st, None), state)
    acc = jnp.concatenate([st[1] for st in state], axis=1)
    out_ref[...] = (acc * _silu(sg_ref[...].astype(F32))).astype(out_ref.dtype)


def _sb_prompt(p, sk_heads, sv_heads, sb_bias, batch, seq):
    nq = seq // SBP_T
    width = SBP_HEADS * D_SB
    return pl.pallas_call(
        _sb_prompt_body,
        grid=(batch, H_SB // SBP_HEADS, nq),
        in_specs=[
            pl.BlockSpec(memory_space=pltpu.SMEM),
            pl.BlockSpec((SBP_T, width), lambda b, h, i: (b * nq + i, COL_SQ // width + h)),
            pl.BlockSpec((SBP_HEADS, seq, D_SB), lambda b, h, i: (h, b, 0)),
            pl.BlockSpec((SBP_HEADS, seq, D_SB), lambda b, h, i: (h, b, 0)),
            pl.BlockSpec((SBP_T, width), lambda b, h, i: (b * nq + i, COL_SG // width + h)),
            pl.BlockSpec((SBP_T, SBP_T), lambda b, h, i: (0, 0)),
        ],
        out_specs=pl.BlockSpec((SBP_T, width), lambda b, h, i: (b * nq + i, h)),
        out_shape=jax.ShapeDtypeStruct((batch * seq, W_SB), BF16),
        compiler_params=pltpu.CompilerParams(
            dimension_semantics=("arbitrary", "arbitrary", "arbitrary"), vmem_limit_bytes=VMEM_LIMIT
        ),
        name="sb_prompt",
    )(sb_bias * LOG2E, p, sk_heads, sv_heads, p, jnp.asarray(_tri_np(SBP_T), BF16))


SBS_CHUNK_PAGES = 4


SBS_SLOTS = 4


def _sb_sample_body(
    pt_ref, bias_ref, tri_ref, q_ref, kn_ref, vn_ref, sg_ref, kpool_ref, vpool_ref,
    out_ref, kbuf_ref, vbuf_ref, sem_ref, kd0_ref, kd1_ref, vd0_ref, vd1_ref, knd_ref, vnd_ref, kstage_ref, vstage_ref,
    *, chunks_per_seq,
):
    b = pl.program_id(0)
    n_seq = pl.num_programs(0)
    t_len = q_ref.shape[0]
    rows = H_SB * t_len
    half = H_SB // 2
    ppage = kbuf_ref.shape[2] // half
    lookahead = SBS_SLOTS - 1
    kd_refs, vd_refs = (kd0_ref, kd1_ref), (vd0_ref, vd1_ref)

    def chunk_copies(n, slot):
        copies = []
        for pg in range(SBS_CHUNK_PAGES):
            page_id = pt_ref[n * SBS_CHUNK_PAGES + pg]
            copies.append(pltpu.make_async_copy(kpool_ref.at[page_id], kbuf_ref.at[slot, pg], sem_ref.at[0, slot]))
            copies.append(pltpu.make_async_copy(vpool_ref.at[page_id], vbuf_ref.at[slot, pg], sem_ref.at[1, slot]))
        return copies

    def densify(src_ref, dst_ref, row0, n_pseudo):
        for j in range(half):
            dst_ref[row0 : row0 + n_pseudo, j * D_SB : (j + 1) * D_SB] = src_ref[
                pl.ds(j, n_pseudo, stride=half), :
            ].astype(dst_ref.dtype)

    @pl.when(b == 0)
    def _prime():
        for n in range(lookahead):
            for cp in chunk_copies(n, n % SBS_SLOTS):
                cp.start()

    q = q_ref[...]
    zeros = jnp.zeros((t_len, D_SB), F32)
    qbd = jnp.concatenate(
        [
            jnp.concatenate([q[:, h * D_SB : (h + 1) * D_SB] if j == h % half else zeros for j in range(half)], axis=1)
            for h in range(H_SB)
        ],
        axis=0,
    ).astype(BF16)
    bias2 = bias_ref[...]
    tri = tri_ref[...]

    def parity_ok(n_pseudo):
        r = lax.broadcasted_iota(jnp.int32, (rows, n_pseudo), 0)
        p = lax.broadcasted_iota(jnp.int32, (rows, n_pseudo), 1)
        return (p % 2) == (r // (half * t_len)), r, p

    def sweep(k, v, n_pages, carry, acc, mask):
        qk2 = lax.dot_general(qbd, k, _NT, preferred_element_type=F32)
        drop, log2_beta = _sb_drop(qk2, bias2, mask)
        pages = [slice(pg * ppage, (pg + 1) * ppage) for pg in range(n_pages)]
        stacked = jnp.concatenate([drop[:, cols] for cols in pages], axis=0)
        after_stacked = _sb_after(stacked, tri)
        afters = [after_stacked[pg * rows : (pg + 1) * rows] for pg in range(n_pages)]
        shifted = [None] * n_pages
        for pg in reversed(range(n_pages)):
            shifted[pg] = afters[pg] + carry
            carry = carry + afters[pg][:, 0:1] + drop[:, pg * ppage : pg * ppage + 1]
        a = jnp.where(mask, jnp.exp2(log2_beta - jnp.concatenate(shifted, axis=1)), 0.0)
        return carry, acc + jnp.dot(a.astype(BF16), v, preferred_element_type=F32)

    for src_ref, stage_ref, dst_ref in ((kn_ref, kstage_ref, knd_ref), (vn_ref, vstage_ref, vnd_ref)):
        stage_ref[...] = jnp.zeros_like(stage_ref)
        densify(src_ref, stage_ref, 0, 2 * t_len)
        dst_ref[...] = stage_ref[...].astype(BF16)
    ok, r, p = parity_ok(ppage)
    carry, acc = sweep(
        knd_ref[...], vnd_ref[...], 1,
        jnp.zeros((rows, 1), F32), jnp.zeros((rows, half * D_SB), F32), ok & ((p // 2) < (r % t_len)),
    )

    chunk_ok, _, _ = parity_ok(SBS_CHUNK_PAGES * ppage)
    for c in range(chunks_per_seq):
        n = b * chunks_per_seq + c
        slot = c % SBS_SLOTS
        for cp in chunk_copies(n, slot):
            cp.wait()
        for cp in chunk_copies(n + lookahead, (c + lookahead) % SBS_SLOTS):
            cp.start()
        kd_ref, vd_ref = kd_refs[c % 2], vd_refs[c % 2]
        for pg in range(SBS_CHUNK_PAGES):
            densify(kbuf_ref.at[slot, pg], kd_ref, pg * ppage, ppage)
            densify(vbuf_ref.at[slot, pg], vd_ref, pg * ppage, ppage)
        carry, acc = sweep(kd_ref[...], vd_ref[...], SBS_CHUNK_PAGES, carry, acc, chunk_ok)

    o = jnp.concatenate(
        [acc[h * t_len : (h + 1) * t_len, (h % half) * D_SB : (h % half + 1) * D_SB] for h in range(H_SB)], axis=1
    )
    out_ref[...] = o * _silu(sg_ref[...])

    @pl.when(b == n_seq - 1)
    def _drain():
        for i in range(lookahead):
            for cp in chunk_copies(n_seq * chunks_per_seq + i, i % SBS_SLOTS):
                cp.wait()


def _sb_sample(page_table, p, sk, sv, pool_k, pool_v, sb_bias, batch, t_len):
    n_pages = page_table.shape[1]
    chunks_per_seq = n_pages // SBS_CHUNK_PAGES
    assert n_pages % SBS_CHUNK_PAGES == 0 and chunks_per_seq % SBS_SLOTS == 0
    page_rows = pool_k.shape[1]
    rows = H_SB * t_len
    half = H_SB // 2
    ppage = page_rows // half
    width = half * D_SB
    bias_rows = jnp.repeat(sb_bias * LOG2E, t_len)[:, None]
    tri = jnp.asarray(_tri_np(ppage), BF16)
    fetch = page_table.reshape(batch, chunks_per_seq, SBS_CHUNK_PAGES)[:, ::-1, :].reshape(-1)
    fetch = jnp.concatenate([fetch, fetch[: (SBS_SLOTS - 1) * SBS_CHUNK_PAGES]])

    const2 = lambda b, pt: (0, 0)
    grid_spec = pltpu.PrefetchScalarGridSpec(
        num_scalar_prefetch=1,
        grid=(batch,),
        in_specs=[
            pl.BlockSpec((rows, 1), const2),
            pl.BlockSpec((ppage, ppage), const2),
            pl.BlockSpec((t_len, W_SB), lambda b, pt: (b, COL_SQ // W_SB)),
            pl.BlockSpec((rows, D_SB), lambda b, pt: (b, 0)),
            pl.BlockSpec((rows, D_SB), lambda b, pt: (b, 0)),
            pl.BlockSpec((t_len, W_SB), lambda b, pt: (b, COL_SG // W_SB)),
            pl.BlockSpec(memory_space=pl.ANY),
            pl.BlockSpec(memory_space=pl.ANY),
        ],
        out_specs=pl.BlockSpec((t_len, W_SB), lambda b, pt: (b, 0)),
        scratch_shapes=[
            pltpu.VMEM((SBS_SLOTS, SBS_CHUNK_PAGES, page_rows, D_SB), F32),
            pltpu.VMEM((SBS_SLOTS, SBS_CHUNK_PAGES, page_rows, D_SB), F32),
            pltpu.SemaphoreType.DMA((2, SBS_SLOTS)),
            pltpu.VMEM((SBS_CHUNK_PAGES * ppage, width), BF16),
            pltpu.VMEM((SBS_CHUNK_PAGES * ppage, width), BF16),
            pltpu.VMEM((SBS_CHUNK_PAGES * ppage, width), BF16),
            pltpu.VMEM((SBS_CHUNK_PAGES * ppage, width), BF16),
            pltpu.VMEM((ppage, width), BF16),
            pltpu.VMEM((ppage, width), BF16),
            pltpu.VMEM((ppage, width), F32),
            pltpu.VMEM((ppage, width), F32),
        ],
    )
    return pl.pallas_call(
        functools.partial(_sb_sample_body, chunks_per_seq=chunks_per_seq),
        grid_spec=grid_spec,
        out_shape=jax.ShapeDtypeStruct((batch * t_len, W_SB), F32),
        compiler_params=pltpu.CompilerParams(dimension_semantics=("arbitrary",), vmem_limit_bytes=VMEM_LIMIT),
        name="sb_sample",
    )(fetch, bias_rows, tri, p, sk, sv, p, pool_k, pool_v)


def _merge_body(x_ref, rb_ref, sb_ref, gr_ref, gs_ref, wr_ref, ws_ref, wo_ref, fg_ref, y_ref):
    m = gr_ref[...].astype(F32) * jnp.dot(rb_ref[...].astype(BF16), wr_ref[...], preferred_element_type=F32)
    m = m + gs_ref[...].astype(F32) * jnp.dot(sb_ref[...].astype(BF16), ws_ref[...], preferred_element_type=F32)
    hid = x_ref[...] + jnp.dot(m.astype(BF16), wo_ref[...], preferred_element_type=F32)
    r = lax.rsqrt(jnp.mean(hid * hid, axis=-1, keepdims=True) + EPS_RMS)
    y_ref[...] = hid * r * fg_ref[...]


def _merge(x, ret_branch, sb_branch, p, w_ret, w_sb, w_out, final_gain, *, tm):
    m = x.shape[0]
    resident = functools.partial(pl.BlockSpec, index_map=lambda i: (0, 0), pipeline_mode=pl.Buffered(1))
    return pl.pallas_call(
        _merge_body,
        grid=(m // tm,),
        in_specs=[
            pl.BlockSpec((tm, D_MODEL), lambda i: (i, 0)),
            pl.BlockSpec((tm, W_V), lambda i: (i, 0)),
            pl.BlockSpec((tm, W_SB), lambda i: (i, 0)),
            pl.BlockSpec((tm, D_MODEL), lambda i: (i, COL_GR // D_MODEL)),
            pl.BlockSpec((tm, D_MODEL), lambda i: (i, COL_GS // D_MODEL)),
            resident((W_V, D_MODEL)),
            resident((W_SB, D_MODEL)),
            resident((D_MODEL, D_MODEL)),
            pl.BlockSpec((1, D_MODEL), lambda i: (0, 0)),
        ],
        out_specs=pl.BlockSpec((tm, D_MODEL), lambda i: (i, 0)),
        out_shape=jax.ShapeDtypeStruct((m, D_MODEL), F32),
        compiler_params=pltpu.CompilerParams(dimension_semantics=("arbitrary",), vmem_limit_bytes=VMEM_LIMIT),
        name="merge",
    )(x, ret_branch, sb_branch, p, p, w_ret, w_sb, w_out, final_gain)


def _rope_tables(pos):
    half = DK_RET // 2
    inv = ROPE_BASE ** (-np.arange(half, dtype=np.float64) / half)
    ang = np.asarray(pos, np.float64)[:, None] * inv[None, :]
    cos, sin = np.cos(ang), np.sin(ang)
    cos_full = np.concatenate([cos, cos], axis=-1)
    sin_signed = np.concatenate([-sin, sin], axis=-1)
    return cos_full.astype(np.float32), sin_signed.astype(np.float32)


def _layer(xp, xs, state, pool_k, pool_v, page_table, norm_gain, w_in, sb_bias, gn_gain, w_ret, w_sb, w_out, final_gain):
    batch, seq, _ = xp.shape
    dec_batch, t_len, _ = xs.shape
    n_pool, page = pool_k.shape[:2]
    past_len = page_table.shape[1] * page

    w_ret_b, w_sb_b, w_out_b = w_ret.astype(BF16), w_sb.astype(BF16), w_out.astype(BF16)
    gain2 = norm_gain[None, :]
    gn_gain2 = gn_gain[None, :]
    final_gain2 = final_gain[None, :]

    x2p = xp.reshape(batch * seq, D_MODEL)
    x2s = xs.reshape(dec_batch * t_len, D_MODEL)
    tm_slab, tm_heads = 1024, 512
    cos_p, sin_p = _rope_tables(np.arange(seq))
    cos_s, sin_s = _rope_tables(past_len + np.arange(t_len))
    cos_s, sin_s = np.tile(cos_s, (tm_slab // t_len, 1)), np.tile(sin_s, (tm_slab // t_len, 1))

    xnp = _rmsnorm_bf16(x2p, gain2, tm=tm_heads)
    xns = _rmsnorm_bf16(x2s, gain2, tm=tm_heads)
    pp = _project_slab(xnp, cos_p, sin_p, w_in, tm=tm_slab, rope_period_tiles=seq // tm_slab, p_dtype=BF16)
    ps = _project_slab(xns, cos_s, sin_s, w_in, tm=tm_slab, rope_period_tiles=1, p_dtype=F32)
    skp, skp_heads = _project_heads(xnp, w_in, W_IN_COL_SK, tm=tm_heads)
    svp, svp_heads = _project_heads(xnp, w_in, W_IN_COL_SV, tm=tm_heads)
    sks, _ = _project_heads(xns, w_in, W_IN_COL_SK, tm=tm_heads)
    svs, _ = _project_heads(xns, w_in, W_IN_COL_SV, tm=tm_heads)

    rbp, state_p = _ret_prompt(pp, gn_gain2, batch, seq)
    rbs, state_s = _ret_sample(ps, state, gn_gain2, dec_batch, t_len)

    sbp = _sb_prompt(pp, skp_heads, svp_heads, sb_bias, batch, seq)
    sbs = _sb_sample(
        page_table, ps, sks, svs, pool_k.reshape(n_pool, page * H_SB, D_SB), pool_v.reshape(n_pool, page * H_SB, D_SB),
        sb_bias, dec_batch, t_len,
    )

    yp = _merge(x2p, rbp, sbp, pp, w_ret_b, w_sb_b, w_out_b, final_gain2, tm=256)
    ys = _merge(x2s, rbs, sbs, ps, w_ret_b, w_sb_b, w_out_b, final_gain2, tm=256)

    return (
        yp.reshape(batch, seq, D_MODEL),
        ys.reshape(dec_batch, t_len, D_MODEL),
        state_p,
        skp.reshape(batch, seq, H_SB, D_SB),
        svp.reshape(batch, seq, H_SB, D_SB),
        state_s,
        sks.reshape(dec_batch, t_len, H_SB, D_SB),
        svs.reshape(dec_batch, t_len, H_SB, D_SB),
    )


def kernel(x_prompt, x_sample, state_ret, cache_sb_k, cache_sb_v, page_table, norm_gain, w_in, sb_bias, ret_gn_gain, w_ret_proj, w_sb_proj, w_out, final_norm_gain):
    depth = w_in.shape[0]
    assert depth == 1, "single-layer trunk"
    outs = _layer(
        x_prompt, x_sample, state_ret[0], cache_sb_k[0], cache_sb_v[0], page_table, norm_gain[0], w_in[0],
        sb_bias[0], ret_gn_gain[0], w_ret_proj[0], w_sb_proj[0], w_out[0], final_norm_gain,
    )
    yp, ys, state_p, skp, svp, state_s, sks, svs = outs
    return (yp, ys, state_p[None], skp[None], svp[None], state_s[None], sks[None], svs[None])
```

```python
import functools

import jax
import jax.numpy as jnp
import numpy as np
from jax import lax
from jax.experimental import pallas as pl
from jax.experimental.pallas import tpu as pltpu

F32 = jnp.float32
BF16 = jnp.bfloat16

D_MODEL = 2048
H_RET = 8
DK_RET = 128
DV_RET = 256
RET_CHUNK = 128
H_SB = 8
D_SB = 128
ROPE_BASE = 10000.0
EPS_RMS = 1e-6
EPS_GN = 1e-5
W_QK = H_RET * DK_RET
W_V = H_RET * DV_RET
W_SB = H_SB * D_SB

PROJ_TN = 1024
PROJ_GROUP = 512
PROJ_GROUP_ROWS = 256
W_IN_COL_SK = 2 * W_QK + 2 * W_V + W_SB
W_IN_COL_SV = W_IN_COL_SK + W_SB
P_COLS = 2 * W_QK + 2 * W_V + 2 * W_SB + 2 * D_MODEL
N_P_TILES = P_COLS // PROJ_TN
COL_Q, COL_K, COL_V, COL_G, COL_SQ, COL_SG, COL_GR, COL_GS = 0, 1024, 2048, 4096, 6144, 7168, 8192, 10240

LOG2E = float(np.log2(np.e))
SB_Q_SCALE = D_SB**-0.5 * LOG2E

VMEM_LIMIT = 56 * 1024 * 1024

_NT = (((1,), (1,)), ((), ()))


def _silu(g):
    return g * jax.nn.sigmoid(g)


def _rmsnorm_body(x_ref, gain_ref, xn_ref):
    x = x_ref[...]
    r = lax.rsqrt(jnp.mean(x * x, axis=-1, keepdims=True) + EPS_RMS)
    xn_ref[...] = (x * r * gain_ref[...]).astype(xn_ref.dtype)


def _rmsnorm_bf16(x, gain, *, tm):
    m = x.shape[0]
    return pl.pallas_call(
        _rmsnorm_body,
        grid=(m // tm,),
        in_specs=[pl.BlockSpec((tm, D_MODEL), lambda i: (i, 0)), pl.BlockSpec((1, D_MODEL), lambda i: (0, 0))],
        out_specs=pl.BlockSpec((tm, D_MODEL), lambda i: (i, 0)),
        out_shape=jax.ShapeDtypeStruct((m, D_MODEL), BF16),
        compiler_params=pltpu.CompilerParams(dimension_semantics=("arbitrary",), vmem_limit_bytes=VMEM_LIMIT),
        name="rmsnorm",
    )(x, gain)


def _proj_groups(xn_ref, wb_ref, row_groups):
    rows = xn_ref.shape[0] // row_groups
    for c in range(PROJ_TN // PROJ_GROUP):
        for r in range(row_groups):
            acc = jnp.dot(
                xn_ref[r * rows : (r + 1) * rows, :],
                wb_ref[:, c * PROJ_GROUP : (c + 1) * PROJ_GROUP],
                preferred_element_type=F32,
            )
            yield r * rows, c * PROJ_GROUP, rows, acc


def _lane_blocks(acc):
    for h in range(PROJ_GROUP // DK_RET):
        yield h * DK_RET, acc[:, h * DK_RET : (h + 1) * DK_RET]


def _proj_slab_body(xn_ref, cos_ref, sin_ref, w_ref, p_ref, wb_ref, *, row_groups):
    j = pl.program_id(0)

    @pl.when(pl.program_id(1) == 0)
    def _cast():
        wb_ref[...] = w_ref[...].astype(BF16)

    groups = functools.partial(_proj_groups, xn_ref, wb_ref, row_groups)

    @pl.when(j < COL_V // PROJ_TN)
    def _rope():
        scale = jnp.where(j == COL_K // PROJ_TN, F32(DK_RET**-0.5), F32(1.0))
        for row0, col0, rows, acc in groups():
            cos = cos_ref[row0 : row0 + rows, :] * scale
            sin = sin_ref[row0 : row0 + rows, :] * scale
            for off, xh in _lane_blocks(acc):
                y = xh * cos + pltpu.roll(xh, DK_RET // 2, axis=1) * sin
                p_ref[row0 : row0 + rows, col0 + off : col0 + off + DK_RET] = y.astype(p_ref.dtype)

    @pl.when((j >= COL_V // PROJ_TN) & (j < COL_GR // PROJ_TN))
    def _plain():
        scale = jnp.where(j == COL_SQ // PROJ_TN, F32(SB_Q_SCALE), F32(1.0))
        for row0, col0, rows, acc in groups():
            p_ref[row0 : row0 + rows, col0 : col0 + PROJ_GROUP] = (acc * scale).astype(p_ref.dtype)

    @pl.when(j >= COL_GR // PROJ_TN)
    def _gate():
        for row0, col0, rows, acc in groups():
            y = 0.5 * jnp.tanh(0.5 * acc) + 0.5
            p_ref[row0 : row0 + rows, col0 : col0 + PROJ_GROUP] = y.astype(p_ref.dtype)


def _w_tile(j):
    first_kv = COL_SQ // PROJ_TN + 1
    return jnp.where(j < first_kv, j, j + 2 * W_SB // PROJ_TN)


def _project_slab(xn, cos, sin, w_in, *, tm, rope_period_tiles, p_dtype):
    m = xn.shape[0]
    return pl.pallas_call(
        functools.partial(_proj_slab_body, row_groups=tm // PROJ_GROUP_ROWS),
        grid=(N_P_TILES, m // tm),
        in_specs=[
            pl.BlockSpec((tm, D_MODEL), lambda j, i: (i, 0)),
            pl.BlockSpec((tm, DK_RET), lambda j, i: (i % rope_period_tiles, 0)),
            pl.BlockSpec((tm, DK_RET), lambda j, i: (i % rope_period_tiles, 0)),
            pl.BlockSpec((D_MODEL, PROJ_TN), lambda j, i: (0, _w_tile(j))),
        ],
        out_specs=pl.BlockSpec((tm, PROJ_TN), lambda j, i: (i, j)),
        out_shape=jax.ShapeDtypeStruct((m, P_COLS), p_dtype),
        scratch_shapes=[pltpu.VMEM((D_MODEL, PROJ_TN), BF16)],
        compiler_params=pltpu.CompilerParams(
            dimension_semantics=("arbitrary", "arbitrary"), vmem_limit_bytes=VMEM_LIMIT
        ),
        name="proj",
    )(xn, cos, sin, w_in)


def _proj_heads_body(xn_ref, w_ref, tok_ref, head_ref, wb_ref, *, row_groups):
    @pl.when(pl.program_id(0) == 0)
    def _cast():
        wb_ref[...] = w_ref[...].astype(BF16)

    for row0, col0, rows, acc in _proj_groups(xn_ref, wb_ref, row_groups):
        for off, ah in _lane_blocks(acc):
            head = (col0 + off) // D_SB
            tok_ref[pl.ds(row0 * H_SB + head, rows, stride=H_SB), :] = ah
            head_ref[head, row0 : row0 + rows, :] = ah.astype(BF16)


def _project_heads(xn, w_in, col, *, tm):
    m = xn.shape[0]
    return pl.pallas_call(
        functools.partial(_proj_heads_body, row_groups=tm // PROJ_GROUP_ROWS),
        grid=(m // tm,),
        in_specs=[
            pl.BlockSpec((tm, D_MODEL), lambda i: (i, 0)),
            pl.BlockSpec((D_MODEL, W_SB), lambda i: (0, col // W_SB)),
        ],
        out_specs=[
            pl.BlockSpec((tm * H_SB, D_SB), lambda i: (i, 0)),
            pl.BlockSpec((H_SB, tm, D_SB), lambda i: (0, i, 0)),
        ],
        out_shape=[
            jax.ShapeDtypeStruct((m * H_SB, D_SB), F32),
            jax.ShapeDtypeStruct((H_SB, m, D_SB), BF16),
        ],
        scratch_shapes=[pltpu.VMEM((D_MODEL, W_SB), BF16)],
        compiler_params=pltpu.CompilerParams(dimension_semantics=("arbitrary",), vmem_limit_bytes=VMEM_LIMIT),
        name="proj_heads",
    )(xn, w_in)


def _group_norm_gate(o, gain, g):
    mu = jnp.mean(o, axis=-1, keepdims=True)
    d = o - mu
    var = jnp.mean(d * d, axis=-1, keepdims=True)
    return d * lax.rsqrt(var + EPS_GN) * gain * _silu(g)


def _ret_prompt_body(q_ref, k_ref, v_ref, g_ref, gain_ref, decay_ref, cd_ref, kd_ref, gs_ref, out_ref, s_ref):
    c = pl.program_id(1)

    @pl.when(c == 0)
    def _init():
        s_ref[...] = jnp.zeros_like(s_ref)

    heads = range(H_RET)
    q = [q_ref[:, h * DK_RET : (h + 1) * DK_RET] for h in heads]
    k = [k_ref[:, h * DK_RET : (h + 1) * DK_RET] for h in heads]
    v = [v_ref[:, h * DV_RET : (h + 1) * DV_RET] for h in heads]
    s = [s_ref[0, h] for h in heads]
    inner = [lax.dot_general(q[h], k[h], _NT, preferred_element_type=F32) for h in heads]
    cross = [jnp.dot(q[h], s[h].astype(BF16), preferred_element_type=F32) for h in heads]
    for h in heads:
        kdec_t = (k[h].astype(F32) * kd_ref[:, h : h + 1]).T.astype(BF16)
        s_ref[0, h] = gs_ref[h] * s[h] + jnp.dot(kdec_t, v[h], preferred_element_type=F32)
    for h in heads:
        o = jnp.dot((inner[h] * decay_ref[h]).astype(BF16), v[h], preferred_element_type=F32)
        o = o + cross[h] * cd_ref[:, h : h + 1]
        g = g_ref[:, h * DV_RET : (h + 1) * DV_RET].astype(F32)
        gain = gain_ref[:, h * DV_RET : (h + 1) * DV_RET]
        out_ref[:, h * DV_RET : (h + 1) * DV_RET] = _group_norm_gate(o, gain, g).astype(out_ref.dtype)


def _ret_tables(chunk):
    lg = np.log1p(-np.exp2(-5.0 - np.arange(H_RET, dtype=np.float64)))
    idx = np.arange(chunk, dtype=np.float64)
    diff = idx[:, None] - idx[None, :]
    decay = np.where(diff >= 0, np.exp(lg[:, None, None] * np.maximum(diff, 0.0)[None]), 0.0)
    cross = np.exp(lg[:, None] * (idx + 1.0)[None, :]).T
    kdec = np.exp(lg[:, None] * (chunk - 1.0 - idx)[None, :]).T
    sdec = np.broadcast_to(np.exp(lg * chunk)[:, None, None], (H_RET, 1, DV_RET))
    return tuple(np.asarray(a, np.float32) for a in (decay, cross, kdec, sdec))


def _ret_prompt(p, gn_gain, batch, seq):
    nc = seq // RET_CHUNK
    decay, cross, kdec, sdec = _ret_tables(RET_CHUNK)
    row = lambda b, c: b * nc + c
    const2 = lambda b, c: (0, 0)
    const3 = lambda b, c: (0, 0, 0)
    return pl.pallas_call(
        _ret_prompt_body,
        grid=(batch, nc),
        in_specs=[
            pl.BlockSpec((RET_CHUNK, W_QK), lambda b, c: (row(b, c), COL_Q // W_QK)),
            pl.BlockSpec((RET_CHUNK, W_QK), lambda b, c: (row(b, c), COL_K // W_QK)),
            pl.BlockSpec((RET_CHUNK, W_V), lambda b, c: (row(b, c), COL_V // W_V)),
            pl.BlockSpec((RET_CHUNK, W_V), lambda b, c: (row(b, c), COL_G // W_V)),
            pl.BlockSpec((1, W_V), const2),
            pl.BlockSpec((H_RET, RET_CHUNK, RET_CHUNK), const3),
            pl.BlockSpec((RET_CHUNK, H_RET), const2),
            pl.BlockSpec((RET_CHUNK, H_RET), const2),
            pl.BlockSpec((H_RET, 1, DV_RET), const3),
        ],
        out_specs=[
            pl.BlockSpec((RET_CHUNK, W_V), lambda b, c: (row(b, c), 0)),
            pl.BlockSpec((1, H_RET, DK_RET, DV_RET), lambda b, c: (b, 0, 0, 0)),
        ],
        out_shape=[
            jax.ShapeDtypeStruct((batch * seq, W_V), BF16),
            jax.ShapeDtypeStruct((batch, H_RET, DK_RET, DV_RET), F32),
        ],
        compiler_params=pltpu.CompilerParams(
            dimension_semantics=("arbitrary", "arbitrary"), vmem_limit_bytes=VMEM_LIMIT
        ),
        name="ret_prompt",
    )(p, p, p, p, gn_gain, decay, cross, kdec, sdec)


RS_SEQS = 16


def _ret_sample_body(q_ref, k_ref, v_ref, g_ref, s_ref, gain_ref, decay_ref, cd_ref, kd_ref, gs_ref, out_ref, snew_ref):
    n_seq = s_ref.shape[0]
    rows = q_ref.shape[0]
    t_len = rows // n_seq
    q = q_ref[...]
    k = k_ref[...]
    v = v_ref[...].astype(BF16)
    s = s_ref[...]
    inner = lax.dot_general(q.astype(BF16), k.astype(BF16), _NT, preferred_element_type=F32) * decay_ref[...]
    o = jnp.dot(inner.astype(BF16), v, preferred_element_type=F32)
    row_seq = lax.broadcasted_iota(jnp.int32, (rows, DK_RET), 0) // t_len
    q_exp = jnp.concatenate([jnp.where(row_seq == b, q, 0.0) for b in range(n_seq)], axis=1).astype(BF16)
    s_rows = s.reshape(n_seq * DK_RET, DV_RET).astype(BF16)
    o = o + jnp.dot(q_exp, s_rows, preferred_element_type=F32) * cd_ref[...]
    kdec_t = (k * kd_ref[...]).T
    col_seq = lax.broadcasted_iota(jnp.int32, (DK_RET, rows), 1) // t_len
    k_exp_t = jnp.concatenate([jnp.where(col_seq == b, kdec_t, 0.0) for b in range(n_seq)], axis=0).astype(BF16)
    upd = jnp.dot(k_exp_t, v, preferred_element_type=F32)
    snew_ref[...] = gs_ref[...] * s + upd.reshape(n_seq, DK_RET, DV_RET)
    out_ref[...] = _group_norm_gate(o, gain_ref[...], g_ref[...])


def _ret_sample(p, state, gn_gain, batch, t_len):
    decay, cross, kdec, sdec = _ret_tables(t_len)
    rows = RS_SEQS * t_len
    decay_bd = np.stack([np.kron(np.eye(RS_SEQS, dtype=np.float32), decay[h]) for h in range(H_RET)])
    cross_rows = np.tile(cross.T, (1, RS_SEQS))[:, :, None]
    kdec_rows = np.tile(kdec.T, (1, RS_SEQS))[:, :, None]
    qk_blk = lambda col: pl.BlockSpec((rows, DK_RET), lambda i, h: (i, col // DK_RET + h))
    v_blk = lambda col: pl.BlockSpec((rows, DV_RET), lambda i, h: (i, col // DV_RET + h))
    state_blk = pl.BlockSpec((RS_SEQS, None, DK_RET, DV_RET), lambda i, h: (i, h, 0, 0))
    return pl.pallas_call(
        _ret_sample_body,
        grid=(batch // RS_SEQS, H_RET),
        in_specs=[
            qk_blk(COL_Q),
            qk_blk(COL_K),
            v_blk(COL_V),
            v_blk(COL_G),
            state_blk,
            pl.BlockSpec((1, DV_RET), lambda i, h: (0, h)),
            pl.BlockSpec((None, rows, rows), lambda i, h: (h, 0, 0)),
            pl.BlockSpec((None, rows, 1), lambda i, h: (h, 0, 0)),
            pl.BlockSpec((None, rows, 1), lambda i, h: (h, 0, 0)),
            pl.BlockSpec((None, 1, DV_RET), lambda i, h: (h, 0, 0)),
        ],
        out_specs=[pl.BlockSpec((rows, DV_RET), lambda i, h: (i, h)), state_blk],
        out_shape=[
            jax.ShapeDtypeStruct((batch * t_len, W_V), F32),
            jax.ShapeDtypeStruct((batch, H_RET, DK_RET, DV_RET), F32),
        ],
        compiler_params=pltpu.CompilerParams(
            dimension_semantics=("arbitrary", "arbitrary"), vmem_limit_bytes=VMEM_LIMIT
        ),
        name="ret_sample",
    )(p, p, p, p, state, gn_gain, decay_bd, cross_rows, kdec_rows, sdec)


def _sb_weights(qk2, tri, bias2, carry, mask):
    drop, log2_beta = _sb_drop(qk2, bias2, mask)
    return _sb_finish(drop, log2_beta, _sb_after(drop, tri), carry, mask)


def _sb_drop(qk2, bias2, mask):
    z2 = qk2 + bias2
    drop = jnp.maximum(z2, 0.0) + jnp.log2(1.0 + jnp.exp2(-jnp.abs(z2)))
    log2_beta = z2 - drop
    if mask is not None:
        drop = jnp.where(mask, drop, 0.0)
    return drop, log2_beta


def _sb_after(drop, tri):
    return jnp.dot(drop.astype(BF16), tri, preferred_element_type=F32)


def _sb_finish(drop, log2_beta, after, carry, mask):
    a = jnp.exp2(log2_beta - after - carry)
    if mask is not None:
        a = jnp.where(mask, a, 0.0)
    return a, carry + after[:, 0:1] + drop[:, 0:1]


def _tri_np(n):
    return np.tril(np.ones((n, n), np.float32), -1)


SBP_T = 256
SBP_HEADS = 8


def _sb_prompt_body(bias_ref, q_ref, k_ref, v_ref, sg_ref, tri_ref, out_ref):
    hg = pl.program_id(1)
    i = pl.program_id(2)
    heads = range(SBP_HEADS)
    bias2 = [bias_ref[hg * SBP_HEADS + g] for g in heads]
    tri = tri_ref[...]
    q = [q_ref[:, g * D_SB : (g + 1) * D_SB] for g in heads]
    r = lax.broadcasted_iota(jnp.int32, (SBP_T, SBP_T), 0)
    c = lax.broadcasted_iota(jnp.int32, (SBP_T, SBP_T), 1)

    def tile(kb, state, mask):
        off = pl.multiple_of(kb * SBP_T, SBP_T)
        qk2 = [lax.dot_general(q[g], k_ref[g, pl.ds(off, SBP_T), :], _NT, preferred_element_type=F32) for g in heads]
        dl = [_sb_drop(qk2[g], bias2[g], mask) for g in heads]
        after = [_sb_after(dl[g][0], tri) for g in heads]
        new_state = []
        for g in heads:
            carry, acc = state[g]
            a, carry = _sb_finish(dl[g][0], dl[g][1], after[g], carry, mask)
            acc = acc + jnp.dot(a.astype(BF16), v_ref[g, pl.ds(off, SBP_T), :], preferred_element_type=F32)
            new_state.append((carry, acc))
        return tuple(new_state)

    state = tuple((jnp.zeros((SBP_T, 1), F32), jnp.zeros((SBP_T, D_SB), F32)) for _ in heads)
    state = tile(i, state, c < r)
    state = lax.fori_loop(0, i, lambda n, st: tile(i - 1 - n, st, None), state)
    acc = jnp.concatenate([st[1] for st in state], axis=1)
    out_ref[...] = (acc * _silu(sg_ref[...].astype(F32))).astype(out_ref.dtype)


def _sb_prompt(p, sk_heads, sv_heads, sb_bias, batch, seq):
    nq = seq // SBP_T
    width = SBP_HEADS * D_SB
    return pl.pallas_call(
        _sb_prompt_body,
        grid=(batch, H_SB // SBP_HEADS, nq),
        in_specs=[
            pl.BlockSpec(memory_space=pltpu.SMEM),
            pl.BlockSpec((SBP_T, width), lambda b, h, i: (b * nq + i, COL_SQ // width + h)),
            pl.BlockSpec((SBP_HEADS, seq, D_SB), lambda b, h, i: (h, b, 0)),
            pl.BlockSpec((SBP_HEADS, seq, D_SB), lambda b, h, i: (h, b, 0)),
            pl.BlockSpec((SBP_T, width), lambda b, h, i: (b * nq + i, COL_SG // width + h)),
            pl.BlockSpec((SBP_T, SBP_T), lambda b, h, i: (0, 0)),
        ],
        out_specs=pl.BlockSpec((SBP_T, width), lambda b, h, i: (b * nq + i, h)),
        out_shape=jax.ShapeDtypeStruct((batch * seq, W_SB), BF16),
        compiler_params=pltpu.CompilerParams(
            dimension_semantics=("arbitrary", "arbitrary", "arbitrary"), vmem_limit_bytes=VMEM_LIMIT
        ),
        name="sb_prompt",
    )(sb_bias * LOG2E, p, sk_heads, sv_heads, p, jnp.asarray(_tri_np(SBP_T), BF16))


SBS_CHUNK_PAGES = 4


SBS_SLOTS = 4


def _sb_sample_body(
    pt_ref, bias_ref, tri_ref, q_ref, kn_ref, vn_ref, sg_ref, kpool_ref, vpool_ref,
    out_ref, kbuf_ref, vbuf_ref, sem_ref, kd0_ref, kd1_ref, vd0_ref, vd1_ref, knd_ref, vnd_ref, kstage_ref, vstage_ref,
    *, chunks_per_seq,
):
    b = pl.program_id(0)
    n_seq = pl.num_programs(0)
    t_len = q_ref.shape[0]
    rows = H_SB * t_len
    half = H_SB // 2
    ppage = kbuf_ref.shape[2] // half
    lookahead = SBS_SLOTS - 1
    kd_refs, vd_refs = (kd0_ref, kd1_ref), (vd0_ref, vd1_ref)

    def chunk_copies(n, slot):
        copies = []
        for pg in range(SBS_CHUNK_PAGES):
            page_id = pt_ref[n * SBS_CHUNK_PAGES + pg]
            copies.append(pltpu.make_async_copy(kpool_ref.at[page_id], kbuf_ref.at[slot, pg], sem_ref.at[0, slot]))
            copies.append(pltpu.make_async_copy(vpool_ref.at[page_id], vbuf_ref.at[slot, pg], sem_ref.at[1, slot]))
        return copies

    def densify(src_ref, dst_ref, row0, n_pseudo):
        for j in range(half):
            dst_ref[row0 : row0 + n_pseudo, j * D_SB : (j + 1) * D_SB] = src_ref[
                pl.ds(j, n_pseudo, stride=half), :
            ].astype(dst_ref.dtype)

    @pl.when(b == 0)
    def _prime():
        for n in range(lookahead):
            for cp in chunk_copies(n, n % SBS_SLOTS):
                cp.start()

    q = q_ref[...]
    zeros = jnp.zeros((t_len, D_SB), F32)
    qbd = jnp.concatenate(
        [
            jnp.concatenate([q[:, h * D_SB : (h + 1) * D_SB] if j == h % half else zeros for j in range(half)], axis=1)
            for h in range(H_SB)
        ],
        axis=0,
    ).astype(BF16)
    bias2 = bias_ref[...]
    tri = tri_ref[...]

    def parity_ok(n_pseudo):
        r = lax.broadcasted_iota(jnp.int32, (rows, n_pseudo), 0)
        p = lax.broadcasted_iota(jnp.int32, (rows, n_pseudo), 1)
        return (p % 2) == (r // (half * t_len)), r, p

    def sweep(k, v, n_pages, carry, acc, mask):
        qk2 = lax.dot_general(qbd, k, _NT, preferred_element_type=F32)
        drop, log2_beta = _sb_drop(qk2, bias2, mask)
        pages = [slice(pg * ppage, (pg + 1) * ppage) for pg in range(n_pages)]
        stacked = jnp.concatenate([drop[:, cols] for cols in pages], axis=0)
        after_stacked = _sb_after(stacked, tri)
        afters = [after_stacked[pg * rows : (pg + 1) * rows] for pg in range(n_pages)]
        shifted = [None] * n_pages
        for pg in reversed(range(n_pages)):
            shifted[pg] = afters[pg] + carry
            carry = carry + afters[pg][:, 0:1] + drop[:, pg * ppage : pg * ppage + 1]
        a = jnp.where(mask, jnp.exp2(log2_beta - jnp.concatenate(shifted, axis=1)), 0.0)
        return carry, acc + jnp.dot(a.astype(BF16), v, preferred_element_type=F32)

    for src_ref, stage_ref, dst_ref in ((kn_ref, kstage_ref, knd_ref), (vn_ref, vstage_ref, vnd_ref)):
        stage_ref[...] = jnp.zeros_like(stage_ref)
        densify(src_ref, stage_ref, 0, 2 * t_len)
        dst_ref[...] = stage_ref[...].astype(BF16)
    ok, r, p = parity_ok(ppage)
    carry, acc = sweep(
        knd_ref[...], vnd_ref[...], 1,
        jnp.zeros((rows, 1), F32), jnp.zeros((rows, half * D_SB), F32), ok & ((p // 2) < (r % t_len)),
    )

    chunk_ok, _, _ = parity_ok(SBS_CHUNK_PAGES * ppage)
    for c in range(chunks_per_seq):
        n = b * chunks_per_seq + c
        slot = c % SBS_SLOTS
        for cp in chunk_copies(n, slot):
            cp.wait()
        for cp in chunk_copies(n + lookahead, (c + lookahead) % SBS_SLOTS):
            cp.start()
        kd_ref, vd_ref = kd_refs[c % 2], vd_refs[c % 2]
        for pg in range(SBS_CHUNK_PAGES):
            densify(kbuf_ref.at[slot, pg], kd_ref, pg * ppage, ppage)
            densify(vbuf_ref.at[slot, pg], vd_ref, pg * ppage, ppage)
        carry, acc = sweep(kd_ref[...], vd_ref[...], SBS_CHUNK_PAGES, carry, acc, chunk_ok)

    o = jnp.concatenate(
        [acc[h * t_len : (h + 1) * t_len, (h % half) * D_SB : (h % half + 1) * D_SB] for h in range(H_SB)], axis=1
    )
    out_ref[...] = o * _silu(sg_ref[...])

    @pl.when(b == n_seq - 1)
    def _drain():
        for i in range(lookahead):
            for cp in chunk_copies(n_seq * chunks_per_seq + i, i % SBS_SLOTS):
                cp.wait()


def _sb_sample(page_table, p, sk, sv, pool_k, pool_v, sb_bias, batch, t_len):
    n_pages = page_table.shape[1]
    chunks_per_seq = n_pages // SBS_CHUNK_PAGES
    assert n_pages % SBS_CHUNK_PAGES == 0 and chunks_per_seq % SBS_SLOTS == 0
    page_rows = pool_k.shape[1]
    rows = H_SB * t_len
    half = H_SB // 2
    ppage = page_rows // half
    width = half * D_SB
    bias_rows = jnp.repeat(sb_bias * LOG2E, t_len)[:, None]
    tri = jnp.asarray(_tri_np(ppage), BF16)
    fetch = page_table.reshape(batch, chunks_per_seq, SBS_CHUNK_PAGES)[:, ::-1, :].reshape(-1)
    fetch = jnp.concatenate([fetch, fetch[: (SBS_SLOTS - 1) * SBS_CHUNK_PAGES]])

    const2 = lambda b, pt: (0, 0)
    grid_spec = pltpu.PrefetchScalarGridSpec(
        num_scalar_prefetch=1,
        grid=(batch,),
        in_specs=[
            pl.BlockSpec((rows, 1), const2),
            pl.BlockSpec((ppage, ppage), const2),
            pl.BlockSpec((t_len, W_SB), lambda b, pt: (b, COL_SQ // W_SB)),
            pl.BlockSpec((rows, D_SB), lambda b, pt: (b, 0)),
            pl.BlockSpec((rows, D_SB), lambda b, pt: (b, 0)),
            pl.BlockSpec((t_len, W_SB), lambda b, pt: (b, COL_SG // W_SB)),
            pl.BlockSpec(memory_space=pl.ANY),
            pl.BlockSpec(memory_space=pl.ANY),
        ],
        out_specs=pl.BlockSpec((t_len, W_SB), lambda b, pt: (b, 0)),
        scratch_shapes=[
            pltpu.VMEM((SBS_SLOTS, SBS_CHUNK_PAGES, page_rows, D_SB), F32),
            pltpu.VMEM((SBS_SLOTS, SBS_CHUNK_PAGES, page_rows, D_SB), F32),
            pltpu.SemaphoreType.DMA((2, SBS_SLOTS)),
            pltpu.VMEM((SBS_CHUNK_PAGES * ppage, width), BF16),
            pltpu.VMEM((SBS_CHUNK_PAGES * ppage, width), BF16),
            pltpu.VMEM((SBS_CHUNK_PAGES * ppage, width), BF16),
            pltpu.VMEM((SBS_CHUNK_PAGES * ppage, width), BF16),
            pltpu.VMEM((ppage, width), BF16),
            pltpu.VMEM((ppage, width), BF16),
            pltpu.VMEM((ppage, width), F32),
            pltpu.VMEM((ppage, width), F32),
        ],
    )
    return pl.pallas_call(
        functools.partial(_sb_sample_body, chunks_per_seq=chunks_per_seq),
        grid_spec=grid_spec,
        out_shape=jax.ShapeDtypeStruct((batch * t_len, W_SB), F32),
        compiler_params=pltpu.CompilerParams(dimension_semantics=("arbitrary",), vmem_limit_bytes=VMEM_LIMIT),
        name="sb_sample",
    )(fetch, bias_rows, tri, p, sk, sv, p, pool_k, pool_v)


def _merge_body(x_ref, rb_ref, sb_ref, gr_ref, gs_ref, wr_ref, ws_ref, wo_ref, fg_ref, y_ref):
    m = gr_ref[...].astype(F32) * jnp.dot(rb_ref[...].astype(BF16), wr_ref[...], preferred_element_type=F32)
    m = m + gs_ref[...].astype(F32) * jnp.dot(sb_ref[...].astype(BF16), ws_ref[...], preferred_element_type=F32)
    hid = x_ref[...] + jnp.dot(m.astype(BF16), wo_ref[...], preferred_element_type=F32)
    r = lax.rsqrt(jnp.mean(hid * hid, axis=-1, keepdims=True) + EPS_RMS)
    y_ref[...] = hid * r * fg_ref[...]


def _merge(x, ret_branch, sb_branch, p, w_ret, w_sb, w_out, final_gain, *, tm):
    m = x.shape[0]
    resident = functools.partial(pl.BlockSpec, index_map=lambda i: (0, 0), pipeline_mode=pl.Buffered(1))
    return pl.pallas_call(
        _merge_body,
        grid=(m // tm,),
        in_specs=[
            pl.BlockSpec((tm, D_MODEL), lambda i: (i, 0)),
            pl.BlockSpec((tm, W_V), lambda i: (i, 0)),
            pl.BlockSpec((tm, W_SB), lambda i: (i, 0)),
            pl.BlockSpec((tm, D_MODEL), lambda i: (i, COL_GR // D_MODEL)),
            pl.BlockSpec((tm, D_MODEL), lambda i: (i, COL_GS // D_MODEL)),
            resident((W_V, D_MODEL)),
            resident((W_SB, D_MODEL)),
            resident((D_MODEL, D_MODEL)),
            pl.BlockSpec((1, D_MODEL), lambda i: (0, 0)),
        ],
        out_specs=pl.BlockSpec((tm, D_MODEL), lambda i: (i, 0)),
        out_shape=jax.ShapeDtypeStruct((m, D_MODEL), F32),
        compiler_params=pltpu.CompilerParams(dimension_semantics=("arbitrary",), vmem_limit_bytes=VMEM_LIMIT),
        name="merge",
    )(x, ret_branch, sb_branch, p, p, w_ret, w_sb, w_out, final_gain)


def _rope_tables(pos):
    half = DK_RET // 2
    inv = ROPE_BASE ** (-np.arange(half, dtype=np.float64) / half)
    ang = np.asarray(pos, np.float64)[:, None] * inv[None, :]
    cos, sin = np.cos(ang), np.sin(ang)
    cos_full = np.concatenate([cos, cos], axis=-1)
    sin_signed = np.concatenate([-sin, sin], axis=-1)
    return cos_full.astype(np.float32), sin_signed.astype(np.float32)


def _layer(xp, xs, state, pool_k, pool_v, page_table, norm_gain, w_in, sb_bias, gn_gain, w_ret, w_sb, w_out, final_gain):
    batch, seq, _ = xp.shape
    dec_batch, t_len, _ = xs.shape
    n_pool, page = pool_k.shape[:2]
    past_len = page_table.shape[1] * page

    w_ret_b, w_sb_b, w_out_b = w_ret.astype(BF16), w_sb.astype(BF16), w_out.astype(BF16)
    gain2 = norm_gain[None, :]
    gn_gain2 = gn_gain[None, :]
    final_gain2 = final_gain[None, :]

    x2p = xp.reshape(batch * seq, D_MODEL)
    x2s = xs.reshape(dec_batch * t_len, D_MODEL)
    tm_slab, tm_heads = 1024, 512
    cos_p, sin_p = _rope_tables(np.arange(seq))
    cos_s, sin_s = _rope_tables(past_len + np.arange(t_len))
    cos_s, sin_s = np.tile(cos_s, (tm_slab // t_len, 1)), np.tile(sin_s, (tm_slab // t_len, 1))

    xnp = _rmsnorm_bf16(x2p, gain2, tm=tm_heads)
    xns = _rmsnorm_bf16(x2s, gain2, tm=tm_heads)
    pp = _project_slab(xnp, cos_p, sin_p, w_in, tm=tm_slab, rope_period_tiles=seq // tm_slab, p_dtype=BF16)
    ps = _project_slab(xns, cos_s, sin_s, w_in, tm=tm_slab, rope_period_tiles=1, p_dtype=F32)
    skp, skp_heads = _project_heads(xnp, w_in, W_IN_COL_SK, tm=tm_heads)
    svp, svp_heads = _project_heads(xnp, w_in, W_IN_COL_SV, tm=tm_heads)
    sks, _ = _project_heads(xns, w_in, W_IN_COL_SK, tm=tm_heads)
    svs, _ = _project_heads(xns, w_in, W_IN_COL_SV, tm=tm_heads)

    rbp, state_p = _ret_prompt(pp, gn_gain2, batch, seq)
    rbs, state_s = _ret_sample(ps, state, gn_gain2, dec_batch, t_len)

    sbp = _sb_prompt(pp, skp_heads, svp_heads, sb_bias, batch, seq)
    sbs = _sb_sample(
        page_table, ps, sks, svs, pool_k.reshape(n_pool, page * H_SB, D_SB), pool_v.reshape(n_pool, page * H_SB, D_SB),
        sb_bias, dec_batch, t_len,
    )

    yp = _merge(x2p, rbp, sbp, pp, w_ret_b, w_sb_b, w_out_b, final_gain2, tm=256)
    ys = _merge(x2s, rbs, sbs, ps, w_ret_b, w_sb_b, w_out_b, final_gain2, tm=256)

    return (
        yp.reshape(batch, seq, D_MODEL),
        ys.reshape(dec_batch, t_len, D_MODEL),
        state_p,
        skp.reshape(batch, seq, H_SB, D_SB),
        svp.reshape(batch, seq, H_SB, D_SB),
        state_s,
        sks.reshape(dec_batch, t_len, H_SB, D_SB),
        svs.reshape(dec_batch, t_len, H_SB, D_SB),
    )


def kernel(x_prompt, x_sample, state_ret, cache_sb_k, cache_sb_v, page_table, norm_gain, w_in, sb_bias, ret_gn_gain, w_ret_proj, w_sb_proj, w_out, final_norm_gain):
    depth = w_in.shape[0]
    assert depth == 1, "single-layer trunk"
    outs = _layer(
        x_prompt, x_sample, state_ret[0], cache_sb_k[0], cache_sb_v[0], page_table, norm_gain[0], w_in[0],
        sb_bias[0], ret_gn_gain[0], w_ret_proj[0], w_sb_proj[0], w_out[0], final_norm_gain,
    )
    yp, ys, state_p, skp, svp, state_s, sks, svs = outs
    return (yp, ys, state_p[None], skp[None], svp[None], state_s[None], sks[None], svs[None])
```

```python
import functools

import jax
import jax.numpy as jnp
import numpy as np
from jax import lax
from jax.experimental import pallas as pl
from jax.experimental.pallas import tpu as pltpu

F32 = jnp.float32
BF16 = jnp.bfloat16

D_MODEL = 2048
H_RET = 8
DK_RET = 128
DV_RET = 256
RET_CHUNK = 128
H_SB = 8
D_SB = 128
ROPE_BASE = 10000.0
EPS_RMS = 1e-6
EPS_GN = 1e-5
W_QK = H_RET * DK_RET
W_V = H_RET * DV_RET
W_SB = H_SB * D_SB

PROJ_TN = 1024
PROJ_GROUP = 512
PROJ_GROUP_ROWS = 256
W_IN_COL_SK = 2 * W_QK + 2 * W_V + W_SB
W_IN_COL_SV = W_IN_COL_SK + W_SB
P_COLS = 2 * W_QK + 2 * W_V + 2 * W_SB + 2 * D_MODEL
N_P_TILES = P_COLS // PROJ_TN
COL_Q, COL_K, COL_V, COL_G, COL_SQ, COL_SG, COL_GR, COL_GS = 0, 1024, 2048, 4096, 6144, 7168, 8192, 10240

LOG2E = float(np.log2(np.e))
SB_Q_SCALE = D_SB**-0.5 * LOG2E

VMEM_LIMIT = 56 * 1024 * 1024

_NT = (((1,), (1,)), ((), ()))


def _silu(g):
    return g * jax.nn.sigmoid(g)


def _rmsnorm_body(x_ref, gain_ref, xn_ref):
    x = x_ref[...]
    r = lax.rsqrt(jnp.mean(x * x, axis=-1, keepdims=True) + EPS_RMS)
    xn_ref[...] = (x * r * gain_ref[...]).astype(xn_ref.dtype)


def _proj_groups(xn_ref, wb_ref, row_groups):
    rows = xn_ref.shape[0] // row_groups
    for c in range(PROJ_TN // PROJ_GROUP):
        for r in range(row_groups):
            acc = jnp.dot(
                xn_ref[r * rows : (r + 1) * rows, :],
                wb_ref[:, c * PROJ_GROUP : (c + 1) * PROJ_GROUP],
                preferred_element_type=F32,
            )
            yield r * rows, c * PROJ_GROUP, rows, acc


def _lane_blocks(acc):
    for h in range(PROJ_GROUP // DK_RET):
        yield h * DK_RET, acc[:, h * DK_RET : (h + 1) * DK_RET]


def _proj_slab_body(xn_ref, cos_ref, sin_ref, w_ref, p_ref, wb_ref, *, row_groups):
    j = pl.program_id(0)

    @pl.when(pl.program_id(1) == 0)
    def _cast():
        wb_ref[...] = w_ref[...].astype(BF16)

    groups = functools.partial(_proj_groups, xn_ref, wb_ref, row_groups)

    @pl.when(j < COL_V // PROJ_TN)
    def _rope():
        scale = jnp.where(j == COL_K // PROJ_TN, F32(DK_RET**-0.5), F32(1.0))
        for row0, col0, rows, acc in groups():
            cos = cos_ref[row0 : row0 + rows, :] * scale
            sin = sin_ref[row0 : row0 + rows, :] * scale
            for off, xh in _lane_blocks(acc):
                y = xh * cos + pltpu.roll(xh, DK_RET // 2, axis=1) * sin
                p_ref[row0 : row0 + rows, col0 + off : col0 + off + DK_RET] = y.astype(p_ref.dtype)

    @pl.when((j >= COL_V // PROJ_TN) & (j < COL_GR // PROJ_TN))
    def _plain():
        scale = jnp.where(j == COL_SQ // PROJ_TN, F32(SB_Q_SCALE), F32(1.0))
        for row0, col0, rows, acc in groups():
            p_ref[row0 : row0 + rows, col0 : col0 + PROJ_GROUP] = (acc * scale).astype(p_ref.dtype)

    @pl.when(j >= COL_GR // PROJ_TN)
    def _gate():
        for row0, col0, rows, acc in groups():
            y = 0.5 * jnp.tanh(0.5 * acc) + 0.5
            p_ref[row0 : row0 + rows, col0 : col0 + PROJ_GROUP] = y.astype(p_ref.dtype)


def _w_tile(j):
    first_kv = COL_SQ // PROJ_TN + 1
    return jnp.where(j < first_kv, j, j + 2 * W_SB // PROJ_TN)


def _project_slab(xn, cos, sin, w_in, *, tm, rope_period_tiles, p_dtype):
    m = xn.shape[0]
    return pl.pallas_call(
        functools.partial(_proj_slab_body, row_groups=tm // PROJ_GROUP_ROWS),
        grid=(N_P_TILES, m // tm),
        in_specs=[
            pl.BlockSpec((tm, D_MODEL), lambda j, i: (i, 0)),
            pl.BlockSpec((tm, DK_RET), lambda j, i: (i % rope_period_tiles, 0)),
            pl.BlockSpec((tm, DK_RET), lambda j, i: (i % rope_period_tiles, 0)),
            pl.BlockSpec((D_MODEL, PROJ_TN), lambda j, i: (0, _w_tile(j))),
        ],
        out_specs=pl.BlockSpec((tm, PROJ_TN), lambda j, i: (i, j)),
        out_shape=jax.ShapeDtypeStruct((m, P_COLS), p_dtype),
        scratch_shapes=[pltpu.VMEM((D_MODEL, PROJ_TN), BF16)],
        compiler_params=pltpu.CompilerParams(
            dimension_semantics=("arbitrary", "arbitrary"), vmem_limit_bytes=VMEM_LIMIT
        ),
        name="proj",
    )(xn, cos, sin, w_in)


def _proj_heads_body(xn_ref, w_ref, tok_ref, head_ref, wb_ref, *, row_groups):
    @pl.when(pl.program_id(0) == 0)
    def _cast():
        wb_ref[...] = w_ref[...].astype(BF16)

    for row0, col0, rows, acc in _proj_groups(xn_ref, wb_ref, row_groups):
        for off, ah in _lane_blocks(acc):
            head = (col0 + off) // D_SB
            tok_ref[pl.ds(row0 * H_SB + head, rows, stride=H_SB), :] = ah
            head_ref[head, row0 : row0 + rows, :] = ah.astype(BF16)


def _proj_heads_norm_body(x_ref, gain_ref, w_ref, tok_ref, head_ref, xn_ref, wb_ref, *, row_groups):
    _rmsnorm_body(x_ref, gain_ref, xn_ref)
    _proj_heads_body(xn_ref, w_ref, tok_ref, head_ref, wb_ref, row_groups=row_groups)


def _project_heads_norm(x, gain, w_in, col, *, tm):
    m = x.shape[0]
    return pl.pallas_call(
        functools.partial(_proj_heads_norm_body, row_groups=tm // PROJ_GROUP_ROWS),
        grid=(m // tm,),
        in_specs=[
            pl.BlockSpec((tm, D_MODEL), lambda i: (i, 0)),
            pl.BlockSpec((1, D_MODEL), lambda i: (0, 0)),
            pl.BlockSpec((D_MODEL, W_SB), lambda i: (0, col // W_SB), pipeline_mode=pl.Buffered(1)),
        ],
        out_specs=[
            pl.BlockSpec((tm * H_SB, D_SB), lambda i: (i, 0)),
            pl.BlockSpec((H_SB, tm, D_SB), lambda i: (0, i, 0)),
            pl.BlockSpec((tm, D_MODEL), lambda i: (i, 0)),
        ],
        out_shape=[
            jax.ShapeDtypeStruct((m * H_SB, D_SB), F32),
            jax.ShapeDtypeStruct((H_SB, m, D_SB), BF16),
            jax.ShapeDtypeStruct((m, D_MODEL), BF16),
        ],
        scratch_shapes=[pltpu.VMEM((D_MODEL, W_SB), BF16)],
        compiler_params=pltpu.CompilerParams(dimension_semantics=("arbitrary",), vmem_limit_bytes=VMEM_LIMIT),
        name="proj_heads_norm",
    )(x, gain, w_in)


def _project_heads(xn, w_in, col, *, tm):
    m = xn.shape[0]
    return pl.pallas_call(
        functools.partial(_proj_heads_body, row_groups=tm // PROJ_GROUP_ROWS),
        grid=(m // tm,),
        in_specs=[
            pl.BlockSpec((tm, D_MODEL), lambda i: (i, 0)),
            pl.BlockSpec((D_MODEL, W_SB), lambda i: (0, col // W_SB), pipeline_mode=pl.Buffered(1)),
        ],
        out_specs=[
            pl.BlockSpec((tm * H_SB, D_SB), lambda i: (i, 0)),
            pl.BlockSpec((H_SB, tm, D_SB), lambda i: (0, i, 0)),
        ],
        out_shape=[
            jax.ShapeDtypeStruct((m * H_SB, D_SB), F32),
            jax.ShapeDtypeStruct((H_SB, m, D_SB), BF16),
        ],
        scratch_shapes=[pltpu.VMEM((D_MODEL, W_SB), BF16)],
        compiler_params=pltpu.CompilerParams(dimension_semantics=("arbitrary",), vmem_limit_bytes=VMEM_LIMIT),
        name="proj_heads",
    )(xn, w_in)


def _group_norm_gate(o, gain, g):
    mu = jnp.mean(o, axis=-1, keepdims=True)
    d = o - mu
    var = jnp.mean(d * d, axis=-1, keepdims=True)
    return d * lax.rsqrt(var + EPS_GN) * gain * _silu(g)


def _ret_prompt_body(q_ref, k_ref, v_ref, g_ref, gain_ref, decay_ref, cd_ref, kd_ref, gs_ref, out_ref, s_ref):
    c = pl.program_id(1)

    @pl.when(c == 0)
    def _init():
        s_ref[...] = jnp.zeros_like(s_ref)

    heads = range(H_RET)
    q = [q_ref[:, h * DK_RET : (h + 1) * DK_RET] for h in heads]
    k = [k_ref[:, h * DK_RET : (h + 1) * DK_RET] for h in heads]
    v = [v_ref[:, h * DV_RET : (h + 1) * DV_RET] for h in heads]
    s = [s_ref[0, h] for h in heads]
    inner = [lax.dot_general(q[h], k[h], _NT, preferred_element_type=F32) for h in heads]
    cross = [jnp.dot(q[h], s[h].astype(BF16), preferred_element_type=F32) for h in heads]
    for h in heads:
        kdec_t = (k[h].astype(F32) * kd_ref[:, h : h + 1]).T.astype(BF16)
        s_ref[0, h] = gs_ref[h] * s[h] + jnp.dot(kdec_t, v[h], preferred_element_type=F32)
    for h in heads:
        o = jnp.dot((inner[h] * decay_ref[h]).astype(BF16), v[h], preferred_element_type=F32)
        o = o + cross[h] * cd_ref[:, h : h + 1]
        g = g_ref[:, h * DV_RET : (h + 1) * DV_RET].astype(F32)
        gain = gain_ref[:, h * DV_RET : (h + 1) * DV_RET]
        out_ref[:, h * DV_RET : (h + 1) * DV_RET] = _group_norm_gate(o, gain, g).astype(out_ref.dtype)


def _ret_tables(chunk):
    lg = np.log1p(-np.exp2(-5.0 - np.arange(H_RET, dtype=np.float64)))
    idx = np.arange(chunk, dtype=np.float64)
    diff = idx[:, None] - idx[None, :]
    decay = np.where(diff >= 0, np.exp(lg[:, None, None] * np.maximum(diff, 0.0)[None]), 0.0)
    cross = np.exp(lg[:, None] * (idx + 1.0)[None, :]).T
    kdec = np.exp(lg[:, None] * (chunk - 1.0 - idx)[None, :]).T
    sdec = np.broadcast_to(np.exp(lg * chunk)[:, None, None], (H_RET, 1, DV_RET))
    return tuple(np.asarray(a, np.float32) for a in (decay, cross, kdec, sdec))


def _ret_prompt(p, gn_gain, batch, seq):
    nc = seq // RET_CHUNK
    decay, cross, kdec, sdec = _ret_tables(RET_CHUNK)
    row = lambda b, c: b * nc + c
    const2 = lambda b, c: (0, 0)
    const3 = lambda b, c: (0, 0, 0)
    return pl.pallas_call(
        _ret_prompt_body,
        grid=(batch, nc),
        in_specs=[
            pl.BlockSpec((RET_CHUNK, W_QK), lambda b, c: (row(b, c), COL_Q // W_QK)),
            pl.BlockSpec((RET_CHUNK, W_QK), lambda b, c: (row(b, c), COL_K // W_QK)),
            pl.BlockSpec((RET_CHUNK, W_V), lambda b, c: (row(b, c), COL_V // W_V)),
            pl.BlockSpec((RET_CHUNK, W_V), lambda b, c: (row(b, c), COL_G // W_V)),
            pl.BlockSpec((1, W_V), const2),
            pl.BlockSpec((H_RET, RET_CHUNK, RET_CHUNK), const3),
            pl.BlockSpec((RET_CHUNK, H_RET), const2),
            pl.BlockSpec((RET_CHUNK, H_RET), const2),
            pl.BlockSpec((H_RET, 1, DV_RET), const3),
        ],
        out_specs=[
            pl.BlockSpec((RET_CHUNK, W_V), lambda b, c: (row(b, c), 0)),
            pl.BlockSpec((1, H_RET, DK_RET, DV_RET), lambda b, c: (b, 0, 0, 0)),
        ],
        out_shape=[
            jax.ShapeDtypeStruct((batch * seq, W_V), BF16),
            jax.ShapeDtypeStruct((batch, H_RET, DK_RET, DV_RET), F32),
        ],
        compiler_params=pltpu.CompilerParams(
            dimension_semantics=("arbitrary", "arbitrary"), vmem_limit_bytes=VMEM_LIMIT
        ),
        name="ret_prompt",
    )(p, p, p, p, gn_gain, decay, cross, kdec, sdec)


RS_SEQS = 32


def _ret_sample_body(q_ref, k_ref, v_ref, g_ref, s_ref, gain_ref, decay_ref, cd_ref, kd_ref, gs_ref, out_ref, snew_ref):
    n_seq = s_ref.shape[0]
    rows = q_ref.shape[0]
    t_len = rows // n_seq
    q = q_ref[...]
    k = k_ref[...]
    v = v_ref[...].astype(BF16)
    s = s_ref[...]
    inner = lax.dot_general(q.astype(BF16), k.astype(BF16), _NT, preferred_element_type=F32) * decay_ref[...]
    o = jnp.dot(inner.astype(BF16), v, preferred_element_type=F32)
    row_seq = lax.broadcasted_iota(jnp.int32, (rows, DK_RET), 0) // t_len
    q_exp = jnp.concatenate([jnp.where(row_seq == b, q, 0.0) for b in range(n_seq)], axis=1).astype(BF16)
    s_rows = s.reshape(n_seq * DK_RET, DV_RET).astype(BF16)
    o = o + jnp.dot(q_exp, s_rows, preferred_element_type=F32) * cd_ref[...]
    kdec_t = (k * kd_ref[...]).T
    col_seq = lax.broadcasted_iota(jnp.int32, (DK_RET, rows), 1) // t_len
    k_exp_t = jnp.concatenate([jnp.where(col_seq == b, kdec_t, 0.0) for b in range(n_seq)], axis=0).astype(BF16)
    upd = jnp.dot(k_exp_t, v, preferred_element_type=F32)
    snew_ref[...] = gs_ref[...] * s + upd.reshape(n_seq, DK_RET, DV_RET)
    out_ref[...] = _group_norm_gate(o, gain_ref[...], g_ref[...])


def _ret_sample(p, state, gn_gain, batch, t_len):
    decay, cross, kdec, sdec = _ret_tables(t_len)
    rows = RS_SEQS * t_len
    decay_bd = np.stack([np.kron(np.eye(RS_SEQS, dtype=np.float32), decay[h]) for h in range(H_RET)])
    cross_rows = np.tile(cross.T, (1, RS_SEQS))[:, :, None]
    kdec_rows = np.tile(kdec.T, (1, RS_SEQS))[:, :, None]
    qk_blk = lambda col: pl.BlockSpec((rows, DK_RET), lambda i, h: (i, col // DK_RET + h))
    v_blk = lambda col: pl.BlockSpec((rows, DV_RET), lambda i, h: (i, col // DV_RET + h))
    state_blk = pl.BlockSpec((RS_SEQS, None, DK_RET, DV_RET), lambda i, h: (i, h, 0, 0))
    return pl.pallas_call(
        _ret_sample_body,
        grid=(batch // RS_SEQS, H_RET),
        in_specs=[
            qk_blk(COL_Q),
            qk_blk(COL_K),
            v_blk(COL_V),
            v_blk(COL_G),
            state_blk,
            pl.BlockSpec((1, DV_RET), lambda i, h: (0, h)),
            pl.BlockSpec((None, rows, rows), lambda i, h: (h, 0, 0)),
            pl.BlockSpec((None, rows, 1), lambda i, h: (h, 0, 0)),
            pl.BlockSpec((None, rows, 1), lambda i, h: (h, 0, 0)),
            pl.BlockSpec((None, 1, DV_RET), lambda i, h: (h, 0, 0)),
        ],
        out_specs=[pl.BlockSpec((rows, DV_RET), lambda i, h: (i, h)), state_blk],
        out_shape=[
            jax.ShapeDtypeStruct((batch * t_len, W_V), F32),
            jax.ShapeDtypeStruct((batch, H_RET, DK_RET, DV_RET), F32),
        ],
        compiler_params=pltpu.CompilerParams(
            dimension_semantics=("arbitrary", "arbitrary"), vmem_limit_bytes=VMEM_LIMIT
        ),
        name="ret_sample",
    )(p, p, p, p, state, gn_gain, decay_bd, cross_rows, kdec_rows, sdec)


def _sb_weights(qk2, tri, bias2, carry, mask):
    drop, log2_beta = _sb_drop(qk2, bias2, mask)
    return _sb_finish(drop, log2_beta, _sb_after(drop, tri), carry, mask)


def _sb_drop(qk2, bias2, mask):
    z2 = qk2 + bias2
    drop = jnp.maximum(z2, 0.0) + jnp.log2(1.0 + jnp.exp2(-jnp.abs(z2)))
    log2_beta = z2 - drop
    if mask is not None:
        drop = jnp.where(mask, drop, 0.0)
    return drop, log2_beta


def _sb_after(drop, tri):
    return jnp.dot(drop.astype(BF16), tri, preferred_element_type=F32)


def _sb_finish(drop, log2_beta, after, carry, mask):
    a = jnp.exp2(log2_beta - after - carry)
    if mask is not None:
        a = jnp.where(mask, a, 0.0)
    return a, carry + after[:, 0:1] + drop[:, 0:1]


def _tri_np(n):
    return np.tril(np.ones((n, n), np.float32), -1)


SBP_T = 256
SBP_HEADS = 8


def _sb_prompt_body(bias_ref, q_ref, k_ref, v_ref, sg_ref, tri_ref, out_ref):
    hg = pl.program_id(1)
    i = pl.program_id(2)
    heads = range(SBP_HEADS)
    bias2 = [bias_ref[hg * SBP_HEADS + g] for g in heads]
    tri = tri_ref[...]
    q = [q_ref[:, g * D_SB : (g + 1) * D_SB] for g in heads]
    r = lax.broadcasted_iota(jnp.int32, (SBP_T, SBP_T), 0)
    c = lax.broadcasted_iota(jnp.int32, (SBP_T, SBP_T), 1)

    def tile(kb, state, mask):
        off = pl.multiple_of(kb * SBP_T, SBP_T)
        qk2 = [lax.dot_general(q[g], k_ref[g, pl.ds(off, SBP_T), :], _NT, preferred_element_type=F32) for g in heads]
        dl = [_sb_drop(qk2[g], bias2[g], mask) for g in heads]
        after = [_sb_after(dl[g][0], tri) for g in heads]
        new_state = []
        for g in heads:
            carry, acc = state[g]
            a, carry = _sb_finish(dl[g][0], dl[g][1], after[g], carry, mask)
            acc = acc + jnp.dot(a.astype(BF16), v_ref[g, pl.ds(off, SBP_T), :], preferred_element_type=F32)
            new_state.append((carry, acc))
        return tuple(new_state)

    state = tuple((jnp.zeros((SBP_T, 1), F32), jnp.zeros((SBP_T, D_SB), F32)) for _ in heads)
    state = tile(i, state, c < r)
    state = lax.fori_loop(0, i, lambda n, st: tile(i - 1 - n, st, None), state)
    acc = jnp.concatenate([st[1] for st in state], axis=1)
    out_ref[...] = (acc * _silu(sg_ref[...].astype(F32))).astype(out_ref.dtype)


def _sb_prompt(p, sk_heads, sv_heads, sb_bias, batch, seq):
    nq = seq // SBP_T
    width = SBP_HEADS * D_SB
    return pl.pallas_call(
        _sb_prompt_body,
        grid=(batch, H_SB // SBP_HEADS, nq),
        in_specs=[
            pl.BlockSpec(memory_space=pltpu.SMEM),
            pl.BlockSpec((SBP_T, width), lambda b, h, i: (b * nq + i, COL_SQ // width + h)),
            pl.BlockSpec((SBP_HEADS, seq, D_SB), lambda b, h, i: (h, b, 0)),
            pl.BlockSpec((SBP_HEADS, seq, D_SB), lambda b, h, i: (h, b, 0)),
            pl.BlockSpec((SBP_T, width), lambda b, h, i: (b * nq + i, COL_SG // width + h)),
            pl.BlockSpec((SBP_T, SBP_T), lambda b, h, i: (0, 0)),
        ],
        out_specs=pl.BlockSpec((SBP_T, width), lambda b, h, i: (b * nq + i, h)),
        out_shape=jax.ShapeDtypeStruct((batch * seq, W_SB), BF16),
        compiler_params=pltpu.CompilerParams(
            dimension_semantics=("arbitrary", "arbitrary", "arbitrary"), vmem_limit_bytes=VMEM_LIMIT
        ),
        name="sb_prompt",
    )(sb_bias * LOG2E, p, sk_heads, sv_heads, p, jnp.asarray(_tri_np(SBP_T), BF16))


SBS_CHUNK_PAGES = 4


SBS_SLOTS = 4


def _sb_sample_body(
    pt_ref, bias_ref, tri_ref, q_ref, kn_ref, vn_ref, sg_ref, kpool_ref, vpool_ref,
    out_ref, kbuf_ref, vbuf_ref, sem_ref, kd0_ref, kd1_ref, vd0_ref, vd1_ref, knd_ref, vnd_ref, kstage_ref, vstage_ref,
    *, chunks_per_seq,
):
    b = pl.program_id(0)
    n_seq = pl.num_programs(0)
    t_len = q_ref.shape[0]
    rows = H_SB * t_len
    half = H_SB // 2
    ppage = kbuf_ref.shape[2] // half
    lookahead = SBS_SLOTS - 1
    kd_refs, vd_refs = (kd0_ref, kd1_ref), (vd0_ref, vd1_ref)

    def chunk_copies(n, slot):
        copies = []
        for pg in range(SBS_CHUNK_PAGES):
            page_id = pt_ref[n * SBS_CHUNK_PAGES + pg]
            copies.append(pltpu.make_async_copy(kpool_ref.at[page_id], kbuf_ref.at[slot, pg], sem_ref.at[0, slot]))
            copies.append(pltpu.make_async_copy(vpool_ref.at[page_id], vbuf_ref.at[slot, pg], sem_ref.at[1, slot]))
        return copies

    def densify(src_ref, dst_ref, row0, n_pseudo):
        for j in range(half):
            dst_ref[row0 : row0 + n_pseudo, j * D_SB : (j + 1) * D_SB] = src_ref[
                pl.ds(j, n_pseudo, stride=half), :
            ].astype(dst_ref.dtype)

    @pl.when(b == 0)
    def _prime():
        for n in range(lookahead):
            for cp in chunk_copies(n, n % SBS_SLOTS):
                cp.start()

    q = q_ref[...]
    zeros = jnp.zeros((t_len, D_SB), F32)
    qbd = jnp.concatenate(
        [
            jnp.concatenate([q[:, h * D_SB : (h + 1) * D_SB] if j == h % half else zeros for j in range(half)], axis=1)
            for h in range(H_SB)
        ],
        axis=0,
    ).astype(BF16)
    bias2 = bias_ref[...]
    tri = tri_ref[...]

    def parity_ok(n_pseudo):
        r = lax.broadcasted_iota(jnp.int32, (rows, n_pseudo), 0)
        p = lax.broadcasted_iota(jnp.int32, (rows, n_pseudo), 1)
        return (p % 2) == (r // (half * t_len)), r, p

    def sweep(k, v, n_pages, carry, acc, mask):
        qk2 = lax.dot_general(qbd, k, _NT, preferred_element_type=F32)
        drop, log2_beta = _sb_drop(qk2, bias2, mask)
        pages = [slice(pg * ppage, (pg + 1) * ppage) for pg in range(n_pages)]
        stacked = jnp.concatenate([drop[:, cols] for cols in pages], axis=0)
        after_stacked = _sb_after(stacked, tri)
        afters = [after_stacked[pg * rows : (pg + 1) * rows] for pg in range(n_pages)]
        shifted = [None] * n_pages
        for pg in reversed(range(n_pages)):
            shifted[pg] = afters[pg] + carry
            carry = carry + afters[pg][:, 0:1] + drop[:, pg * ppage : pg * ppage + 1]
        a = jnp.where(mask, jnp.exp2(log2_beta - jnp.concatenate(shifted, axis=1)), 0.0)
        return carry, acc + jnp.dot(a.astype(BF16), v, preferred_element_type=F32)

    for src_ref, stage_ref, dst_ref in ((kn_ref, kstage_ref, knd_ref), (vn_ref, vstage_ref, vnd_ref)):
        stage_ref[...] = jnp.zeros_like(stage_ref)
        densify(src_ref, stage_ref, 0, 2 * t_len)
        dst_ref[...] = stage_ref[...].astype(BF16)
    ok, r, p = parity_ok(ppage)
    carry, acc = sweep(
        knd_ref[...], vnd_ref[...], 1,
        jnp.zeros((rows, 1), F32), jnp.zeros((rows, half * D_SB), F32), ok & ((p // 2) < (r % t_len)),
    )

    chunk_ok, _, _ = parity_ok(SBS_CHUNK_PAGES * ppage)
    for c in range(chunks_per_seq):
        n = b * chunks_per_seq + c
        slot = c % SBS_SLOTS
        for cp in chunk_copies(n, slot):
            cp.wait()
        for cp in chunk_copies(n + lookahead, (c + lookahead) % SBS_SLOTS):
            cp.start()
        kd_ref, vd_ref = kd_refs[c % 2], vd_refs[c % 2]
        for pg in range(SBS_CHUNK_PAGES):
            densify(kbuf_ref.at[slot, pg], kd_ref, pg * ppage, ppage)
            densify(vbuf_ref.at[slot, pg], vd_ref, pg * ppage, ppage)
        carry, acc = sweep(kd_ref[...], vd_ref[...], SBS_CHUNK_PAGES, carry, acc, chunk_ok)

    o = jnp.concatenate(
        [acc[h * t_len : (h + 1) * t_len, (h % half) * D_SB : (h % half + 1) * D_SB] for h in range(H_SB)], axis=1
    )
    out_ref[...] = o * _silu(sg_ref[...])

    @pl.when(b == n_seq - 1)
    def _drain():
        for i in range(lookahead):
            for cp in chunk_copies(n_seq * chunks_per_seq + i, i % SBS_SLOTS):
                cp.wait()


def _sb_sample(page_table, p, sk, sv, pool_k, pool_v, sb_bias, batch, t_len):
    n_pages = page_table.shape[1]
    chunks_per_seq = n_pages // SBS_CHUNK_PAGES
    assert n_pages % SBS_CHUNK_PAGES == 0 and chunks_per_seq % SBS_SLOTS == 0
    page_rows = pool_k.shape[1]
    rows = H_SB * t_len
    half = H_SB // 2
    ppage = page_rows // half
    width = half * D_SB
    bias_rows = jnp.repeat(sb_bias * LOG2E, t_len)[:, None]
    tri = jnp.asarray(_tri_np(ppage), BF16)
    fetch = page_table.reshape(batch, chunks_per_seq, SBS_CHUNK_PAGES)[:, ::-1, :].reshape(-1)
    fetch = jnp.concatenate([fetch, fetch[: (SBS_SLOTS - 1) * SBS_CHUNK_PAGES]])

    const2 = lambda b, pt: (0, 0)
    grid_spec = pltpu.PrefetchScalarGridSpec(
        num_scalar_prefetch=1,
        grid=(batch,),
        in_specs=[
            pl.BlockSpec((rows, 1), const2),
            pl.BlockSpec((ppage, ppage), const2),
            pl.BlockSpec((t_len, W_SB), lambda b, pt: (b, COL_SQ // W_SB)),
            pl.BlockSpec((rows, D_SB), lambda b, pt: (b, 0)),
            pl.BlockSpec((rows, D_SB), lambda b, pt: (b, 0)),
            pl.BlockSpec((t_len, W_SB), lambda b, pt: (b, COL_SG // W_SB)),
            pl.BlockSpec(memory_space=pl.ANY),
            pl.BlockSpec(memory_space=pl.ANY),
        ],
        out_specs=pl.BlockSpec((t_len, W_SB), lambda b, pt: (b, 0)),
        scratch_shapes=[
            pltpu.VMEM((SBS_SLOTS, SBS_CHUNK_PAGES, page_rows, D_SB), F32),
            pltpu.VMEM((SBS_SLOTS, SBS_CHUNK_PAGES, page_rows, D_SB), F32),
            pltpu.SemaphoreType.DMA((2, SBS_SLOTS)),
            pltpu.VMEM((SBS_CHUNK_PAGES * ppage, width), BF16),
            pltpu.VMEM((SBS_CHUNK_PAGES * ppage, width), BF16),
            pltpu.VMEM((SBS_CHUNK_PAGES * ppage, width), BF16),
            pltpu.VMEM((SBS_CHUNK_PAGES * ppage, width), BF16),
            pltpu.VMEM((ppage, width), BF16),
            pltpu.VMEM((ppage, width), BF16),
            pltpu.VMEM((ppage, width), F32),
            pltpu.VMEM((ppage, width), F32),
        ],
    )
    return pl.pallas_call(
        functools.partial(_sb_sample_body, chunks_per_seq=chunks_per_seq),
        grid_spec=grid_spec,
        out_shape=jax.ShapeDtypeStruct((batch * t_len, W_SB), F32),
        compiler_params=pltpu.CompilerParams(dimension_semantics=("arbitrary",), vmem_limit_bytes=VMEM_LIMIT),
        name="sb_sample",
    )(fetch, bias_rows, tri, p, sk, sv, p, pool_k, pool_v)


def _merge_body(x_ref, rb_ref, sb_ref, gr_ref, gs_ref, wr_ref, ws_ref, wo_ref, fg_ref, y_ref):
    m = gr_ref[...].astype(F32) * jnp.dot(rb_ref[...].astype(BF16), wr_ref[...], preferred_element_type=F32)
    m = m + gs_ref[...].astype(F32) * jnp.dot(sb_ref[...].astype(BF16), ws_ref[...], preferred_element_type=F32)
    hid = x_ref[...] + jnp.dot(m.astype(BF16), wo_ref[...], preferred_element_type=F32)
    r = lax.rsqrt(jnp.mean(hid * hid, axis=-1, keepdims=True) + EPS_RMS)
    y_ref[...] = hid * r * fg_ref[...]


def _merge(x, ret_branch, sb_branch, p, w_ret, w_sb, w_out, final_gain, *, tm):
    m = x.shape[0]
    resident = functools.partial(pl.BlockSpec, index_map=lambda i: (0, 0), pipeline_mode=pl.Buffered(1))
    return pl.pallas_call(
        _merge_body,
        grid=(m // tm,),
        in_specs=[
            pl.BlockSpec((tm, D_MODEL), lambda i: (i, 0)),
            pl.BlockSpec((tm, W_V), lambda i: (i, 0)),
            pl.BlockSpec((tm, W_SB), lambda i: (i, 0)),
            pl.BlockSpec((tm, D_MODEL), lambda i: (i, COL_GR // D_MODEL)),
            pl.BlockSpec((tm, D_MODEL), lambda i: (i, COL_GS // D_MODEL)),
            resident((W_V, D_MODEL)),
            resident((W_SB, D_MODEL)),
            resident((D_MODEL, D_MODEL)),
            pl.BlockSpec((1, D_MODEL), lambda i: (0, 0)),
        ],
        out_specs=pl.BlockSpec((tm, D_MODEL), lambda i: (i, 0)),
        out_shape=jax.ShapeDtypeStruct((m, D_MODEL), F32),
        compiler_params=pltpu.CompilerParams(dimension_semantics=("arbitrary",), vmem_limit_bytes=VMEM_LIMIT),
        name="merge",
    )(x, ret_branch, sb_branch, p, p, w_ret, w_sb, w_out, final_gain)


def _rope_tables(pos):
    half = DK_RET // 2
    inv = ROPE_BASE ** (-np.arange(half, dtype=np.float64) / half)
    ang = np.asarray(pos, np.float64)[:, None] * inv[None, :]
    cos, sin = np.cos(ang), np.sin(ang)
    cos_full = np.concatenate([cos, cos], axis=-1)
    sin_signed = np.concatenate([-sin, sin], axis=-1)
    return cos_full.astype(np.float32), sin_signed.astype(np.float32)


def _layer(xp, xs, state, pool_k, pool_v, page_table, norm_gain, w_in, sb_bias, gn_gain, w_ret, w_sb, w_out, final_gain):
    batch, seq, _ = xp.shape
    dec_batch, t_len, _ = xs.shape
    n_pool, page = pool_k.shape[:2]
    past_len = page_table.shape[1] * page

    w_ret_b, w_sb_b, w_out_b = w_ret.astype(BF16), w_sb.astype(BF16), w_out.astype(BF16)
    gain2 = norm_gain[None, :]
    gn_gain2 = gn_gain[None, :]
    final_gain2 = final_gain[None, :]

    x2p = xp.reshape(batch * seq, D_MODEL)
    x2s = xs.reshape(dec_batch * t_len, D_MODEL)
    tm_slab, tm_heads = 1024, 512
    cos_p, sin_p = _rope_tables(np.arange(seq))
    cos_s, sin_s = _rope_tables(past_len + np.arange(t_len))
    cos_s, sin_s = np.tile(cos_s, (tm_slab // t_len, 1)), np.tile(sin_s, (tm_slab // t_len, 1))

    skp, skp_heads, xnp = _project_heads_norm(x2p, gain2, w_in, W_IN_COL_SK, tm=tm_heads)
    sks, _, xns = _project_heads_norm(x2s, gain2, w_in, W_IN_COL_SK, tm=tm_heads)
    pp = _project_slab(xnp, cos_p, sin_p, w_in, tm=tm_slab, rope_period_tiles=seq // tm_slab, p_dtype=BF16)
    ps = _project_slab(xns, cos_s, sin_s, w_in, tm=tm_slab, rope_period_tiles=1, p_dtype=F32)
    svp, svp_heads = _project_heads(xnp, w_in, W_IN_COL_SV, tm=tm_slab)
    svs, _ = _project_heads(xns, w_in, W_IN_COL_SV, tm=tm_slab)

    rbp, state_p = _ret_prompt(pp, gn_gain2, batch, seq)
    rbs, state_s = _ret_sample(ps, state, gn_gain2, dec_batch, t_len)

    sbp = _sb_prompt(pp, skp_heads, svp_heads, sb_bias, batch, seq)
    sbs = _sb_sample(
        page_table, ps, sks, svs, pool_k.reshape(n_pool, page * H_SB, D_SB), pool_v.reshape(n_pool, page * H_SB, D_SB),
        sb_bias, dec_batch, t_len,
    )

    yp = _merge(x2p, rbp, sbp, pp, w_ret_b, w_sb_b, w_out_b, final_gain2, tm=256)
    ys = _merge(x2s, rbs, sbs, ps, w_ret_b, w_sb_b, w_out_b, final_gain2, tm=256)

    return (
        yp.reshape(batch, seq, D_MODEL),
        ys.reshape(dec_batch, t_len, D_MODEL),
        state_p,
        skp.reshape(batch, seq, H_SB, D_SB),
        svp.reshape(batch, seq, H_SB, D_SB),
        state_s,
        sks.reshape(dec_batch, t_len, H_SB, D_SB),
        svs.reshape(dec_batch, t_len, H_SB, D_SB),
    )


def kernel(x_prompt, x_sample, state_ret, cache_sb_k, cache_sb_v, page_table, norm_gain, w_in, sb_bias, ret_gn_gain, w_ret_proj, w_sb_proj, w_out, final_norm_gain):
    depth = w_in.shape[0]
    assert depth == 1, "single-layer trunk"
    outs = _layer(
        x_prompt, x_sample, state_ret[0], cache_sb_k[0], cache_sb_v[0], page_table, norm_gain[0], w_in[0],
        sb_bias[0], ret_gn_gain[0], w_ret_proj[0], w_sb_proj[0], w_out[0], final_norm_gain,
    )
    yp, ys, state_p, skp, svp, state_s, sks, svs = outs
    return (yp, ys, state_p[None], skp[None], svp[None], state_s[None], sks[None], svs[None])
```

```python
import functools

import jax
import jax.numpy as jnp
import numpy as np
from jax import lax
from jax.experimental import pallas as pl
from jax.experimental.pallas import tpu as pltpu

F32 = jnp.float32
BF16 = jnp.bfloat16

D_MODEL = 2048
H_RET = 8
DK_RET = 128
DV_RET = 256
RET_CHUNK = 128
H_SB = 8
D_SB = 128
ROPE_BASE = 10000.0
EPS_RMS = 1e-6
EPS_GN = 1e-5
W_QK = H_RET * DK_RET
W_V = H_RET * DV_RET
W_SB = H_SB * D_SB

PROJ_TN = 1024
PROJ_GROUP = 512
PROJ_GROUP_ROWS = 256
COL_Q = 0
COL_K = COL_Q + W_QK
COL_V = COL_K + W_QK
COL_G = COL_V + W_V
COL_SQ = COL_G + W_V
COL_SG = COL_SQ + W_SB
COL_GR = COL_SG + W_SB
COL_GS = COL_GR + D_MODEL
P_COLS = COL_GS + D_MODEL
N_P_TILES = P_COLS // PROJ_TN
W_IN_COL_SK = COL_SQ + W_SB
W_IN_COL_SV = W_IN_COL_SK + W_SB

LOG2E = float(np.log2(np.e))
SB_Q_SCALE = D_SB**-0.5 * LOG2E

VMEM_LIMIT = 56 * 1024 * 1024

_NT = (((1,), (1,)), ((), ()))


def _silu(g):
    return g * jax.nn.sigmoid(g)


def _rmsnorm_body(x_ref, gain_ref, xn_ref):
    x = x_ref[...]
    r = lax.rsqrt(jnp.mean(x * x, axis=-1, keepdims=True) + EPS_RMS)
    xn_ref[...] = (x * r * gain_ref[...]).astype(xn_ref.dtype)


def _proj_groups(xn_ref, wb_ref, row_groups):
    rows = xn_ref.shape[0] // row_groups
    for c in range(PROJ_TN // PROJ_GROUP):
        for r in range(row_groups):
            acc = jnp.dot(
                xn_ref[r * rows : (r + 1) * rows, :],
                wb_ref[:, c * PROJ_GROUP : (c + 1) * PROJ_GROUP],
                preferred_element_type=F32,
            )
            yield r * rows, c * PROJ_GROUP, rows, acc


def _lane_blocks(acc):
    for h in range(PROJ_GROUP // DK_RET):
        yield h * DK_RET, acc[:, h * DK_RET : (h + 1) * DK_RET]


def _proj_slab_body(xn_ref, cos_ref, sin_ref, w_ref, p_ref, wb_ref, *, row_groups):
    j = pl.program_id(0)

    @pl.when(pl.program_id(1) == 0)
    def _cast():
        wb_ref[...] = w_ref[...].astype(BF16)

    groups = functools.partial(_proj_groups, xn_ref, wb_ref, row_groups)

    @pl.when(j < COL_V // PROJ_TN)
    def _rope():
        scale = jnp.where(j == COL_K // PROJ_TN, F32(DK_RET**-0.5), F32(1.0))
        for row0, col0, rows, acc in groups():
            cos = cos_ref[row0 : row0 + rows, :] * scale
            sin = sin_ref[row0 : row0 + rows, :] * scale
            for off, xh in _lane_blocks(acc):
                y = xh * cos + pltpu.roll(xh, DK_RET // 2, axis=1) * sin
                p_ref[row0 : row0 + rows, col0 + off : col0 + off + DK_RET] = y.astype(p_ref.dtype)

    @pl.when((j >= COL_V // PROJ_TN) & (j < COL_GR // PROJ_TN))
    def _plain():
        scale = jnp.where(j == COL_SQ // PROJ_TN, F32(SB_Q_SCALE), F32(1.0))
        for row0, col0, rows, acc in groups():
            p_ref[row0 : row0 + rows, col0 : col0 + PROJ_GROUP] = (acc * scale).astype(p_ref.dtype)

    @pl.when(j >= COL_GR // PROJ_TN)
    def _gate():
        for row0, col0, rows, acc in groups():
            y = 0.5 * jnp.tanh(0.5 * acc) + 0.5
            p_ref[row0 : row0 + rows, col0 : col0 + PROJ_GROUP] = y.astype(p_ref.dtype)


def _w_tile(j):
    first_kv = COL_SQ // PROJ_TN + 1
    return jnp.where(j < first_kv, j, j + 2 * W_SB // PROJ_TN)


def _project_slab(xn, cos, sin, w_in, *, tm, rope_period_tiles, p_dtype):
    m = xn.shape[0]
    return pl.pallas_call(
        functools.partial(_proj_slab_body, row_groups=tm // PROJ_GROUP_ROWS),
        grid=(N_P_TILES, m // tm),
        in_specs=[
            pl.BlockSpec((tm, D_MODEL), lambda j, i: (i, 0)),
            pl.BlockSpec((tm, DK_RET), lambda j, i: (i % rope_period_tiles, 0)),
            pl.BlockSpec((tm, DK_RET), lambda j, i: (i % rope_period_tiles, 0)),
            pl.BlockSpec((D_MODEL, PROJ_TN), lambda j, i: (0, _w_tile(j))),
        ],
        out_specs=pl.BlockSpec((tm, PROJ_TN), lambda j, i: (i, j)),
        out_shape=jax.ShapeDtypeStruct((m, P_COLS), p_dtype),
        scratch_shapes=[pltpu.VMEM((D_MODEL, PROJ_TN), BF16)],
        compiler_params=pltpu.CompilerParams(
            dimension_semantics=("arbitrary", "arbitrary"), vmem_limit_bytes=VMEM_LIMIT
        ),
        name="proj",
    )(xn, cos, sin, w_in)


def _proj_heads_body(xn_ref, w_ref, tok_ref, head_ref, wb_ref, *, row_groups):
    @pl.when(pl.program_id(0) == 0)
    def _cast():
        wb_ref[...] = w_ref[...].astype(BF16)

    for row0, col0, rows, acc in _proj_groups(xn_ref, wb_ref, row_groups):
        for off, ah in _lane_blocks(acc):
            head = (col0 + off) // D_SB
            tok_ref[pl.ds(row0 * H_SB + head, rows, stride=H_SB), :] = ah
            head_ref[head, row0 : row0 + rows, :] = ah.astype(BF16)


def _proj_heads_norm_body(x_ref, gain_ref, w_ref, tok_ref, head_ref, xn_ref, wb_ref, *, row_groups):
    _rmsnorm_body(x_ref, gain_ref, xn_ref)
    _proj_heads_body(xn_ref, w_ref, tok_ref, head_ref, wb_ref, row_groups=row_groups)


def _project_heads_norm(x, gain, w_in, col, *, tm):
    m = x.shape[0]
    return pl.pallas_call(
        functools.partial(_proj_heads_norm_body, row_groups=tm // PROJ_GROUP_ROWS),
        grid=(m // tm,),
        in_specs=[
            pl.BlockSpec((tm, D_MODEL), lambda i: (i, 0)),
            pl.BlockSpec((1, D_MODEL), lambda i: (0, 0)),
            pl.BlockSpec((D_MODEL, W_SB), lambda i: (0, col // W_SB), pipeline_mode=pl.Buffered(1)),
        ],
        out_specs=[
            pl.BlockSpec((tm * H_SB, D_SB), lambda i: (i, 0)),
            pl.BlockSpec((H_SB, tm, D_SB), lambda i: (0, i, 0)),
            pl.BlockSpec((tm, D_MODEL), lambda i: (i, 0)),
        ],
        out_shape=[
            jax.ShapeDtypeStruct((m * H_SB, D_SB), F32),
            jax.ShapeDtypeStruct((H_SB, m, D_SB), BF16),
            jax.ShapeDtypeStruct((m, D_MODEL), BF16),
        ],
        scratch_shapes=[pltpu.VMEM((D_MODEL, W_SB), BF16)],
        compiler_params=pltpu.CompilerParams(dimension_semantics=("arbitrary",), vmem_limit_bytes=VMEM_LIMIT),
        name="proj_heads_norm",
    )(x, gain, w_in)


def _project_heads(xn, w_in, col, *, tm):
    m = xn.shape[0]
    return pl.pallas_call(
        functools.partial(_proj_heads_body, row_groups=tm // PROJ_GROUP_ROWS),
        grid=(m // tm,),
        in_specs=[
            pl.BlockSpec((tm, D_MODEL), lambda i: (i, 0)),
            pl.BlockSpec((D_MODEL, W_SB), lambda i: (0, col // W_SB), pipeline_mode=pl.Buffered(1)),
        ],
        out_specs=[
            pl.BlockSpec((tm * H_SB, D_SB), lambda i: (i, 0)),
            pl.BlockSpec((H_SB, tm, D_SB), lambda i: (0, i, 0)),
        ],
        out_shape=[
            jax.ShapeDtypeStruct((m * H_SB, D_SB), F32),
            jax.ShapeDtypeStruct((H_SB, m, D_SB), BF16),
        ],
        scratch_shapes=[pltpu.VMEM((D_MODEL, W_SB), BF16)],
        compiler_params=pltpu.CompilerParams(dimension_semantics=("arbitrary",), vmem_limit_bytes=VMEM_LIMIT),
        name="proj_heads",
    )(xn, w_in)


def _group_norm_gate(o, gain, g):
    mu = jnp.mean(o, axis=-1, keepdims=True)
    d = o - mu
    var = jnp.mean(d * d, axis=-1, keepdims=True)
    return d * lax.rsqrt(var + EPS_GN) * gain * _silu(g)


def _ret_prompt_body(q_ref, k_ref, v_ref, g_ref, gain_ref, decay_ref, cd_ref, kd_ref, gs_ref, out_ref, s_ref):
    c = pl.program_id(1)

    @pl.when(c == 0)
    def _init():
        s_ref[...] = jnp.zeros_like(s_ref)

    heads = range(H_RET)
    q = [q_ref[:, h * DK_RET : (h + 1) * DK_RET] for h in heads]
    k = [k_ref[:, h * DK_RET : (h + 1) * DK_RET] for h in heads]
    v = [v_ref[:, h * DV_RET : (h + 1) * DV_RET] for h in heads]
    s = [s_ref[0, h] for h in heads]
    inner = [lax.dot_general(q[h], k[h], _NT, preferred_element_type=F32) for h in heads]
    cross = [jnp.dot(q[h], s[h].astype(BF16), preferred_element_type=F32) for h in heads]
    for h in heads:
        kdec_t = (k[h].astype(F32) * kd_ref[:, h : h + 1]).T.astype(BF16)
        s_ref[0, h] = gs_ref[h] * s[h] + jnp.dot(kdec_t, v[h], preferred_element_type=F32)
    for h in heads:
        o = jnp.dot((inner[h] * decay_ref[h]).astype(BF16), v[h], preferred_element_type=F32)
        o = o + cross[h] * cd_ref[:, h : h + 1]
        g = g_ref[:, h * DV_RET : (h + 1) * DV_RET].astype(F32)
        gain = gain_ref[:, h * DV_RET : (h + 1) * DV_RET]
        out_ref[:, h * DV_RET : (h + 1) * DV_RET] = _group_norm_gate(o, gain, g).astype(out_ref.dtype)


def _ret_tables(chunk):
    lg = np.log1p(-np.exp2(-5.0 - np.arange(H_RET, dtype=np.float64)))
    idx = np.arange(chunk, dtype=np.float64)
    diff = idx[:, None] - idx[None, :]
    decay = np.where(diff >= 0, np.exp(lg[:, None, None] * np.maximum(diff, 0.0)[None]), 0.0)
    cross = np.exp(lg[:, None] * (idx + 1.0)[None, :]).T
    kdec = np.exp(lg[:, None] * (chunk - 1.0 - idx)[None, :]).T
    sdec = np.broadcast_to(np.exp(lg * chunk)[:, None, None], (H_RET, 1, DV_RET))
    return tuple(np.asarray(a, np.float32) for a in (decay, cross, kdec, sdec))


def _ret_prompt(p, gn_gain, batch, seq):
    nc = seq // RET_CHUNK
    decay, cross, kdec, sdec = _ret_tables(RET_CHUNK)
    row = lambda b, c: b * nc + c
    const2 = lambda b, c: (0, 0)
    const3 = lambda b, c: (0, 0, 0)
    return pl.pallas_call(
        _ret_prompt_body,
        grid=(batch, nc),
        in_specs=[
            pl.BlockSpec((RET_CHUNK, W_QK), lambda b, c: (row(b, c), COL_Q // W_QK)),
            pl.BlockSpec((RET_CHUNK, W_QK), lambda b, c: (row(b, c), COL_K // W_QK)),
            pl.BlockSpec((RET_CHUNK, W_V), lambda b, c: (row(b, c), COL_V // W_V)),
            pl.BlockSpec((RET_CHUNK, W_V), lambda b, c: (row(b, c), COL_G // W_V)),
            pl.BlockSpec((1, W_V), const2),
            pl.BlockSpec((H_RET, RET_CHUNK, RET_CHUNK), const3),
            pl.BlockSpec((RET_CHUNK, H_RET), const2),
            pl.BlockSpec((RET_CHUNK, H_RET), const2),
            pl.BlockSpec((H_RET, 1, DV_RET), const3),
        ],
        out_specs=[
            pl.BlockSpec((RET_CHUNK, W_V), lambda b, c: (row(b, c), 0)),
            pl.BlockSpec((1, H_RET, DK_RET, DV_RET), lambda b, c: (b, 0, 0, 0)),
        ],
        out_shape=[
            jax.ShapeDtypeStruct((batch * seq, W_V), BF16),
            jax.ShapeDtypeStruct((batch, H_RET, DK_RET, DV_RET), F32),
        ],
        compiler_params=pltpu.CompilerParams(
            dimension_semantics=("arbitrary", "arbitrary"), vmem_limit_bytes=VMEM_LIMIT
        ),
        name="ret_prompt",
    )(p, p, p, p, gn_gain, decay, cross, kdec, sdec)


RS_SEQS = 32


def _ret_sample_body(q_ref, k_ref, v_ref, g_ref, s_ref, gain_ref, decay_ref, cd_ref, kd_ref, gs_ref, out_ref, snew_ref):
    n_seq = s_ref.shape[0]
    rows = q_ref.shape[0]
    t_len = rows // n_seq
    q = q_ref[...]
    k = k_ref[...]
    v = v_ref[...].astype(BF16)
    s = s_ref[...]
    inner = lax.dot_general(q.astype(BF16), k.astype(BF16), _NT, preferred_element_type=F32) * decay_ref[...]
    o = jnp.dot(inner.astype(BF16), v, preferred_element_type=F32)
    row_seq = lax.broadcasted_iota(jnp.int32, (rows, DK_RET), 0) // t_len
    q_exp = jnp.concatenate([jnp.where(row_seq == b, q, 0.0) for b in range(n_seq)], axis=1).astype(BF16)
    s_rows = s.reshape(n_seq * DK_RET, DV_RET).astype(BF16)
    o = o + jnp.dot(q_exp, s_rows, preferred_element_type=F32) * cd_ref[...]
    kdec_t = (k * kd_ref[...]).T
    col_seq = lax.broadcasted_iota(jnp.int32, (DK_RET, rows), 1) // t_len
    k_exp_t = jnp.concatenate([jnp.where(col_seq == b, kdec_t, 0.0) for b in range(n_seq)], axis=0).astype(BF16)
    upd = jnp.dot(k_exp_t, v, preferred_element_type=F32)
    snew_ref[...] = gs_ref[...] * s + upd.reshape(n_seq, DK_RET, DV_RET)
    out_ref[...] = _group_norm_gate(o, gain_ref[...], g_ref[...])


def _ret_sample(p, state, gn_gain, batch, t_len):
    decay, cross, kdec, sdec = _ret_tables(t_len)
    rows = RS_SEQS * t_len
    decay_bd = np.stack([np.kron(np.eye(RS_SEQS, dtype=np.float32), decay[h]) for h in range(H_RET)])
    cross_rows = np.tile(cross.T, (1, RS_SEQS))[:, :, None]
    kdec_rows = np.tile(kdec.T, (1, RS_SEQS))[:, :, None]
    qk_blk = lambda col: pl.BlockSpec((rows, DK_RET), lambda i, h: (i, col // DK_RET + h))
    v_blk = lambda col: pl.BlockSpec((rows, DV_RET), lambda i, h: (i, col // DV_RET + h))
    state_blk = pl.BlockSpec((RS_SEQS, None, DK_RET, DV_RET), lambda i, h: (i, h, 0, 0))
    return pl.pallas_call(
        _ret_sample_body,
        grid=(batch // RS_SEQS, H_RET),
        in_specs=[
            qk_blk(COL_Q),
            qk_blk(COL_K),
            v_blk(COL_V),
            v_blk(COL_G),
            state_blk,
            pl.BlockSpec((1, DV_RET), lambda i, h: (0, h)),
            pl.BlockSpec((None, rows, rows), lambda i, h: (h, 0, 0)),
            pl.BlockSpec((None, rows, 1), lambda i, h: (h, 0, 0)),
            pl.BlockSpec((None, rows, 1), lambda i, h: (h, 0, 0)),
            pl.BlockSpec((None, 1, DV_RET), lambda i, h: (h, 0, 0)),
        ],
        out_specs=[pl.BlockSpec((rows, DV_RET), lambda i, h: (i, h)), state_blk],
        out_shape=[
            jax.ShapeDtypeStruct((batch * t_len, W_V), F32),
            jax.ShapeDtypeStruct((batch, H_RET, DK_RET, DV_RET), F32),
        ],
        compiler_params=pltpu.CompilerParams(
            dimension_semantics=("arbitrary", "arbitrary"), vmem_limit_bytes=VMEM_LIMIT
        ),
        name="ret_sample",
    )(p, p, p, p, state, gn_gain, decay_bd, cross_rows, kdec_rows, sdec)


def _sb_weights(qk2, tri, bias2, carry, mask):
    drop, log2_beta = _sb_drop(qk2, bias2, mask)
    return _sb_finish(drop, log2_beta, _sb_after(drop, tri), carry, mask)


def _sb_drop(qk2, bias2, mask):
    z2 = qk2 + bias2
    drop = jnp.maximum(z2, 0.0) + jnp.log2(1.0 + jnp.exp2(-jnp.abs(z2)))
    log2_beta = z2 - drop
    if mask is not None:
        drop = jnp.where(mask, drop, 0.0)
    return drop, log2_beta


def _sb_after(drop, tri):
    return jnp.dot(drop.astype(BF16), tri, preferred_element_type=F32)


def _sb_finish(drop, log2_beta, after, carry, mask):
    a = jnp.exp2(log2_beta - after - carry)
    if mask is not None:
        a = jnp.where(mask, a, 0.0)
    return a, carry + after[:, 0:1] + drop[:, 0:1]


def _tri_np(n):
    return np.tril(np.ones((n, n), np.float32), -1)


SBP_T = 256
SBP_TQ = 256
SBP_HEADS = 8


def _sb_prompt_body(bias_ref, q_ref, k_ref, v_ref, sg_ref, tri_ref, out_ref):
    hg = pl.program_id(1)
    i = pl.program_id(2)
    heads = range(SBP_HEADS)
    bias2 = [bias_ref[hg * SBP_HEADS + g] for g in heads]
    tri = tri_ref[...]
    q = [q_ref[:, g * D_SB : (g + 1) * D_SB] for g in heads]
    r = lax.broadcasted_iota(jnp.int32, (SBP_TQ, SBP_T), 0)
    c = lax.broadcasted_iota(jnp.int32, (SBP_TQ, SBP_T), 1)
    tiles_per_q = SBP_TQ // SBP_T

    def tile(kb, state, mask):
        off = pl.multiple_of(kb * SBP_T, SBP_T)
        qk2 = [lax.dot_general(q[g], k_ref[g, pl.ds(off, SBP_T), :], _NT, preferred_element_type=F32) for g in heads]
        dl = [_sb_drop(qk2[g], bias2[g], mask) for g in heads]
        after = [_sb_after(dl[g][0], tri) for g in heads]
        new_state = []
        for g in heads:
            carry, acc = state[g]
            a, carry = _sb_finish(dl[g][0], dl[g][1], after[g], carry, mask)
            acc = acc + jnp.dot(a.astype(BF16), v_ref[g, pl.ds(off, SBP_T), :], preferred_element_type=F32)
            new_state.append((carry, acc))
        return tuple(new_state)

    state = tuple((jnp.zeros((SBP_TQ, 1), F32), jnp.zeros((SBP_TQ, D_SB), F32)) for _ in heads)
    first_full = i * tiles_per_q
    for d in reversed(range(tiles_per_q)):
        state = tile(first_full + d, state, c + d * SBP_T < r)
    state = lax.fori_loop(0, first_full, lambda n, st: tile(first_full - 1 - n, st, None), state)
    acc = jnp.concatenate([st[1] for st in state], axis=1)
    out_ref[...] = (acc * _silu(sg_ref[...].astype(F32))).astype(out_ref.dtype)


def _sb_prompt(p, sk_heads, sv_heads, sb_bias, batch, seq):
    nq = seq // SBP_TQ
    width = SBP_HEADS * D_SB
    return pl.pallas_call(
        _sb_prompt_body,
        grid=(batch, H_SB // SBP_HEADS, nq),
        in_specs=[
            pl.BlockSpec(memory_space=pltpu.SMEM),
            pl.BlockSpec((SBP_TQ, width), lambda b, h, i: (b * nq + i, COL_SQ // width + h)),
            pl.BlockSpec((SBP_HEADS, seq, D_SB), lambda b, h, i: (h, b, 0)),
            pl.BlockSpec((SBP_HEADS, seq, D_SB), lambda b, h, i: (h, b, 0)),
            pl.BlockSpec((SBP_TQ, width), lambda b, h, i: (b * nq + i, COL_SG // width + h)),
            pl.BlockSpec((SBP_T, SBP_T), lambda b, h, i: (0, 0)),
        ],
        out_specs=pl.BlockSpec((SBP_TQ, width), lambda b, h, i: (b * nq + i, h)),
        out_shape=jax.ShapeDtypeStruct((batch * seq, W_SB), BF16),
        compiler_params=pltpu.CompilerParams(
            dimension_semantics=("arbitrary", "arbitrary", "arbitrary"), vmem_limit_bytes=VMEM_LIMIT
        ),
        name="sb_prompt",
    )(sb_bias * LOG2E, p, sk_heads, sv_heads, p, jnp.asarray(_tri_np(SBP_T), BF16))


SBS_CHUNK_PAGES = 4


SBS_SLOTS = 4


def _sb_sample_body(
    pt_ref, bias_ref, tri_ref, q_ref, kn_ref, vn_ref, sg_ref, kpool_ref, vpool_ref,
    out_ref, kbuf_ref, vbuf_ref, sem_ref, kd0_ref, kd1_ref, vd0_ref, vd1_ref, knd_ref, vnd_ref, kstage_ref, vstage_ref,
    *, chunks_per_seq,
):
    b = pl.program_id(0)
    n_seq = pl.num_programs(0)
    t_len = q_ref.shape[0]
    rows = H_SB * t_len
    half = H_SB // 2
    ppage = kbuf_ref.shape[2] // half
    lookahead = SBS_SLOTS - 1
    kd_refs, vd_refs = (kd0_ref, kd1_ref), (vd0_ref, vd1_ref)

    def chunk_copies(n, slot):
        copies = []
        for pg in range(SBS_CHUNK_PAGES):
            page_id = pt_ref[n * SBS_CHUNK_PAGES + pg]
            copies.append(pltpu.make_async_copy(kpool_ref.at[page_id], kbuf_ref.at[slot, pg], sem_ref.at[0, slot]))
            copies.append(pltpu.make_async_copy(vpool_ref.at[page_id], vbuf_ref.at[slot, pg], sem_ref.at[1, slot]))
        return copies

    def densify(src_ref, dst_ref, row0, n_pseudo):
        for j in range(half):
            dst_ref[row0 : row0 + n_pseudo, j * D_SB : (j + 1) * D_SB] = src_ref[
                pl.ds(j, n_pseudo, stride=half), :
            ].astype(dst_ref.dtype)

    @pl.when(b == 0)
    def _prime():
        for n in range(lookahead):
            for cp in chunk_copies(n, n % SBS_SLOTS):
                cp.start()

    q = q_ref[...]
    zeros = jnp.zeros((t_len, D_SB), F32)
    qbd = jnp.concatenate(
        [
            jnp.concatenate([q[:, h * D_SB : (h + 1) * D_SB] if j == h % half else zeros for j in range(half)], axis=1)
            for h in range(H_SB)
        ],
        axis=0,
    ).astype(BF16)
    bias2 = bias_ref[...]
    tri = tri_ref[...]

    def parity_ok(n_pseudo):
        r = lax.broadcasted_iota(jnp.int32, (rows, n_pseudo), 0)
        p = lax.broadcasted_iota(jnp.int32, (rows, n_pseudo), 1)
        return (p % 2) == (r // (half * t_len)), r, p

    def sweep(k, v, n_pages, carry, acc, mask):
        qk2 = lax.dot_general(qbd, k, _NT, preferred_element_type=F32)
        drop, log2_beta = _sb_drop(qk2, bias2, mask)
        pages = [slice(pg * ppage, (pg + 1) * ppage) for pg in range(n_pages)]
        stacked = jnp.concatenate([drop[:, cols] for cols in pages], axis=0)
        after_stacked = _sb_after(stacked, tri)
        afters = [after_stacked[pg * rows : (pg + 1) * rows] for pg in range(n_pages)]
        shifted = [None] * n_pages
        for pg in reversed(range(n_pages)):
            shifted[pg] = afters[pg] + carry
            carry = carry + afters[pg][:, 0:1] + drop[:, pg * ppage : pg * ppage + 1]
        a = jnp.where(mask, jnp.exp2(log2_beta - jnp.concatenate(shifted, axis=1)), 0.0)
        return carry, acc + jnp.dot(a.astype(BF16), v, preferred_element_type=F32)

    for src_ref, stage_ref, dst_ref in ((kn_ref, kstage_ref, knd_ref), (vn_ref, vstage_ref, vnd_ref)):
        stage_ref[...] = jnp.zeros_like(stage_ref)
        densify(src_ref, stage_ref, 0, 2 * t_len)
        dst_ref[...] = stage_ref[...].astype(BF16)
    ok, r, p = parity_ok(ppage)
    carry, acc = sweep(
        knd_ref[...], vnd_ref[...], 1,
        jnp.zeros((rows, 1), F32), jnp.zeros((rows, half * D_SB), F32), ok & ((p // 2) < (r % t_len)),
    )

    chunk_ok, _, _ = parity_ok(SBS_CHUNK_PAGES * ppage)
    for c in range(chunks_per_seq):
        n = b * chunks_per_seq + c
        slot = c % SBS_SLOTS
        for cp in chunk_copies(n, slot):
            cp.wait()
        for cp in chunk_copies(n + lookahead, (c + lookahead) % SBS_SLOTS):
            cp.start()
        kd_ref, vd_ref = kd_refs[c % 2], vd_refs[c % 2]
        for pg in range(SBS_CHUNK_PAGES):
            densify(kbuf_ref.at[slot, pg], kd_ref, pg * ppage, ppage)
            densify(vbuf_ref.at[slot, pg], vd_ref, pg * ppage, ppage)
        carry, acc = sweep(kd_ref[...], vd_ref[...], SBS_CHUNK_PAGES, carry, acc, chunk_ok)

    o = jnp.concatenate(
        [acc[h * t_len : (h + 1) * t_len, (h % half) * D_SB : (h % half + 1) * D_SB] for h in range(H_SB)], axis=1
    )
    out_ref[...] = o * _silu(sg_ref[...])

    @pl.when(b == n_seq - 1)
    def _drain():
        for i in range(lookahead):
            for cp in chunk_copies(n_seq * chunks_per_seq + i, i % SBS_SLOTS):
                cp.wait()


def _sb_sample(page_table, p, sk, sv, pool_k, pool_v, sb_bias, batch, t_len):
    n_pages = page_table.shape[1]
    chunks_per_seq = n_pages // SBS_CHUNK_PAGES
    assert n_pages % SBS_CHUNK_PAGES == 0 and chunks_per_seq % SBS_SLOTS == 0
    page_rows = pool_k.shape[1]
    rows = H_SB * t_len
    half = H_SB // 2
    ppage = page_rows // half
    width = half * D_SB
    bias_rows = jnp.repeat(sb_bias * LOG2E, t_len)[:, None]
    tri = jnp.asarray(_tri_np(ppage), BF16)
    fetch = page_table.reshape(batch, chunks_per_seq, SBS_CHUNK_PAGES)[:, ::-1, :].reshape(-1)
    fetch = jnp.concatenate([fetch, fetch[: (SBS_SLOTS - 1) * SBS_CHUNK_PAGES]])

    const2 = lambda b, pt: (0, 0)
    grid_spec = pltpu.PrefetchScalarGridSpec(
        num_scalar_prefetch=1,
        grid=(batch,),
        in_specs=[
            pl.BlockSpec((rows, 1), const2),
            pl.BlockSpec((ppage, ppage), const2),
            pl.BlockSpec((t_len, W_SB), lambda b, pt: (b, COL_SQ // W_SB)),
            pl.BlockSpec((rows, D_SB), lambda b, pt: (b, 0)),
            pl.BlockSpec((rows, D_SB), lambda b, pt: (b, 0)),
            pl.BlockSpec((t_len, W_SB), lambda b, pt: (b, COL_SG // W_SB)),
            pl.BlockSpec(memory_space=pl.ANY),
            pl.BlockSpec(memory_space=pl.ANY),
        ],
        out_specs=pl.BlockSpec((t_len, W_SB), lambda b, pt: (b, 0)),
        scratch_shapes=[
            pltpu.VMEM((SBS_SLOTS, SBS_CHUNK_PAGES, page_rows, D_SB), F32),
            pltpu.VMEM((SBS_SLOTS, SBS_CHUNK_PAGES, page_rows, D_SB), F32),
            pltpu.SemaphoreType.DMA((2, SBS_SLOTS)),
            pltpu.VMEM((SBS_CHUNK_PAGES * ppage, width), BF16),
            pltpu.VMEM((SBS_CHUNK_PAGES * ppage, width), BF16),
            pltpu.VMEM((SBS_CHUNK_PAGES * ppage, width), BF16),
            pltpu.VMEM((SBS_CHUNK_PAGES * ppage, width), BF16),
            pltpu.VMEM((ppage, width), BF16),
            pltpu.VMEM((ppage, width), BF16),
            pltpu.VMEM((ppage, width), F32),
            pltpu.VMEM((ppage, width), F32),
        ],
    )
    return pl.pallas_call(
        functools.partial(_sb_sample_body, chunks_per_seq=chunks_per_seq),
        grid_spec=grid_spec,
        out_shape=jax.ShapeDtypeStruct((batch * t_len, W_SB), F32),
        compiler_params=pltpu.CompilerParams(dimension_semantics=("arbitrary",), vmem_limit_bytes=VMEM_LIMIT),
        name="sb_sample",
    )(fetch, bias_rows, tri, p, sk, sv, p, pool_k, pool_v)


def _merge_body(x_ref, rb_ref, sb_ref, gr_ref, gs_ref, wr_ref, ws_ref, wo_ref, fg_ref, y_ref):
    m = gr_ref[...].astype(F32) * jnp.dot(rb_ref[...].astype(BF16), wr_ref[...], preferred_element_type=F32)
    m = m + gs_ref[...].astype(F32) * jnp.dot(sb_ref[...].astype(BF16), ws_ref[...], preferred_element_type=F32)
    hid = x_ref[...] + jnp.dot(m.astype(BF16), wo_ref[...], preferred_element_type=F32)
    r = lax.rsqrt(jnp.mean(hid * hid, axis=-1, keepdims=True) + EPS_RMS)
    y_ref[...] = hid * r * fg_ref[...]


def _merge(x, ret_branch, sb_branch, p, w_ret, w_sb, w_out, final_gain, *, tm):
    m = x.shape[0]
    resident = functools.partial(pl.BlockSpec, index_map=lambda i: (0, 0), pipeline_mode=pl.Buffered(1))
    return pl.pallas_call(
        _merge_body,
        grid=(m // tm,),
        in_specs=[
            pl.BlockSpec((tm, D_MODEL), lambda i: (i, 0)),
            pl.BlockSpec((tm, W_V), lambda i: (i, 0)),
            pl.BlockSpec((tm, W_SB), lambda i: (i, 0)),
            pl.BlockSpec((tm, D_MODEL), lambda i: (i, COL_GR // D_MODEL)),
            pl.BlockSpec((tm, D_MODEL), lambda i: (i, COL_GS // D_MODEL)),
            resident((W_V, D_MODEL)),
            resident((W_SB, D_MODEL)),
            resident((D_MODEL, D_MODEL)),
            pl.BlockSpec((1, D_MODEL), lambda i: (0, 0)),
        ],
        out_specs=pl.BlockSpec((tm, D_MODEL), lambda i: (i, 0)),
        out_shape=jax.ShapeDtypeStruct((m, D_MODEL), F32),
        compiler_params=pltpu.CompilerParams(dimension_semantics=("arbitrary",), vmem_limit_bytes=VMEM_LIMIT),
        name="merge",
    )(x, ret_branch, sb_branch, p, p, w_ret, w_sb, w_out, final_gain)


def _rope_tables(pos):
    half = DK_RET // 2
    inv = ROPE_BASE ** (-np.arange(half, dtype=np.float64) / half)
    ang = np.asarray(pos, np.float64)[:, None] * inv[None, :]
    cos, sin = np.cos(ang), np.sin(ang)
    cos_full = np.concatenate([cos, cos], axis=-1)
    sin_signed = np.concatenate([-sin, sin], axis=-1)
    return cos_full.astype(np.float32), sin_signed.astype(np.float32)


def _layer(xp, xs, state, pool_k, pool_v, page_table, norm_gain, w_in, sb_bias, gn_gain, w_ret, w_sb, w_out, final_gain):
    batch, seq, _ = xp.shape
    dec_batch, t_len, _ = xs.shape
    n_pool, page = pool_k.shape[:2]
    past_len = page_table.shape[1] * page

    w_ret_b, w_sb_b, w_out_b = w_ret.astype(BF16), w_sb.astype(BF16), w_out.astype(BF16)
    gain2 = norm_gain[None, :]
    gn_gain2 = gn_gain[None, :]
    final_gain2 = final_gain[None, :]

    x2p = xp.reshape(batch * seq, D_MODEL)
    x2s = xs.reshape(dec_batch * t_len, D_MODEL)
    tm_slab, tm_heads = 1024, 512
    cos_p, sin_p = _rope_tables(np.arange(seq))
    cos_s, sin_s = _rope_tables(past_len + np.arange(t_len))
    cos_s, sin_s = np.tile(cos_s, (tm_slab // t_len, 1)), np.tile(sin_s, (tm_slab // t_len, 1))

    skp, skp_heads, xnp = _project_heads_norm(x2p, gain2, w_in, W_IN_COL_SK, tm=tm_heads)
    sks, _, xns = _project_heads_norm(x2s, gain2, w_in, W_IN_COL_SK, tm=tm_heads)
    pp = _project_slab(xnp, cos_p, sin_p, w_in, tm=tm_slab, rope_period_tiles=seq // tm_slab, p_dtype=BF16)
    ps = _project_slab(xns, cos_s, sin_s, w_in, tm=tm_slab, rope_period_tiles=1, p_dtype=F32)
    svp, svp_heads = _project_heads(xnp, w_in, W_IN_COL_SV, tm=tm_slab)
    svs, _ = _project_heads(xns, w_in, W_IN_COL_SV, tm=tm_slab)

    rbp, state_p = _ret_prompt(pp, gn_gain2, batch, seq)
    rbs, state_s = _ret_sample(ps, state, gn_gain2, dec_batch, t_len)

    sbp = _sb_prompt(pp, skp_heads, svp_heads, sb_bias, batch, seq)
    sbs = _sb_sample(
        page_table, ps, sks, svs, pool_k.reshape(n_pool, page * H_SB, D_SB), pool_v.reshape(n_pool, page * H_SB, D_SB),
        sb_bias, dec_batch, t_len,
    )

    yp = _merge(x2p, rbp, sbp, pp, w_ret_b, w_sb_b, w_out_b, final_gain2, tm=256)
    ys = _merge(x2s, rbs, sbs, ps, w_ret_b, w_sb_b, w_out_b, final_gain2, tm=256)

    return (
        yp.reshape(batch, seq, D_MODEL),
        ys.reshape(dec_batch, t_len, D_MODEL),
        state_p,
        skp.reshape(batch, seq, H_SB, D_SB),
        svp.reshape(batch, seq, H_SB, D_SB),
        state_s,
        sks.reshape(dec_batch, t_len, H_SB, D_SB),
        svs.reshape(dec_batch, t_len, H_SB, D_SB),
    )


def kernel(x_prompt, x_sample, state_ret, cache_sb_k, cache_sb_v, page_table, norm_gain, w_in, sb_bias, ret_gn_gain, w_ret_proj, w_sb_proj, w_out, final_norm_gain):
    depth = w_in.shape[0]
    assert depth == 1, "single-layer trunk"
    outs = _layer(
        x_prompt, x_sample, state_ret[0], cache_sb_k[0], cache_sb_v[0], page_table, norm_gain[0], w_in[0],
        sb_bias[0], ret_gn_gain[0], w_ret_proj[0], w_sb_proj[0], w_out[0], final_norm_gain,
    )
    yp, ys, state_p, skp, svp, state_s, sks, svs = outs
    return (yp, ys, state_p[None], skp[None], svp[None], state_s[None], sks[None], svs[None])
```

```python
import functools

import jax
import jax.numpy as jnp
import numpy as np
from jax import lax
from jax.experimental import pallas as pl
from jax.experimental.pallas import tpu as pltpu

F32 = jnp.float32
BF16 = jnp.bfloat16

D_MODEL = 2048
H_RET = 8
DK_RET = 128
DV_RET = 256
RET_CHUNK = 128
H_SB = 8
D_SB = 128
ROPE_BASE = 10000.0
EPS_RMS = 1e-6
EPS_GN = 1e-5
W_QK = H_RET * DK_RET
W_V = H_RET * DV_RET
W_SB = H_SB * D_SB

PROJ_TN = 1024
PROJ_GROUP = 512
PROJ_GROUP_ROWS = 256
W_IN_COL_SK = 2 * W_QK + 2 * W_V + W_SB
W_IN_COL_SV = W_IN_COL_SK + W_SB
P_COLS = 2 * W_QK + 2 * W_V + 2 * W_SB + 2 * D_MODEL
N_P_TILES = P_COLS // PROJ_TN
COL_Q, COL_K, COL_V, COL_G, COL_SQ, COL_SG, COL_GR, COL_GS = 0, 1024, 2048, 4096, 6144, 7168, 8192, 10240

LOG2E = float(np.log2(np.e))
SB_Q_SCALE = D_SB**-0.5 * LOG2E

VMEM_LIMIT = 56 * 1024 * 1024

_NT = (((1,), (1,)), ((), ()))


def _silu(g):
    return g * jax.nn.sigmoid(g)


def _rmsnorm_body(x_ref, gain_ref, xn_ref):
    x = x_ref[...]
    r = lax.rsqrt(jnp.mean(x * x, axis=-1, keepdims=True) + EPS_RMS)
    xn_ref[...] = (x * r * gain_ref[...]).astype(xn_ref.dtype)


def _proj_groups(xn_ref, wb_ref, row_groups):
    rows = xn_ref.shape[0] // row_groups
    for c in range(PROJ_TN // PROJ_GROUP):
        for r in range(row_groups):
            acc = jnp.dot(
                xn_ref[r * rows : (r + 1) * rows, :],
                wb_ref[:, c * PROJ_GROUP : (c + 1) * PROJ_GROUP],
                preferred_element_type=F32,
            )
            yield r * rows, c * PROJ_GROUP, rows, acc


def _lane_blocks(acc):
    for h in range(PROJ_GROUP // DK_RET):
        yield h * DK_RET, acc[:, h * DK_RET : (h + 1) * DK_RET]


def _proj_slab_body(xn_ref, cos_ref, sin_ref, w_ref, p_ref, wb_ref, *, row_groups):
    j = pl.program_id(0)

    @pl.when(pl.program_id(1) == 0)
    def _cast():
        wb_ref[...] = w_ref[...].astype(BF16)

    groups = functools.partial(_proj_groups, xn_ref, wb_ref, row_groups)

    @pl.when(j < COL_V // PROJ_TN)
    def _rope():
        scale = jnp.where(j == COL_K // PROJ_TN, F32(DK_RET**-0.5), F32(1.0))
        for row0, col0, rows, acc in groups():
            cos = cos_ref[row0 : row0 + rows, :] * scale
            sin = sin_ref[row0 : row0 + rows, :] * scale
            for off, xh in _lane_blocks(acc):
                y = xh * cos + pltpu.roll(xh, DK_RET // 2, axis=1) * sin
                p_ref[row0 : row0 + rows, col0 + off : col0 + off + DK_RET] = y.astype(p_ref.dtype)

    @pl.when((j >= COL_V // PROJ_TN) & (j < COL_GR // PROJ_TN))
    def _plain():
        scale = jnp.where(j == COL_SQ // PROJ_TN, F32(SB_Q_SCALE), F32(1.0))
        for row0, col0, rows, acc in groups():
            p_ref[row0 : row0 + rows, col0 : col0 + PROJ_GROUP] = (acc * scale).astype(p_ref.dtype)

    @pl.when(j >= COL_GR // PROJ_TN)
    def _gate():
        for row0, col0, rows, acc in groups():
            y = 0.5 * jnp.tanh(0.5 * acc) + 0.5
            p_ref[row0 : row0 + rows, col0 : col0 + PROJ_GROUP] = y.astype(p_ref.dtype)


def _w_tile(j):
    first_kv = COL_SQ // PROJ_TN + 1
    return jnp.where(j < first_kv, j, j + 2 * W_SB // PROJ_TN)


def _project_slab(xn, cos, sin, w_in, *, tm, rope_period_tiles, p_dtype):
    m = xn.shape[0]
    return pl.pallas_call(
        functools.partial(_proj_slab_body, row_groups=tm // PROJ_GROUP_ROWS),
        grid=(N_P_TILES, m // tm),
        in_specs=[
            pl.BlockSpec((tm, D_MODEL), lambda j, i: (i, 0)),
            pl.BlockSpec((tm, DK_RET), lambda j, i: (i % rope_period_tiles, 0)),
            pl.BlockSpec((tm, DK_RET), lambda j, i: (i % rope_period_tiles, 0)),
            pl.BlockSpec((D_MODEL, PROJ_TN), lambda j, i: (0, _w_tile(j))),
        ],
        out_specs=pl.BlockSpec((tm, PROJ_TN), lambda j, i: (i, j)),
        out_shape=jax.ShapeDtypeStruct((m, P_COLS), p_dtype),
        scratch_shapes=[pltpu.VMEM((D_MODEL, PROJ_TN), BF16)],
        compiler_params=pltpu.CompilerParams(
            dimension_semantics=("arbitrary", "arbitrary"), vmem_limit_bytes=VMEM_LIMIT
        ),
        name="proj",
    )(xn, cos, sin, w_in)


def _proj_heads_body(xn_ref, w_ref, tok_ref, head_ref, wb_ref, *, row_groups):
    @pl.when(pl.program_id(0) == 0)
    def _cast():
        wb_ref[...] = w_ref[...].astype(BF16)

    for row0, col0, rows, acc in _proj_groups(xn_ref, wb_ref, row_groups):
        for off, ah in _lane_blocks(acc):
            head = (col0 + off) // D_SB
            tok_ref[pl.ds(row0 * H_SB + head, rows, stride=H_SB), :] = ah
            head_ref[head, row0 : row0 + rows, :] = ah.astype(BF16)


def _proj_heads_norm_body(x_ref, gain_ref, w_ref, tok_ref, head_ref, xn_ref, wb_ref, *, row_groups):
    _rmsnorm_body(x_ref, gain_ref, xn_ref)
    _proj_heads_body(xn_ref, w_ref, tok_ref, head_ref, wb_ref, row_groups=row_groups)


def _project_heads_norm(x, gain, w_in, col, *, tm):
    m = x.shape[0]
    return pl.pallas_call(
        functools.partial(_proj_heads_norm_body, row_groups=tm // PROJ_GROUP_ROWS),
        grid=(m // tm,),
        in_specs=[
            pl.BlockSpec((tm, D_MODEL), lambda i: (i, 0)),
            pl.BlockSpec((1, D_MODEL), lambda i: (0, 0)),
            pl.BlockSpec((D_MODEL, W_SB), lambda i: (0, col // W_SB), pipeline_mode=pl.Buffered(1)),
        ],
        out_specs=[
            pl.BlockSpec((tm * H_SB, D_SB), lambda i: (i, 0)),
            pl.BlockSpec((H_SB, tm, D_SB), lambda i: (0, i, 0)),
            pl.BlockSpec((tm, D_MODEL), lambda i: (i, 0)),
        ],
        out_shape=[
            jax.ShapeDtypeStruct((m * H_SB, D_SB), F32),
            jax.ShapeDtypeStruct((H_SB, m, D_SB), BF16),
            jax.ShapeDtypeStruct((m, D_MODEL), BF16),
        ],
        scratch_shapes=[pltpu.VMEM((D_MODEL, W_SB), BF16)],
        compiler_params=pltpu.CompilerParams(dimension_semantics=("arbitrary",), vmem_limit_bytes=VMEM_LIMIT),
        name="proj_heads_norm",
    )(x, gain, w_in)


def _project_heads(xn, w_in, col, *, tm):
    m = xn.shape[0]
    return pl.pallas_call(
        functools.partial(_proj_heads_body, row_groups=tm // PROJ_GROUP_ROWS),
        grid=(m // tm,),
        in_specs=[
            pl.BlockSpec((tm, D_MODEL), lambda i: (i, 0)),
            pl.BlockSpec((D_MODEL, W_SB), lambda i: (0, col // W_SB), pipeline_mode=pl.Buffered(1)),
        ],
        out_specs=[
            pl.BlockSpec((tm * H_SB, D_SB), lambda i: (i, 0)),
            pl.BlockSpec((H_SB, tm, D_SB), lambda i: (0, i, 0)),
        ],
        out_shape=[
            jax.ShapeDtypeStruct((m * H_SB, D_SB), F32),
            jax.ShapeDtypeStruct((H_SB, m, D_SB), BF16),
        ],
        scratch_shapes=[pltpu.VMEM((D_MODEL, W_SB), BF16)],
        compiler_params=pltpu.CompilerParams(dimension_semantics=("arbitrary",), vmem_limit_bytes=VMEM_LIMIT),
        name="proj_heads",
    )(xn, w_in)


def _group_norm_gate(o, gain, g):
    mu = jnp.mean(o, axis=-1, keepdims=True)
    d = o - mu
    var = jnp.mean(d * d, axis=-1, keepdims=True)
    return d * lax.rsqrt(var + EPS_GN) * gain * _silu(g)


def _ret_prompt_body(q_ref, k_ref, v_ref, g_ref, gain_ref, decay_ref, cd_ref, kd_ref, gs_ref, out_ref, s_ref):
    c = pl.program_id(1)

    @pl.when(c == 0)
    def _init():
        s_ref[...] = jnp.zeros_like(s_ref)

    heads = range(H_RET)
    q = [q_ref[:, h * DK_RET : (h + 1) * DK_RET] for h in heads]
    k = [k_ref[:, h * DK_RET : (h + 1) * DK_RET] for h in heads]
    v = [v_ref[:, h * DV_RET : (h + 1) * DV_RET] for h in heads]
    s = [s_ref[0, h] for h in heads]
    inner = [lax.dot_general(q[h], k[h], _NT, preferred_element_type=F32) for h in heads]
    cross = [jnp.dot(q[h], s[h].astype(BF16), preferred_element_type=F32) for h in heads]
    for h in heads:
        kdec_t = (k[h].astype(F32) * kd_ref[:, h : h + 1]).T.astype(BF16)
        s_ref[0, h] = gs_ref[h] * s[h] + jnp.dot(kdec_t, v[h], preferred_element_type=F32)
    for h in heads:
        o = jnp.dot((inner[h] * decay_ref[h]).astype(BF16), v[h], preferred_element_type=F32)
        o = o + cross[h] * cd_ref[:, h : h + 1]
        g = g_ref[:, h * DV_RET : (h + 1) * DV_RET].astype(F32)
        gain = gain_ref[:, h * DV_RET : (h + 1) * DV_RET]
        out_ref[:, h * DV_RET : (h + 1) * DV_RET] = _group_norm_gate(o, gain, g).astype(out_ref.dtype)


def _ret_tables(chunk):
    lg = np.log1p(-np.exp2(-5.0 - np.arange(H_RET, dtype=np.float64)))
    idx = np.arange(chunk, dtype=np.float64)
    diff = idx[:, None] - idx[None, :]
    decay = np.where(diff >= 0, np.exp(lg[:, None, None] * np.maximum(diff, 0.0)[None]), 0.0)
    cross = np.exp(lg[:, None] * (idx + 1.0)[None, :]).T
    kdec = np.exp(lg[:, None] * (chunk - 1.0 - idx)[None, :]).T
    sdec = np.broadcast_to(np.exp(lg * chunk)[:, None, None], (H_RET, 1, DV_RET))
    return tuple(np.asarray(a, np.float32) for a in (decay, cross, kdec, sdec))


def _ret_prompt(p, gn_gain, batch, seq):
    nc = seq // RET_CHUNK
    decay, cross, kdec, sdec = _ret_tables(RET_CHUNK)
    row = lambda b, c: b * nc + c
    const2 = lambda b, c: (0, 0)
    const3 = lambda b, c: (0, 0, 0)
    return pl.pallas_call(
        _ret_prompt_body,
        grid=(batch, nc),
        in_specs=[
            pl.BlockSpec((RET_CHUNK, W_QK), lambda b, c: (row(b, c), COL_Q // W_QK)),
            pl.BlockSpec((RET_CHUNK, W_QK), lambda b, c: (row(b, c), COL_K // W_QK)),
            pl.BlockSpec((RET_CHUNK, W_V), lambda b, c: (row(b, c), COL_V // W_V)),
            pl.BlockSpec((RET_CHUNK, W_V), lambda b, c: (row(b, c), COL_G // W_V)),
            pl.BlockSpec((1, W_V), const2),
            pl.BlockSpec((H_RET, RET_CHUNK, RET_CHUNK), const3),
            pl.BlockSpec((RET_CHUNK, H_RET), const2),
            pl.BlockSpec((RET_CHUNK, H_RET), const2),
            pl.BlockSpec((H_RET, 1, DV_RET), const3),
        ],
        out_specs=[
            pl.BlockSpec((RET_CHUNK, W_V), lambda b, c: (row(b, c), 0)),
            pl.BlockSpec((1, H_RET, DK_RET, DV_RET), lambda b, c: (b, 0, 0, 0)),
        ],
        out_shape=[
            jax.ShapeDtypeStruct((batch * seq, W_V), BF16),
            jax.ShapeDtypeStruct((batch, H_RET, DK_RET, DV_RET), F32),
        ],
        compiler_params=pltpu.CompilerParams(
            dimension_semantics=("arbitrary", "arbitrary"), vmem_limit_bytes=VMEM_LIMIT
        ),
        name="ret_prompt",
    )(p, p, p, p, gn_gain, decay, cross, kdec, sdec)


RS_SEQS = 32


RS_SLOTS = 3


def _ret_sample_body(
    q_ref, k_ref, v_ref, g_ref, state_ref, gain_ref, decay_ref, cd_ref, kd_ref, gs_ref, out_ref, snew_ref, sbuf_ref, sem_ref
):
    n_seq = sbuf_ref.shape[1]
    rows = q_ref.shape[0]
    t_len = rows // n_seq
    n_heads = pl.num_programs(1)
    step = pl.program_id(0) * n_heads + pl.program_id(1)
    n_steps = pl.num_programs(0) * n_heads

    def state_copy(t):
        block = state_ref.at[pl.ds((t // n_heads) * n_seq, n_seq), t % n_heads]
        slot = t % RS_SLOTS
        return pltpu.make_async_copy(block, sbuf_ref.at[slot], sem_ref.at[slot])

    @pl.when(step == 0)
    def _prime():
        for t in range(RS_SLOTS - 1):
            state_copy(t).start()

    @pl.when(step + RS_SLOTS - 1 < n_steps)
    def _refill():
        state_copy(step + RS_SLOTS - 1).start()

    state_copy(step).wait()
    q = q_ref[...]
    k = k_ref[...]
    v = v_ref[...].astype(BF16)
    s = sbuf_ref[step % RS_SLOTS]
    inner = lax.dot_general(q.astype(BF16), k.astype(BF16), _NT, preferred_element_type=F32) * decay_ref[...]
    o = jnp.dot(inner.astype(BF16), v, preferred_element_type=F32)
    row_seq = lax.broadcasted_iota(jnp.int32, (rows, DK_RET), 0) // t_len
    q_exp = jnp.concatenate([jnp.where(row_seq == b, q, 0.0) for b in range(n_seq)], axis=1).astype(BF16)
    s_rows = s.reshape(n_seq * DK_RET, DV_RET).astype(BF16)
    o = o + jnp.dot(q_exp, s_rows, preferred_element_type=F32) * cd_ref[...]
    kdec_t = (k * kd_ref[...]).T
    col_seq = lax.broadcasted_iota(jnp.int32, (DK_RET, rows), 1) // t_len
    k_exp_t = jnp.concatenate([jnp.where(col_seq == b, kdec_t, 0.0) for b in range(n_seq)], axis=0).astype(BF16)
    upd = jnp.dot(k_exp_t, v, preferred_element_type=F32)
    snew_ref[...] = gs_ref[...] * s + upd.reshape(n_seq, DK_RET, DV_RET)
    out_ref[...] = _group_norm_gate(o, gain_ref[...], g_ref[...])


def _ret_sample(p, state, gn_gain, batch, t_len):
    decay, cross, kdec, sdec = _ret_tables(t_len)
    rows = RS_SEQS * t_len
    decay_bd = np.stack([np.kron(np.eye(RS_SEQS, dtype=np.float32), decay[h]) for h in range(H_RET)])
    cross_rows = np.tile(cross.T, (1, RS_SEQS))[:, :, None]
    kdec_rows = np.tile(kdec.T, (1, RS_SEQS))[:, :, None]
    qk_blk = lambda col: pl.BlockSpec((rows, DK_RET), lambda i, h: (i, col // DK_RET + h))
    v_blk = lambda col: pl.BlockSpec((rows, DV_RET), lambda i, h: (i, col // DV_RET + h))
    state_blk = pl.BlockSpec((RS_SEQS, None, DK_RET, DV_RET), lambda i, h: (i, h, 0, 0))
    return pl.pallas_call(
        _ret_sample_body,
        grid=(batch // RS_SEQS, H_RET),
        in_specs=[
            qk_blk(COL_Q),
            qk_blk(COL_K),
            v_blk(COL_V),
            v_blk(COL_G),
            pl.BlockSpec(memory_space=pl.ANY),
            pl.BlockSpec((1, DV_RET), lambda i, h: (0, h)),
            pl.BlockSpec((None, rows, rows), lambda i, h: (h, 0, 0)),
            pl.BlockSpec((None, rows, 1), lambda i, h: (h, 0, 0)),
            pl.BlockSpec((None, rows, 1), lambda i, h: (h, 0, 0)),
            pl.BlockSpec((None, 1, DV_RET), lambda i, h: (h, 0, 0)),
        ],
        out_specs=[pl.BlockSpec((rows, DV_RET), lambda i, h: (i, h)), state_blk],
        out_shape=[
            jax.ShapeDtypeStruct((batch * t_len, W_V), F32),
            jax.ShapeDtypeStruct((batch, H_RET, DK_RET, DV_RET), F32),
        ],
        scratch_shapes=[
            pltpu.VMEM((RS_SLOTS, RS_SEQS, DK_RET, DV_RET), F32),
            pltpu.SemaphoreType.DMA((RS_SLOTS,)),
        ],
        compiler_params=pltpu.CompilerParams(
            dimension_semantics=("arbitrary", "arbitrary"), vmem_limit_bytes=VMEM_LIMIT
        ),
        name="ret_sample",
    )(p, p, p, p, state, gn_gain, decay_bd, cross_rows, kdec_rows, sdec)


def _sb_weights(qk2, tri, bias2, carry, mask):
    drop, log2_beta = _sb_drop(qk2, bias2, mask)
    return _sb_finish(drop, log2_beta, _sb_after(drop, tri), carry, mask)


def _sb_drop(qk2, bias2, mask):
    z2 = qk2 + bias2
    drop = jnp.maximum(z2, 0.0) + jnp.log2(1.0 + jnp.exp2(-jnp.abs(z2)))
    log2_beta = z2 - drop
    if mask is not None:
        drop = jnp.where(mask, drop, 0.0)
    return drop, log2_beta


def _sb_after(drop, tri):
    return jnp.dot(drop.astype(BF16), tri, preferred_element_type=F32)


def _sb_finish(drop, log2_beta, after, carry, mask):
    a = jnp.exp2(log2_beta - after - carry)
    if mask is not None:
        a = jnp.where(mask, a, 0.0)
    return a, carry + after[:, 0:1] + drop[:, 0:1]


def _tri_np(n):
    return np.tril(np.ones((n, n), np.float32), -1)


SBP_T = 256
SBP_HEADS = 8


def _sb_prompt_body(bias_ref, q_ref, k_ref, v_ref, sg_ref, tri_ref, out_ref):
    hg = pl.program_id(1)
    i = pl.program_id(2)
    heads = range(SBP_HEADS)
    bias2 = [bias_ref[hg * SBP_HEADS + g] for g in heads]
    tri = tri_ref[...]
    q = [q_ref[:, g * D_SB : (g + 1) * D_SB] for g in heads]
    r = lax.broadcasted_iota(jnp.int32, (SBP_T, SBP_T), 0)
    c = lax.broadcasted_iota(jnp.int32, (SBP_T, SBP_T), 1)

    def tile(kb, state, mask):
        off = pl.multiple_of(kb * SBP_T, SBP_T)
        qk2 = [lax.dot_general(q[g], k_ref[g, pl.ds(off, SBP_T), :], _NT, preferred_element_type=F32) for g in heads]
        dl = [_sb_drop(qk2[g], bias2[g], mask) for g in heads]
        after = [_sb_after(dl[g][0], tri) for g in heads]
        new_state = []
        for g in heads:
            carry, acc = state[g]
            a, carry = _sb_finish(dl[g][0], dl[g][1], after[g], carry, mask)
            acc = acc + jnp.dot(a.astype(BF16), v_ref[g, pl.ds(off, SBP_T), :], preferred_element_type=F32)
            new_state.append((carry, acc))
        return tuple(new_state)

    state = tuple((jnp.zeros((SBP_T, 1), F32), jnp.zeros((SBP_T, D_SB), F32)) for _ in heads)
    state = tile(i, state, c < r)
    state = lax.fori_loop(0, i, lambda n, st: tile(i - 1 - n, st, None), state)
    acc = jnp.concatenate([st[1] for st in state], axis=1)
    out_ref[...] = (acc * _silu(sg_ref[...].astype(F32))).astype(out_ref.dtype)


def _sb_prompt(p, sk_heads, sv_heads, sb_bias, batch, seq):
    nq = seq // SBP_T
    width = SBP_HEADS * D_SB
    return pl.pallas_call(
        _sb_prompt_body,
        grid=(batch, H_SB // SBP_HEADS, nq),
        in_specs=[
            pl.BlockSpec(memory_space=pltpu.SMEM),
            pl.BlockSpec((SBP_T, width), lambda b, h, i: (b * nq + i, COL_SQ // width + h)),
            pl.BlockSpec((SBP_HEADS, seq, D_SB), lambda b, h, i: (h, b, 0)),
            pl.BlockSpec((SBP_HEADS, seq, D_SB), lambda b, h, i: (h, b, 0)),
            pl.BlockSpec((SBP_T, width), lambda b, h, i: (b * nq + i, COL_SG // width + h)),
            pl.BlockSpec((SBP_T, SBP_T), lambda b, h, i: (0, 0)),
        ],
        out_specs=pl.BlockSpec((SBP_T, width), lambda b, h, i: (b * nq + i, h)),
        out_shape=jax.ShapeDtypeStruct((batch * seq, W_SB), BF16),
        compiler_params=pltpu.CompilerParams(
            dimension_semantics=("arbitrary", "arbitrary", "arbitrary"), vmem_limit_bytes=VMEM_LIMIT
        ),
        name="sb_prompt",
    )(sb_bias * LOG2E, p, sk_heads, sv_heads, p, jnp.asarray(_tri_np(SBP_T), BF16))


SBS_CHUNK_PAGES = 4


SBS_SLOTS = 4


def _sb_sample_body(
    pt_ref, bias_ref, tri_ref, q_ref, kn_ref, vn_ref, sg_ref, kpool_ref, vpool_ref,
    out_ref, kbuf_ref, vbuf_ref, sem_ref, kd0_ref, kd1_ref, vd0_ref, vd1_ref, knd_ref, vnd_ref, kstage_ref, vstage_ref,
    *, chunks_per_seq,
):
    b = pl.program_id(0)
    n_seq = pl.num_programs(0)
    t_len = q_ref.shape[0]
    rows = H_SB * t_len
    half = H_SB // 2
    ppage = kbuf_ref.shape[2] // half
    lookahead = SBS_SLOTS - 1
    kd_refs, vd_refs = (kd0_ref, kd1_ref), (vd0_ref, vd1_ref)

    def chunk_copies(n, slot):
        copies = []
        for pg in range(SBS_CHUNK_PAGES):
            page_id = pt_ref[n * SBS_CHUNK_PAGES + pg]
            copies.append(pltpu.make_async_copy(kpool_ref.at[page_id], kbuf_ref.at[slot, pg], sem_ref.at[0, slot]))
            copies.append(pltpu.make_async_copy(vpool_ref.at[page_id], vbuf_ref.at[slot, pg], sem_ref.at[1, slot]))
        return copies

    def densify(src_ref, dst_ref, row0, n_pseudo):
        for j in range(half):
            dst_ref[row0 : row0 + n_pseudo, j * D_SB : (j + 1) * D_SB] = src_ref[
                pl.ds(j, n_pseudo, stride=half), :
            ].astype(dst_ref.dtype)

    @pl.when(b == 0)
    def _prime():
        for n in range(lookahead):
            for cp in chunk_copies(n, n % SBS_SLOTS):
                cp.start()

    q = q_ref[...]
    zeros = jnp.zeros((t_len, D_SB), F32)
    qbd = jnp.concatenate(
        [
            jnp.concatenate([q[:, h * D_SB : (h + 1) * D_SB] if j == h % half else zeros for j in range(half)], axis=1)
            for h in range(H_SB)
        ],
        axis=0,
    ).astype(BF16)
    bias2 = bias_ref[...]
    tri = tri_ref[...]

    def parity_ok(n_pseudo):
        r = lax.broadcasted_iota(jnp.int32, (rows, n_pseudo), 0)
        p = lax.broadcasted_iota(jnp.int32, (rows, n_pseudo), 1)
        return (p % 2) == (r // (half * t_len)), r, p

    def sweep(k, v, n_pages, carry, acc, mask):
        qk2 = lax.dot_general(qbd, k, _NT, preferred_element_type=F32)
        drop, log2_beta = _sb_drop(qk2, bias2, mask)
        pages = [slice(pg * ppage, (pg + 1) * ppage) for pg in range(n_pages)]
        stacked = jnp.concatenate([drop[:, cols] for cols in pages], axis=0)
        after_stacked = _sb_after(stacked, tri)
        afters = [after_stacked[pg * rows : (pg + 1) * rows] for pg in range(n_pages)]
        shifted = [None] * n_pages
        for pg in reversed(range(n_pages)):
            shifted[pg] = afters[pg] + carry
            carry = carry + afters[pg][:, 0:1] + drop[:, pg * ppage : pg * ppage + 1]
        a = jnp.where(mask, jnp.exp2(log2_beta - jnp.concatenate(shifted, axis=1)), 0.0)
        return carry, acc + jnp.dot(a.astype(BF16), v, preferred_element_type=F32)

    for src_ref, stage_ref, dst_ref in ((kn_ref, kstage_ref, knd_ref), (vn_ref, vstage_ref, vnd_ref)):
        stage_ref[...] = jnp.zeros_like(stage_ref)
        densify(src_ref, stage_ref, 0, 2 * t_len)
        dst_ref[...] = stage_ref[...].astype(BF16)
    ok, r, p = parity_ok(ppage)
    carry, acc = sweep(
        knd_ref[...], vnd_ref[...], 1,
        jnp.zeros((rows, 1), F32), jnp.zeros((rows, half * D_SB), F32), ok & ((p // 2) < (r % t_len)),
    )

    chunk_ok, _, _ = parity_ok(SBS_CHUNK_PAGES * ppage)
    for c in range(chunks_per_seq):
        n = b * chunks_per_seq + c
        slot = c % SBS_SLOTS
        for cp in chunk_copies(n, slot):
            cp.wait()
        for cp in chunk_copies(n + lookahead, (c + lookahead) % SBS_SLOTS):
            cp.start()
        kd_ref, vd_ref = kd_refs[c % 2], vd_refs[c % 2]
        for pg in range(SBS_CHUNK_PAGES):
            densify(kbuf_ref.at[slot, pg], kd_ref, pg * ppage, ppage)
            densify(vbuf_ref.at[slot, pg], vd_ref, pg * ppage, ppage)
        carry, acc = sweep(kd_ref[...], vd_ref[...], SBS_CHUNK_PAGES, carry, acc, chunk_ok)

    o = jnp.concatenate(
        [acc[h * t_len : (h + 1) * t_len, (h % half) * D_SB : (h % half + 1) * D_SB] for h in range(H_SB)], axis=1
    )
    out_ref[...] = o * _silu(sg_ref[...])

    @pl.when(b == n_seq - 1)
    def _drain():
        for i in range(lookahead):
            for cp in chunk_copies(n_seq * chunks_per_seq + i, i % SBS_SLOTS):
                cp.wait()


def _sb_sample(page_table, p, sk, sv, pool_k, pool_v, sb_bias, batch, t_len):
    n_pages = page_table.shape[1]
    chunks_per_seq = n_pages // SBS_CHUNK_PAGES
    assert n_pages % SBS_CHUNK_PAGES == 0 and chunks_per_seq % SBS_SLOTS == 0
    page_rows = pool_k.shape[1]
    rows = H_SB * t_len
    half = H_SB // 2
    ppage = page_rows // half
    width = half * D_SB
    bias_rows = jnp.repeat(sb_bias * LOG2E, t_len)[:, None]
    tri = jnp.asarray(_tri_np(ppage), BF16)
    fetch = page_table.reshape(batch, chunks_per_seq, SBS_CHUNK_PAGES)[:, ::-1, :].reshape(-1)
    fetch = jnp.concatenate([fetch, fetch[: (SBS_SLOTS - 1) * SBS_CHUNK_PAGES]])

    const2 = lambda b, pt: (0, 0)
    grid_spec = pltpu.PrefetchScalarGridSpec(
        num_scalar_prefetch=1,
        grid=(batch,),
        in_specs=[
            pl.BlockSpec((rows, 1), const2),
            pl.BlockSpec((ppage, ppage), const2),
            pl.BlockSpec((t_len, W_SB), lambda b, pt: (b, COL_SQ // W_SB)),
            pl.BlockSpec((rows, D_SB), lambda b, pt: (b, 0)),
            pl.BlockSpec((rows, D_SB), lambda b, pt: (b, 0)),
            pl.BlockSpec((t_len, W_SB), lambda b, pt: (b, COL_SG // W_SB)),
            pl.BlockSpec(memory_space=pl.ANY),
            pl.BlockSpec(memory_space=pl.ANY),
        ],
        out_specs=pl.BlockSpec((t_len, W_SB), lambda b, pt: (b, 0)),
        scratch_shapes=[
            pltpu.VMEM((SBS_SLOTS, SBS_CHUNK_PAGES, page_rows, D_SB), F32),
            pltpu.VMEM((SBS_SLOTS, SBS_CHUNK_PAGES, page_rows, D_SB), F32),
            pltpu.SemaphoreType.DMA((2, SBS_SLOTS)),
            pltpu.VMEM((SBS_CHUNK_PAGES * ppage, width), BF16),
            pltpu.VMEM((SBS_CHUNK_PAGES * ppage, width), BF16),
            pltpu.VMEM((SBS_CHUNK_PAGES * ppage, width), BF16),
            pltpu.VMEM((SBS_CHUNK_PAGES * ppage, width), BF16),
            pltpu.VMEM((ppage, width), BF16),
            pltpu.VMEM((ppage, width), BF16),
            pltpu.VMEM((ppage, width), F32),
            pltpu.VMEM((ppage, width), F32),
        ],
    )
    return pl.pallas_call(
        functools.partial(_sb_sample_body, chunks_per_seq=chunks_per_seq),
        grid_spec=grid_spec,
        out_shape=jax.ShapeDtypeStruct((batch * t_len, W_SB), F32),
        compiler_params=pltpu.CompilerParams(dimension_semantics=("arbitrary",), vmem_limit_bytes=VMEM_LIMIT),
        name="sb_sample",
    )(fetch, bias_rows, tri, p, sk, sv, p, pool_k, pool_v)


def _merge_body(x_ref, rb_ref, sb_ref, gr_ref, gs_ref, wr_ref, ws_ref, wo_ref, fg_ref, y_ref):
    m = gr_ref[...].astype(F32) * jnp.dot(rb_ref[...].astype(BF16), wr_ref[...], preferred_element_type=F32)
    m = m + gs_ref[...].astype(F32) * jnp.dot(sb_ref[...].astype(BF16), ws_ref[...], preferred_element_type=F32)
    hid = x_ref[...] + jnp.dot(m.astype(BF16), wo_ref[...], preferred_element_type=F32)
    r = lax.rsqrt(jnp.mean(hid * hid, axis=-1, keepdims=True) + EPS_RMS)
    y_ref[...] = hid * r * fg_ref[...]


def _merge(x, ret_branch, sb_branch, p, w_ret, w_sb, w_out, final_gain, *, tm):
    m = x.shape[0]
    resident = functools.partial(pl.BlockSpec, index_map=lambda i: (0, 0), pipeline_mode=pl.Buffered(1))
    return pl.pallas_call(
        _merge_body,
        grid=(m // tm,),
        in_specs=[
            pl.BlockSpec((tm, D_MODEL), lambda i: (i, 0)),
            pl.BlockSpec((tm, W_V), lambda i: (i, 0)),
            pl.BlockSpec((tm, W_SB), lambda i: (i, 0)),
            pl.BlockSpec((tm, D_MODEL), lambda i: (i, COL_GR // D_MODEL)),
            pl.BlockSpec((tm, D_MODEL), lambda i: (i, COL_GS // D_MODEL)),
            resident((W_V, D_MODEL)),
            resident((W_SB, D_MODEL)),
            resident((D_MODEL, D_MODEL)),
            pl.BlockSpec((1, D_MODEL), lambda i: (0, 0)),
        ],
        out_specs=pl.BlockSpec((tm, D_MODEL), lambda i: (i, 0)),
        out_shape=jax.ShapeDtypeStruct((m, D_MODEL), F32),
        compiler_params=pltpu.CompilerParams(dimension_semantics=("arbitrary",), vmem_limit_bytes=VMEM_LIMIT),
        name="merge",
    )(x, ret_branch, sb_branch, p, p, w_ret, w_sb, w_out, final_gain)


def _rope_tables(pos):
    half = DK_RET // 2
    inv = ROPE_BASE ** (-np.arange(half, dtype=np.float64) / half)
    ang = np.asarray(pos, np.float64)[:, None] * inv[None, :]
    cos, sin = np.cos(ang), np.sin(ang)
    cos_full = np.concatenate([cos, cos], axis=-1)
    sin_signed = np.concatenate([-sin, sin], axis=-1)
    return cos_full.astype(np.float32), sin_signed.astype(np.float32)


def _layer(xp, xs, state, pool_k, pool_v, page_table, norm_gain, w_in, sb_bias, gn_gain, w_ret, w_sb, w_out, final_gain):
    batch, seq, _ = xp.shape
    dec_batch, t_len, _ = xs.shape
    n_pool, page = pool_k.shape[:2]
    past_len = page_table.shape[1] * page

    w_ret_b, w_sb_b, w_out_b = w_ret.astype(BF16), w_sb.astype(BF16), w_out.astype(BF16)
    gain2 = norm_gain[None, :]
    gn_gain2 = gn_gain[None, :]
    final_gain2 = final_gain[None, :]

    x2p = xp.reshape(batch * seq, D_MODEL)
    x2s = xs.reshape(dec_batch * t_len, D_MODEL)
    tm_slab, tm_heads = 1024, 512
    cos_p, sin_p = _rope_tables(np.arange(seq))
    cos_s, sin_s = _rope_tables(past_len + np.arange(t_len))
    cos_s, sin_s = np.tile(cos_s, (tm_slab // t_len, 1)), np.tile(sin_s, (tm_slab // t_len, 1))

    skp, skp_heads, xnp = _project_heads_norm(x2p, gain2, w_in, W_IN_COL_SK, tm=tm_heads)
    sks, _, xns = _project_heads_norm(x2s, gain2, w_in, W_IN_COL_SK, tm=tm_heads)
    pp = _project_slab(xnp, cos_p, sin_p, w_in, tm=tm_slab, rope_period_tiles=seq // tm_slab, p_dtype=BF16)
    ps = _project_slab(xns, cos_s, sin_s, w_in, tm=tm_slab, rope_period_tiles=1, p_dtype=F32)
    svp, svp_heads = _project_heads(xnp, w_in, W_IN_COL_SV, tm=tm_slab)
    svs, _ = _project_heads(xns, w_in, W_IN_COL_SV, tm=tm_slab)

    rbp, state_p = _ret_prompt(pp, gn_gain2, batch, seq)
    rbs, state_s = _ret_sample(ps, state, gn_gain2, dec_batch, t_len)

    sbp = _sb_prompt(pp, skp_heads, svp_heads, sb_bias, batch, seq)
    sbs = _sb_sample(
        page_table, ps, sks, svs, pool_k.reshape(n_pool, page * H_SB, D_SB), pool_v.reshape(n_pool, page * H_SB, D_SB),
        sb_bias, dec_batch, t_len,
    )

    yp = _merge(x2p, rbp, sbp, pp, w_ret_b, w_sb_b, w_out_b, final_gain2, tm=256)
    ys = _merge(x2s, rbs, sbs, ps, w_ret_b, w_sb_b, w_out_b, final_gain2, tm=256)

    return (
        yp.reshape(batch, seq, D_MODEL),
        ys.reshape(dec_batch, t_len, D_MODEL),
        state_p,
        skp.reshape(batch, seq, H_SB, D_SB),
        svp.reshape(batch, seq, H_SB, D_SB),
        state_s,
        sks.reshape(dec_batch, t_len, H_SB, D_SB),
        svs.reshape(dec_batch, t_len, H_SB, D_SB),
    )


def kernel(x_prompt, x_sample, state_ret, cache_sb_k, cache_sb_v, page_table, norm_gain, w_in, sb_bias, ret_gn_gain, w_ret_proj, w_sb_proj, w_out, final_norm_gain):
    depth = w_in.shape[0]
    assert depth == 1, "single-layer trunk"
    outs = _layer(
        x_prompt, x_sample, state_ret[0], cache_sb_k[0], cache_sb_v[0], page_table, norm_gain[0], w_in[0],
        sb_bias[0], ret_gn_gain[0], w_ret_proj[0], w_sb_proj[0], w_out[0], final_norm_gain,
    )
    yp, ys, state_p, skp, svp, state_s, sks, svs = outs
    return (yp, ys, state_p[None], skp[None], svp[None], state_s[None], sks[None], svs[None])
```

```python
import functools

import jax
import jax.numpy as jnp
import numpy as np
from jax import lax
from jax.experimental import pallas as pl
from jax.experimental.pallas import tpu as pltpu

F32 = jnp.float32
BF16 = jnp.bfloat16

D_MODEL = 2048
H_RET = 8
DK_RET = 128
DV_RET = 256
RET_CHUNK = 128
H_SB = 8
D_SB = 128
ROPE_BASE = 10000.0
EPS_RMS = 1e-6
EPS_GN = 1e-5
W_QK = H_RET * DK_RET
W_V = H_RET * DV_RET
W_SB = H_SB * D_SB

PROJ_TN = 1024
PROJ_GROUP = 512
PROJ_GROUP_ROWS = 256
W_IN_COL_SK = 2 * W_QK + 2 * W_V + W_SB
W_IN_COL_SV = W_IN_COL_SK + W_SB
P_COLS = 2 * W_QK + 2 * W_V + 2 * W_SB + 2 * D_MODEL
N_P_TILES = P_COLS // PROJ_TN
COL_Q, COL_K, COL_V, COL_G, COL_SQ, COL_SG, COL_GR, COL_GS = 0, 1024, 2048, 4096, 6144, 7168, 8192, 10240

LOG2E = float(np.log2(np.e))
SB_Q_SCALE = D_SB**-0.5 * LOG2E

VMEM_LIMIT = 56 * 1024 * 1024

_NT = (((1,), (1,)), ((), ()))


def _silu(g):
    return g * jax.nn.sigmoid(g)


def _rmsnorm_body(x_ref, gain_ref, xn_ref):
    x = x_ref[...]
    r = lax.rsqrt(jnp.mean(x * x, axis=-1, keepdims=True) + EPS_RMS)
    xn_ref[...] = (x * r * gain_ref[...]).astype(xn_ref.dtype)


def _proj_groups(xn_ref, wb_ref, row_groups):
    rows = xn_ref.shape[0] // row_groups
    for c in range(PROJ_TN // PROJ_GROUP):
        for r in range(row_groups):
            acc = jnp.dot(
                xn_ref[r * rows : (r + 1) * rows, :],
                wb_ref[:, c * PROJ_GROUP : (c + 1) * PROJ_GROUP],
                preferred_element_type=F32,
            )
            yield r * rows, c * PROJ_GROUP, rows, acc


def _lane_blocks(acc):
    for h in range(PROJ_GROUP // DK_RET):
        yield h * DK_RET, acc[:, h * DK_RET : (h + 1) * DK_RET]


def _proj_slab_body(xn_ref, cos_ref, sin_ref, w_ref, p_ref, wb_ref, *, row_groups):
    j = pl.program_id(0)

    @pl.when(pl.program_id(1) == 0)
    def _cast():
        wb_ref[...] = w_ref[...].astype(BF16)

    groups = functools.partial(_proj_groups, xn_ref, wb_ref, row_groups)

    @pl.when(j < COL_V // PROJ_TN)
    def _rope():
        scale = jnp.where(j == COL_K // PROJ_TN, F32(DK_RET**-0.5), F32(1.0))
        for row0, col0, rows, acc in groups():
            cos = cos_ref[row0 : row0 + rows, :] * scale
            sin = sin_ref[row0 : row0 + rows, :] * scale
            for off, xh in _lane_blocks(acc):
                y = xh * cos + pltpu.roll(xh, DK_RET // 2, axis=1) * sin
                p_ref[row0 : row0 + rows, col0 + off : col0 + off + DK_RET] = y.astype(p_ref.dtype)

    @pl.when((j >= COL_V // PROJ_TN) & (j < COL_GR // PROJ_TN))
    def _plain():
        scale = jnp.where(j == COL_SQ // PROJ_TN, F32(SB_Q_SCALE), F32(1.0))
        for row0, col0, rows, acc in groups():
            p_ref[row0 : row0 + rows, col0 : col0 + PROJ_GROUP] = (acc * scale).astype(p_ref.dtype)

    @pl.when(j >= COL_GR // PROJ_TN)
    def _gate():
        for row0, col0, rows, acc in groups():
            y = 0.5 * jnp.tanh(0.5 * acc) + 0.5
            p_ref[row0 : row0 + rows, col0 : col0 + PROJ_GROUP] = y.astype(p_ref.dtype)


def _w_tile(j):
    first_kv = COL_SQ // PROJ_TN + 1
    return jnp.where(j < first_kv, j, j + 2 * W_SB // PROJ_TN)


def _project_slab(xn, cos, sin, w_in, *, tm, rope_period_tiles, p_dtype):
    m = xn.shape[0]
    return pl.pallas_call(
        functools.partial(_proj_slab_body, row_groups=tm // PROJ_GROUP_ROWS),
        grid=(N_P_TILES, m // tm),
        in_specs=[
            pl.BlockSpec((tm, D_MODEL), lambda j, i: (i, 0)),
            pl.BlockSpec((tm, DK_RET), lambda j, i: (i % rope_period_tiles, 0)),
            pl.BlockSpec((tm, DK_RET), lambda j, i: (i % rope_period_tiles, 0)),
            pl.BlockSpec((D_MODEL, PROJ_TN), lambda j, i: (0, _w_tile(j))),
        ],
        out_specs=pl.BlockSpec((tm, PROJ_TN), lambda j, i: (i, j)),
        out_shape=jax.ShapeDtypeStruct((m, P_COLS), p_dtype),
        scratch_shapes=[pltpu.VMEM((D_MODEL, PROJ_TN), BF16)],
        compiler_params=pltpu.CompilerParams(
            dimension_semantics=("arbitrary", "arbitrary"), vmem_limit_bytes=VMEM_LIMIT
        ),
        name="proj",
    )(xn, cos, sin, w_in)


def _proj_heads_body(xn_ref, w_ref, tok_ref, head_ref, wb_ref, *, row_groups):
    @pl.when(pl.program_id(0) == 0)
    def _cast():
        wb_ref[...] = w_ref[...].astype(BF16)

    for row0, col0, rows, acc in _proj_groups(xn_ref, wb_ref, row_groups):
        for off, ah in _lane_blocks(acc):
            head = (col0 + off) // D_SB
            tok_ref[pl.ds(row0 * H_SB + head, rows, stride=H_SB), :] = ah
            head_ref[head, row0 : row0 + rows, :] = ah.astype(BF16)


def _proj_heads_norm_body(x_ref, gain_ref, w_ref, tok_ref, head_ref, xn_ref, wb_ref, *, row_groups):
    _rmsnorm_body(x_ref, gain_ref, xn_ref)
    _proj_heads_body(xn_ref, w_ref, tok_ref, head_ref, wb_ref, row_groups=row_groups)


def _project_heads_norm(x, gain, w_in, col, *, tm):
    m = x.shape[0]
    return pl.pallas_call(
        functools.partial(_proj_heads_norm_body, row_groups=tm // PROJ_GROUP_ROWS),
        grid=(m // tm,),
        in_specs=[
            pl.BlockSpec((tm, D_MODEL), lambda i: (i, 0)),
            pl.BlockSpec((1, D_MODEL), lambda i: (0, 0)),
            pl.BlockSpec((D_MODEL, W_SB), lambda i: (0, col // W_SB), pipeline_mode=pl.Buffered(1)),
        ],
        out_specs=[
            pl.BlockSpec((tm * H_SB, D_SB), lambda i: (i, 0)),
            pl.BlockSpec((H_SB, tm, D_SB), lambda i: (0, i, 0)),
            pl.BlockSpec((tm, D_MODEL), lambda i: (i, 0)),
        ],
        out_shape=[
            jax.ShapeDtypeStruct((m * H_SB, D_SB), F32),
            jax.ShapeDtypeStruct((H_SB, m, D_SB), BF16),
            jax.ShapeDtypeStruct((m, D_MODEL), BF16),
        ],
        scratch_shapes=[pltpu.VMEM((D_MODEL, W_SB), BF16)],
        compiler_params=pltpu.CompilerParams(dimension_semantics=("arbitrary",), vmem_limit_bytes=VMEM_LIMIT),
        name="proj_heads_norm",
    )(x, gain, w_in)


def _proj_heads_cast_body(xn_ref, w_ref, *rest, row_groups):
    n = (len(rest) - 3) // 2
    src_refs, (tok_ref, head_ref), dst_refs, wb_ref = rest[:n], rest[n : n + 2], rest[n + 2 : 2 * n + 2], rest[-1]
    _proj_heads_body(xn_ref, w_ref, tok_ref, head_ref, wb_ref, row_groups=row_groups)
    for src_ref, dst_ref in zip(src_refs, dst_refs):
        dst_ref[...] = src_ref[...].astype(dst_ref.dtype)


def _project_heads_cast(xn, w_in, col, weights, *, tm):
    m = xn.shape[0]
    steps = m // tm
    band = lambda w: pl.BlockSpec((w.shape[0] // steps, w.shape[1]), lambda i: (i, 0))
    outs = pl.pallas_call(
        functools.partial(_proj_heads_cast_body, row_groups=tm // PROJ_GROUP_ROWS),
        grid=(steps,),
        in_specs=[
            pl.BlockSpec((tm, D_MODEL), lambda i: (i, 0)),
            pl.BlockSpec((D_MODEL, W_SB), lambda i: (0, col // W_SB), pipeline_mode=pl.Buffered(1)),
        ]
        + [band(w) for w in weights],
        out_specs=[
            pl.BlockSpec((tm * H_SB, D_SB), lambda i: (i, 0)),
            pl.BlockSpec((H_SB, tm, D_SB), lambda i: (0, i, 0)),
        ]
        + [band(w) for w in weights],
        out_shape=[
            jax.ShapeDtypeStruct((m * H_SB, D_SB), F32),
            jax.ShapeDtypeStruct((H_SB, m, D_SB), BF16),
        ]
        + [jax.ShapeDtypeStruct(w.shape, BF16) for w in weights],
        scratch_shapes=[pltpu.VMEM((D_MODEL, W_SB), BF16)],
        compiler_params=pltpu.CompilerParams(dimension_semantics=("arbitrary",), vmem_limit_bytes=VMEM_LIMIT),
        name="proj_heads_cast",
    )(xn, w_in, *weights)
    return outs[0], outs[1], outs[2:]


def _project_heads(xn, w_in, col, *, tm):
    m = xn.shape[0]
    return pl.pallas_call(
        functools.partial(_proj_heads_body, row_groups=tm // PROJ_GROUP_ROWS),
        grid=(m // tm,),
        in_specs=[
            pl.BlockSpec((tm, D_MODEL), lambda i: (i, 0)),
            pl.BlockSpec((D_MODEL, W_SB), lambda i: (0, col // W_SB), pipeline_mode=pl.Buffered(1)),
        ],
        out_specs=[
            pl.BlockSpec((tm * H_SB, D_SB), lambda i: (i, 0)),
            pl.BlockSpec((H_SB, tm, D_SB), lambda i: (0, i, 0)),
        ],
        out_shape=[
            jax.ShapeDtypeStruct((m * H_SB, D_SB), F32),
            jax.ShapeDtypeStruct((H_SB, m, D_SB), BF16),
        ],
        scratch_shapes=[pltpu.VMEM((D_MODEL, W_SB), BF16)],
        compiler_params=pltpu.CompilerParams(dimension_semantics=("arbitrary",), vmem_limit_bytes=VMEM_LIMIT),
        name="proj_heads",
    )(xn, w_in)


def _group_norm_gate(o, gain, g):
    mu = jnp.mean(o, axis=-1, keepdims=True)
    d = o - mu
    var = jnp.mean(d * d, axis=-1, keepdims=True)
    return d * lax.rsqrt(var + EPS_GN) * gain * _silu(g)


def _ret_prompt_body(q_ref, k_ref, v_ref, g_ref, gain_ref, decay_ref, cd_ref, kd_ref, gs_ref, out_ref, s_ref):
    c = pl.program_id(1)

    @pl.when(c == 0)
    def _init():
        s_ref[...] = jnp.zeros_like(s_ref)

    heads = range(H_RET)
    q = [q_ref[:, h * DK_RET : (h + 1) * DK_RET] for h in heads]
    k = [k_ref[:, h * DK_RET : (h + 1) * DK_RET] for h in heads]
    v = [v_ref[:, h * DV_RET : (h + 1) * DV_RET] for h in heads]
    s = [s_ref[0, h] for h in heads]
    inner = [lax.dot_general(q[h], k[h], _NT, preferred_element_type=F32) for h in heads]
    cross = [jnp.dot(q[h], s[h].astype(BF16), preferred_element_type=F32) for h in heads]
    for h in heads:
        kdec_t = (k[h].astype(F32) * kd_ref[:, h : h + 1]).T.astype(BF16)
        s_ref[0, h] = gs_ref[h] * s[h] + jnp.dot(kdec_t, v[h], preferred_element_type=F32)
    for h in heads:
        o = jnp.dot((inner[h] * decay_ref[h]).astype(BF16), v[h], preferred_element_type=F32)
        o = o + cross[h] * cd_ref[:, h : h + 1]
        g = g_ref[:, h * DV_RET : (h + 1) * DV_RET].astype(F32)
        gain = gain_ref[:, h * DV_RET : (h + 1) * DV_RET]
        out_ref[:, h * DV_RET : (h + 1) * DV_RET] = _group_norm_gate(o, gain, g).astype(out_ref.dtype)


def _ret_tables(chunk):
    lg = np.log1p(-np.exp2(-5.0 - np.arange(H_RET, dtype=np.float64)))
    idx = np.arange(chunk, dtype=np.float64)
    diff = idx[:, None] - idx[None, :]
    decay = np.where(diff >= 0, np.exp(lg[:, None, None] * np.maximum(diff, 0.0)[None]), 0.0)
    cross = np.exp(lg[:, None] * (idx + 1.0)[None, :]).T
    kdec = np.exp(lg[:, None] * (chunk - 1.0 - idx)[None, :]).T
    sdec = np.broadcast_to(np.exp(lg * chunk)[:, None, None], (H_RET, 1, DV_RET))
    return tuple(np.asarray(a, np.float32) for a in (decay, cross, kdec, sdec))


def _ret_prompt(p, gn_gain, batch, seq):
    nc = seq // RET_CHUNK
    decay, cross, kdec, sdec = _ret_tables(RET_CHUNK)
    row = lambda b, c: b * nc + c
    const2 = lambda b, c: (0, 0)
    const3 = lambda b, c: (0, 0, 0)
    return pl.pallas_call(
        _ret_prompt_body,
        grid=(batch, nc),
        in_specs=[
            pl.BlockSpec((RET_CHUNK, W_QK), lambda b, c: (row(b, c), COL_Q // W_QK)),
            pl.BlockSpec((RET_CHUNK, W_QK), lambda b, c: (row(b, c), COL_K // W_QK)),
            pl.BlockSpec((RET_CHUNK, W_V), lambda b, c: (row(b, c), COL_V // W_V)),
            pl.BlockSpec((RET_CHUNK, W_V), lambda b, c: (row(b, c), COL_G // W_V)),
            pl.BlockSpec((1, W_V), const2),
            pl.BlockSpec((H_RET, RET_CHUNK, RET_CHUNK), const3),
            pl.BlockSpec((RET_CHUNK, H_RET), const2),
            pl.BlockSpec((RET_CHUNK, H_RET), const2),
            pl.BlockSpec((H_RET, 1, DV_RET), const3),
        ],
        out_specs=[
            pl.BlockSpec((RET_CHUNK, W_V), lambda b, c: (row(b, c), 0)),
            pl.BlockSpec((1, H_RET, DK_RET, DV_RET), lambda b, c: (b, 0, 0, 0)),
        ],
        out_shape=[
            jax.ShapeDtypeStruct((batch * seq, W_V), BF16),
            jax.ShapeDtypeStruct((batch, H_RET, DK_RET, DV_RET), F32),
        ],
        compiler_params=pltpu.CompilerParams(
            dimension_semantics=("arbitrary", "arbitrary"), vmem_limit_bytes=VMEM_LIMIT
        ),
        name="ret_prompt",
    )(p, p, p, p, gn_gain, decay, cross, kdec, sdec)


RS_SEQS = 32


RS_SLOTS = 3


def _ret_sample_body(
    q_ref, k_ref, v_ref, g_ref, state_ref, gain_ref, decay_ref, cd_ref, kd_ref, gs_ref, out_ref, snew_ref, sbuf_ref, sem_ref
):
    n_seq = sbuf_ref.shape[1]
    rows = q_ref.shape[0]
    t_len = rows // n_seq
    n_heads = pl.num_programs(1)
    step = pl.program_id(0) * n_heads + pl.program_id(1)
    n_steps = pl.num_programs(0) * n_heads

    def state_copy(t):
        block = state_ref.at[pl.ds((t // n_heads) * n_seq, n_seq), t % n_heads]
        slot = t % RS_SLOTS
        return pltpu.make_async_copy(block, sbuf_ref.at[slot], sem_ref.at[slot])

    @pl.when(step == 0)
    def _prime():
        for t in range(RS_SLOTS - 1):
            state_copy(t).start()

    @pl.when(step + RS_SLOTS - 1 < n_steps)
    def _refill():
        state_copy(step + RS_SLOTS - 1).start()

    state_copy(step).wait()
    q = q_ref[...]
    k = k_ref[...]
    v = v_ref[...].astype(BF16)
    s = sbuf_ref[step % RS_SLOTS]
    inner = lax.dot_general(q.astype(BF16), k.astype(BF16), _NT, preferred_element_type=F32) * decay_ref[...]
    o = jnp.dot(inner.astype(BF16), v, preferred_element_type=F32)
    row_seq = lax.broadcasted_iota(jnp.int32, (rows, DK_RET), 0) // t_len
    q_exp = jnp.concatenate([jnp.where(row_seq == b, q, 0.0) for b in range(n_seq)], axis=1).astype(BF16)
    s_rows = s.reshape(n_seq * DK_RET, DV_RET).astype(BF16)
    o = o + jnp.dot(q_exp, s_rows, preferred_element_type=F32) * cd_ref[...]
    kdec_t = (k * kd_ref[...]).T
    col_seq = lax.broadcasted_iota(jnp.int32, (DK_RET, rows), 1) // t_len
    k_exp_t = jnp.concatenate([jnp.where(col_seq == b, kdec_t, 0.0) for b in range(n_seq)], axis=0).astype(BF16)
    upd = jnp.dot(k_exp_t, v, preferred_element_type=F32)
    snew_ref[...] = gs_ref[...] * s + upd.reshape(n_seq, DK_RET, DV_RET)
    out_ref[...] = _group_norm_gate(o, gain_ref[...], g_ref[...])


def _ret_sample(p, state, gn_gain, batch, t_len):
    decay, cross, kdec, sdec = _ret_tables(t_len)
    rows = RS_SEQS * t_len
    decay_bd = np.stack([np.kron(np.eye(RS_SEQS, dtype=np.float32), decay[h]) for h in range(H_RET)])
    cross_rows = np.tile(cross.T, (1, RS_SEQS))[:, :, None]
    kdec_rows = np.tile(kdec.T, (1, RS_SEQS))[:, :, None]
    qk_blk = lambda col: pl.BlockSpec((rows, DK_RET), lambda i, h: (i, col // DK_RET + h))
    v_blk = lambda col: pl.BlockSpec((rows, DV_RET), lambda i, h: (i, col // DV_RET + h))
    state_blk = pl.BlockSpec((RS_SEQS, None, DK_RET, DV_RET), lambda i, h: (i, h, 0, 0))
    return pl.pallas_call(
        _ret_sample_body,
        grid=(batch // RS_SEQS, H_RET),
        in_specs=[
            qk_blk(COL_Q),
            qk_blk(COL_K),
            v_blk(COL_V),
            v_blk(COL_G),
            pl.BlockSpec(memory_space=pl.ANY),
            pl.BlockSpec((1, DV_RET), lambda i, h: (0, h)),
            pl.BlockSpec((None, rows, rows), lambda i, h: (h, 0, 0)),
            pl.BlockSpec((None, rows, 1), lambda i, h: (h, 0, 0)),
            pl.BlockSpec((None, rows, 1), lambda i, h: (h, 0, 0)),
            pl.BlockSpec((None, 1, DV_RET), lambda i, h: (h, 0, 0)),
        ],
        out_specs=[pl.BlockSpec((rows, DV_RET), lambda i, h: (i, h)), state_blk],
        out_shape=[
            jax.ShapeDtypeStruct((batch * t_len, W_V), F32),
            jax.ShapeDtypeStruct((batch, H_RET, DK_RET, DV_RET), F32),
        ],
        scratch_shapes=[
            pltpu.VMEM((RS_SLOTS, RS_SEQS, DK_RET, DV_RET), F32),
            pltpu.SemaphoreType.DMA((RS_SLOTS,)),
        ],
        compiler_params=pltpu.CompilerParams(
            dimension_semantics=("arbitrary", "arbitrary"), vmem_limit_bytes=VMEM_LIMIT
        ),
        name="ret_sample",
    )(p, p, p, p, state, gn_gain, decay_bd, cross_rows, kdec_rows, sdec)


def _sb_weights(qk2, tri, bias2, carry, mask):
    drop, log2_beta = _sb_drop(qk2, bias2, mask)
    return _sb_finish(drop, log2_beta, _sb_after(drop, tri), carry, mask)


def _sb_drop(qk2, bias2, mask):
    z2 = qk2 + bias2
    drop = jnp.maximum(z2, 0.0) + jnp.log2(1.0 + jnp.exp2(-jnp.abs(z2)))
    log2_beta = z2 - drop
    if mask is not None:
        drop = jnp.where(mask, drop, 0.0)
    return drop, log2_beta


def _sb_after(drop, tri):
    return jnp.dot(drop.astype(BF16), tri, preferred_element_type=F32)


def _sb_finish(drop, log2_beta, after, carry, mask):
    a = jnp.exp2(log2_beta - after - carry)
    if mask is not None:
        a = jnp.where(mask, a, 0.0)
    return a, carry + after[:, 0:1] + drop[:, 0:1]


def _tri_np(n):
    return np.tril(np.ones((n, n), np.float32), -1)


SBP_T = 256
SBP_HEADS = 8


def _sb_prompt_body(bias_ref, q_ref, k_ref, v_ref, sg_ref, tri_ref, out_ref):
    hg = pl.program_id(1)
    i = pl.program_id(2)
    heads = range(SBP_HEADS)
    bias2 = [bias_ref[hg * SBP_HEADS + g] for g in heads]
    tri = tri_ref[...]
    q = [q_ref[:, g * D_SB : (g + 1) * D_SB] for g in heads]
    r = lax.broadcasted_iota(jnp.int32, (SBP_T, SBP_T), 0)
    c = lax.broadcasted_iota(jnp.int32, (SBP_T, SBP_T), 1)

    def tile(kb, state, mask):
        off = pl.multiple_of(kb * SBP_T, SBP_T)
        qk2 = [lax.dot_general(q[g], k_ref[g, pl.ds(off, SBP_T), :], _NT, preferred_element_type=F32) for g in heads]
        dl = [_sb_drop(qk2[g], bias2[g], mask) for g in heads]
        after = [_sb_after(dl[g][0], tri) for g in heads]
        new_state = []
        for g in heads:
            carry, acc = state[g]
            a, carry = _sb_finish(dl[g][0], dl[g][1], after[g], carry, mask)
            acc = acc + jnp.dot(a.astype(BF16), v_ref[g, pl.ds(off, SBP_T), :], preferred_element_type=F32)
            new_state.append((carry, acc))
        return tuple(new_state)

    state = tuple((jnp.zeros((SBP_T, 1), F32), jnp.zeros((SBP_T, D_SB), F32)) for _ in heads)
    state = tile(i, state, c < r)
    state = lax.fori_loop(0, i, lambda n, st: tile(i - 1 - n, st, None), state)
    acc = jnp.concatenate([st[1] for st in state], axis=1)
    out_ref[...] = (acc * _silu(sg_ref[...].astype(F32))).astype(out_ref.dtype)


def _sb_prompt(p, sk_heads, sv_heads, sb_bias, batch, seq):
    nq = seq // SBP_T
    width = SBP_HEADS * D_SB
    return pl.pallas_call(
        _sb_prompt_body,
        grid=(batch, H_SB // SBP_HEADS, nq),
        in_specs=[
            pl.BlockSpec(memory_space=pltpu.SMEM),
            pl.BlockSpec((SBP_T, width), lambda b, h, i: (b * nq + i, COL_SQ // width + h)),
            pl.BlockSpec((SBP_HEADS, seq, D_SB), lambda b, h, i: (h, b, 0)),
            pl.BlockSpec((SBP_HEADS, seq, D_SB), lambda b, h, i: (h, b, 0)),
            pl.BlockSpec((SBP_T, width), lambda b, h, i: (b * nq + i, COL_SG // width + h)),
            pl.BlockSpec((SBP_T, SBP_T), lambda b, h, i: (0, 0)),
        ],
        out_specs=pl.BlockSpec((SBP_T, width), lambda b, h, i: (b * nq + i, h)),
        out_shape=jax.ShapeDtypeStruct((batch * seq, W_SB), BF16),
        compiler_params=pltpu.CompilerParams(
            dimension_semantics=("arbitrary", "arbitrary", "arbitrary"), vmem_limit_bytes=VMEM_LIMIT
        ),
        name="sb_prompt",
    )(sb_bias * LOG2E, p, sk_heads, sv_heads, p, jnp.asarray(_tri_np(SBP_T), BF16))


SBS_CHUNK_PAGES = 4


SBS_SLOTS = 4


def _sb_sample_body(
    pt_ref, bias_ref, tri_ref, q_ref, kn_ref, vn_ref, sg_ref, kpool_ref, vpool_ref,
    out_ref, kbuf_ref, vbuf_ref, sem_ref, kd0_ref, kd1_ref, vd0_ref, vd1_ref, knd_ref, vnd_ref, kstage_ref, vstage_ref,
    *, chunks_per_seq,
):
    b = pl.program_id(0)
    n_seq = pl.num_programs(0)
    t_len = q_ref.shape[0]
    rows = H_SB * t_len
    half = H_SB // 2
    ppage = kbuf_ref.shape[2] // half
    lookahead = SBS_SLOTS - 1
    kd_refs, vd_refs = (kd0_ref, kd1_ref), (vd0_ref, vd1_ref)

    def chunk_copies(n, slot):
        copies = []
        for pg in range(SBS_CHUNK_PAGES):
            page_id = pt_ref[n * SBS_CHUNK_PAGES + pg]
            copies.append(pltpu.make_async_copy(kpool_ref.at[page_id], kbuf_ref.at[slot, pg], sem_ref.at[0, slot]))
            copies.append(pltpu.make_async_copy(vpool_ref.at[page_id], vbuf_ref.at[slot, pg], sem_ref.at[1, slot]))
        return copies

    def densify(src_ref, dst_ref, row0, n_pseudo):
        for j in range(half):
            dst_ref[row0 : row0 + n_pseudo, j * D_SB : (j + 1) * D_SB] = src_ref[
                pl.ds(j, n_pseudo, stride=half), :
            ].astype(dst_ref.dtype)

    @pl.when(b == 0)
    def _prime():
        for n in range(lookahead):
            for cp in chunk_copies(n, n % SBS_SLOTS):
                cp.start()

    q = q_ref[...]
    zeros = jnp.zeros((t_len, D_SB), F32)
    qbd = jnp.concatenate(
        [
            jnp.concatenate([q[:, h * D_SB : (h + 1) * D_SB] if j == h % half else zeros for j in range(half)], axis=1)
            for h in range(H_SB)
        ],
        axis=0,
    ).astype(BF16)
    bias2 = bias_ref[...]
    tri = tri_ref[...]

    def parity_ok(n_pseudo):
        r = lax.broadcasted_iota(jnp.int32, (rows, n_pseudo), 0)
        p = lax.broadcasted_iota(jnp.int32, (rows, n_pseudo), 1)
        return (p % 2) == (r // (half * t_len)), r, p

    def sweep(k, v, n_pages, carry, acc, mask):
        qk2 = lax.dot_general(qbd, k, _NT, preferred_element_type=F32)
        drop, log2_beta = _sb_drop(qk2, bias2, mask)
        pages = [slice(pg * ppage, (pg + 1) * ppage) for pg in range(n_pages)]
        stacked = jnp.concatenate([drop[:, cols] for cols in pages], axis=0)
        after_stacked = _sb_after(stacked, tri)
        afters = [after_stacked[pg * rows : (pg + 1) * rows] for pg in range(n_pages)]
        shifted = [None] * n_pages
        for pg in reversed(range(n_pages)):
            shifted[pg] = afters[pg] + carry
            carry = carry + afters[pg][:, 0:1] + drop[:, pg * ppage : pg * ppage + 1]
        a = jnp.where(mask, jnp.exp2(log2_beta - jnp.concatenate(shifted, axis=1)), 0.0)
        return carry, acc + jnp.dot(a.astype(BF16), v, preferred_element_type=F32)

    for src_ref, stage_ref, dst_ref in ((kn_ref, kstage_ref, knd_ref), (vn_ref, vstage_ref, vnd_ref)):
        stage_ref[...] = jnp.zeros_like(stage_ref)
        densify(src_ref, stage_ref, 0, 2 * t_len)
        dst_ref[...] = stage_ref[...].astype(BF16)
    ok, r, p = parity_ok(ppage)
    carry, acc = sweep(
        knd_ref[...], vnd_ref[...], 1,
        jnp.zeros((rows, 1), F32), jnp.zeros((rows, half * D_SB), F32), ok & ((p // 2) < (r % t_len)),
    )

    chunk_ok, _, _ = parity_ok(SBS_CHUNK_PAGES * ppage)
    for c in range(chunks_per_seq):
        n = b * chunks_per_seq + c
        slot = c % SBS_SLOTS
        for cp in chunk_copies(n, slot):
            cp.wait()
        for cp in chunk_copies(n + lookahead, (c + lookahead) % SBS_SLOTS):
            cp.start()
        kd_ref, vd_ref = kd_refs[c % 2], vd_refs[c % 2]
        for pg in range(SBS_CHUNK_PAGES):
            densify(kbuf_ref.at[slot, pg], kd_ref, pg * ppage, ppage)
            densify(vbuf_ref.at[slot, pg], vd_ref, pg * ppage, ppage)
        carry, acc = sweep(kd_ref[...], vd_ref[...], SBS_CHUNK_PAGES, carry, acc, chunk_ok)

    o = jnp.concatenate(
        [acc[h * t_len : (h + 1) * t_len, (h % half) * D_SB : (h % half + 1) * D_SB] for h in range(H_SB)], axis=1
    )
    out_ref[...] = o * _silu(sg_ref[...])

    @pl.when(b == n_seq - 1)
    def _drain():
        for i in range(lookahead):
            for cp in chunk_copies(n_seq * chunks_per_seq + i, i % SBS_SLOTS):
                cp.wait()


def _sb_sample(page_table, p, sk, sv, pool_k, pool_v, sb_bias, batch, t_len):
    n_pages = page_table.shape[1]
    chunks_per_seq = n_pages // SBS_CHUNK_PAGES
    assert n_pages % SBS_CHUNK_PAGES == 0 and chunks_per_seq % SBS_SLOTS == 0
    page_rows = pool_k.shape[1]
    rows = H_SB * t_len
    half = H_SB // 2
    ppage = page_rows // half
    width = half * D_SB
    bias_rows = jnp.repeat(sb_bias * LOG2E, t_len)[:, None]
    tri = jnp.asarray(_tri_np(ppage), BF16)
    fetch = page_table.reshape(batch, chunks_per_seq, SBS_CHUNK_PAGES)[:, ::-1, :].reshape(-1)
    fetch = jnp.concatenate([fetch, fetch[: (SBS_SLOTS - 1) * SBS_CHUNK_PAGES]])

    const2 = lambda b, pt: (0, 0)
    grid_spec = pltpu.PrefetchScalarGridSpec(
        num_scalar_prefetch=1,
        grid=(batch,),
        in_specs=[
            pl.BlockSpec((rows, 1), const2),
            pl.BlockSpec((ppage, ppage), const2),
            pl.BlockSpec((t_len, W_SB), lambda b, pt: (b, COL_SQ // W_SB)),
            pl.BlockSpec((rows, D_SB), lambda b, pt: (b, 0)),
            pl.BlockSpec((rows, D_SB), lambda b, pt: (b, 0)),
            pl.BlockSpec((t_len, W_SB), lambda b, pt: (b, COL_SG // W_SB)),
            pl.BlockSpec(memory_space=pl.ANY),
            pl.BlockSpec(memory_space=pl.ANY),
        ],
        out_specs=pl.BlockSpec((t_len, W_SB), lambda b, pt: (b, 0)),
        scratch_shapes=[
            pltpu.VMEM((SBS_SLOTS, SBS_CHUNK_PAGES, page_rows, D_SB), F32),
            pltpu.VMEM((SBS_SLOTS, SBS_CHUNK_PAGES, page_rows, D_SB), F32),
            pltpu.SemaphoreType.DMA((2, SBS_SLOTS)),
            pltpu.VMEM((SBS_CHUNK_PAGES * ppage, width), BF16),
            pltpu.VMEM((SBS_CHUNK_PAGES * ppage, width), BF16),
            pltpu.VMEM((SBS_CHUNK_PAGES * ppage, width), BF16),
            pltpu.VMEM((SBS_CHUNK_PAGES * ppage, width), BF16),
            pltpu.VMEM((ppage, width), BF16),
            pltpu.VMEM((ppage, width), BF16),
            pltpu.VMEM((ppage, width), F32),
            pltpu.VMEM((ppage, width), F32),
        ],
    )
    return pl.pallas_call(
        functools.partial(_sb_sample_body, chunks_per_seq=chunks_per_seq),
        grid_spec=grid_spec,
        out_shape=jax.ShapeDtypeStruct((batch * t_len, W_SB), F32),
        compiler_params=pltpu.CompilerParams(dimension_semantics=("arbitrary",), vmem_limit_bytes=VMEM_LIMIT),
        name="sb_sample",
    )(fetch, bias_rows, tri, p, sk, sv, p, pool_k, pool_v)


def _merge_body(x_ref, rb_ref, sb_ref, gr_ref, gs_ref, wr_ref, ws_ref, wo_ref, fg_ref, y_ref):
    m = gr_ref[...].astype(F32) * jnp.dot(rb_ref[...].astype(BF16), wr_ref[...], preferred_element_type=F32)
    m = m + gs_ref[...].astype(F32) * jnp.dot(sb_ref[...].astype(BF16), ws_ref[...], preferred_element_type=F32)
    hid = x_ref[...] + jnp.dot(m.astype(BF16), wo_ref[...], preferred_element_type=F32)
    r = lax.rsqrt(jnp.mean(hid * hid, axis=-1, keepdims=True) + EPS_RMS)
    y_ref[...] = hid * r * fg_ref[...]


def _merge(x, ret_branch, sb_branch, p, w_ret, w_sb, w_out, final_gain, *, tm):
    m = x.shape[0]
    resident = functools.partial(pl.BlockSpec, index_map=lambda i: (0, 0), pipeline_mode=pl.Buffered(1))
    return pl.pallas_call(
        _merge_body,
        grid=(m // tm,),
        in_specs=[
            pl.BlockSpec((tm, D_MODEL), lambda i: (i, 0)),
            pl.BlockSpec((tm, W_V), lambda i: (i, 0)),
            pl.BlockSpec((tm, W_SB), lambda i: (i, 0)),
            pl.BlockSpec((tm, D_MODEL), lambda i: (i, COL_GR // D_MODEL)),
            pl.BlockSpec((tm, D_MODEL), lambda i: (i, COL_GS // D_MODEL)),
            resident((W_V, D_MODEL)),
            resident((W_SB, D_MODEL)),
            resident((D_MODEL, D_MODEL)),
            pl.BlockSpec((1, D_MODEL), lambda i: (0, 0)),
        ],
        out_specs=pl.BlockSpec((tm, D_MODEL), lambda i: (i, 0)),
        out_shape=jax.ShapeDtypeStruct((m, D_MODEL), F32),
        compiler_params=pltpu.CompilerParams(dimension_semantics=("arbitrary",), vmem_limit_bytes=VMEM_LIMIT),
        name="merge",
    )(x, ret_branch, sb_branch, p, p, w_ret, w_sb, w_out, final_gain)


def _rope_tables(pos):
    half = DK_RET // 2
    inv = ROPE_BASE ** (-np.arange(half, dtype=np.float64) / half)
    ang = np.asarray(pos, np.float64)[:, None] * inv[None, :]
    cos, sin = np.cos(ang), np.sin(ang)
    cos_full = np.concatenate([cos, cos], axis=-1)
    sin_signed = np.concatenate([-sin, sin], axis=-1)
    return cos_full.astype(np.float32), sin_signed.astype(np.float32)


def _layer(xp, xs, state, pool_k, pool_v, page_table, norm_gain, w_in, sb_bias, gn_gain, w_ret, w_sb, w_out, final_gain):
    batch, seq, _ = xp.shape
    dec_batch, t_len, _ = xs.shape
    n_pool, page = pool_k.shape[:2]
    past_len = page_table.shape[1] * page

    gain2 = norm_gain[None, :]
    gn_gain2 = gn_gain[None, :]
    final_gain2 = final_gain[None, :]

    x2p = xp.reshape(batch * seq, D_MODEL)
    x2s = xs.reshape(dec_batch * t_len, D_MODEL)
    tm_slab, tm_heads = 1024, 512
    cos_p, sin_p = _rope_tables(np.arange(seq))
    cos_s, sin_s = _rope_tables(past_len + np.arange(t_len))
    cos_s, sin_s = np.tile(cos_s, (tm_slab // t_len, 1)), np.tile(sin_s, (tm_slab // t_len, 1))

    skp, skp_heads, xnp = _project_heads_norm(x2p, gain2, w_in, W_IN_COL_SK, tm=tm_heads)
    sks, _, xns = _project_heads_norm(x2s, gain2, w_in, W_IN_COL_SK, tm=tm_heads)
    pp = _project_slab(xnp, cos_p, sin_p, w_in, tm=tm_slab, rope_period_tiles=seq // tm_slab, p_dtype=BF16)
    ps = _project_slab(xns, cos_s, sin_s, w_in, tm=tm_slab, rope_period_tiles=1, p_dtype=F32)
    svp, svp_heads, (w_ret_b, w_sb_b, w_out_b) = _project_heads_cast(
        xnp, w_in, W_IN_COL_SV, (w_ret, w_sb, w_out), tm=tm_slab
    )
    svs, _ = _project_heads(xns, w_in, W_IN_COL_SV, tm=tm_slab)

    rbp, state_p = _ret_prompt(pp, gn_gain2, batch, seq)
    rbs, state_s = _ret_sample(ps, state, gn_gain2, dec_batch, t_len)

    sbp = _sb_prompt(pp, skp_heads, svp_heads, sb_bias, batch, seq)
    sbs = _sb_sample(
        page_table, ps, sks, svs, pool_k.reshape(n_pool, page * H_SB, D_SB), pool_v.reshape(n_pool, page * H_SB, D_SB),
        sb_bias, dec_batch, t_len,
    )

    yp = _merge(x2p, rbp, sbp, pp, w_ret_b, w_sb_b, w_out_b, final_gain2, tm=256)
    ys = _merge(x2s, rbs, sbs, ps, w_ret_b, w_sb_b, w_out_b, final_gain2, tm=256)

    return (
        yp.reshape(batch, seq, D_MODEL),
        ys.reshape(dec_batch, t_len, D_MODEL),
        state_p,
        skp.reshape(batch, seq, H_SB, D_SB),
        svp.reshape(batch, seq, H_SB, D_SB),
        state_s,
        sks.reshape(dec_batch, t_len, H_SB, D_SB),
        svs.reshape(dec_batch, t_len, H_SB, D_SB),
    )


def kernel(x_prompt, x_sample, state_ret, cache_sb_k, cache_sb_v, page_table, norm_gain, w_in, sb_bias, ret_gn_gain, w_ret_proj, w_sb_proj, w_out, final_norm_gain):
    depth = w_in.shape[0]
    assert depth == 1, "single-layer trunk"
    outs = _layer(
        x_prompt, x_sample, state_ret[0], cache_sb_k[0], cache_sb_v[0], page_table, norm_gain[0], w_in[0],
        sb_bias[0], ret_gn_gain[0], w_ret_proj[0], w_sb_proj[0], w_out[0], final_norm_gain,
    )
    yp, ys, state_p, skp, svp, state_s, sks, svs = outs
    return (yp, ys, state_p[None], skp[None], svp[None], state_s[None], sks[None], svs[None])
```

```python
import functools

import jax
import jax.numpy as jnp
import numpy as np
from jax import lax
from jax.experimental import pallas as pl
from jax.experimental.pallas import tpu as pltpu

F32 = jnp.float32
BF16 = jnp.bfloat16

D_MODEL = 2048
H_RET = 8
DK_RET = 128
DV_RET = 256
RET_CHUNK = 128
H_SB = 8
D_SB = 128
ROPE_BASE = 10000.0
EPS_RMS = 1e-6
EPS_GN = 1e-5
W_QK = H_RET * DK_RET
W_V = H_RET * DV_RET
W_SB = H_SB * D_SB

PROJ_TN = 1024
PROJ_GROUP = 512
PROJ_GROUP_ROWS = 256
W_IN_COL_SK = 2 * W_QK + 2 * W_V + W_SB
W_IN_COL_SV = W_IN_COL_SK + W_SB
P_COLS = 2 * W_QK + 2 * W_V + 2 * W_SB + 2 * D_MODEL
N_P_TILES = P_COLS // PROJ_TN
COL_Q, COL_K, COL_V, COL_G, COL_SQ, COL_SG, COL_GR, COL_GS = 0, 1024, 2048, 4096, 6144, 7168, 8192, 10240

LOG2E = float(np.log2(np.e))
SB_Q_SCALE = D_SB**-0.5 * LOG2E

VMEM_LIMIT = 56 * 1024 * 1024

_NT = (((1,), (1,)), ((), ()))


def _silu(g):
    return g * jax.nn.sigmoid(g)


def _rmsnorm_body(x_ref, gain_ref, xn_ref):
    x = x_ref[...]
    r = lax.rsqrt(jnp.mean(x * x, axis=-1, keepdims=True) + EPS_RMS)
    xn_ref[...] = (x * r * gain_ref[...]).astype(xn_ref.dtype)


def _proj_groups(xn_ref, wb_ref, row_groups):
    rows = xn_ref.shape[0] // row_groups
    for c in range(PROJ_TN // PROJ_GROUP):
        for r in range(row_groups):
            acc = jnp.dot(
                xn_ref[r * rows : (r + 1) * rows, :],
                wb_ref[:, c * PROJ_GROUP : (c + 1) * PROJ_GROUP],
                preferred_element_type=F32,
            )
            yield r * rows, c * PROJ_GROUP, rows, acc


def _lane_blocks(acc):
    for h in range(PROJ_GROUP // DK_RET):
        yield h * DK_RET, acc[:, h * DK_RET : (h + 1) * DK_RET]


def _proj_slab_body(xn_ref, cos_ref, sin_ref, w_ref, p_ref, wb_ref, *, row_groups):
    j = pl.program_id(0)

    @pl.when(pl.program_id(1) == 0)
    def _cast():
        wb_ref[...] = w_ref[...].astype(BF16)

    groups = functools.partial(_proj_groups, xn_ref, wb_ref, row_groups)

    @pl.when(j < COL_V // PROJ_TN)
    def _rope():
        scale = jnp.where(j == COL_K // PROJ_TN, F32(DK_RET**-0.5), F32(1.0))
        for row0, col0, rows, acc in groups():
            cos = cos_ref[row0 : row0 + rows, :] * scale
            sin = sin_ref[row0 : row0 + rows, :] * scale
            for off, xh in _lane_blocks(acc):
                y = xh * cos + pltpu.roll(xh, DK_RET // 2, axis=1) * sin
                p_ref[row0 : row0 + rows, col0 + off : col0 + off + DK_RET] = y.astype(p_ref.dtype)

    @pl.when((j >= COL_V // PROJ_TN) & (j < COL_GR // PROJ_TN))
    def _plain():
        scale = jnp.where(j == COL_SQ // PROJ_TN, F32(SB_Q_SCALE), F32(1.0))
        for row0, col0, rows, acc in groups():
            p_ref[row0 : row0 + rows, col0 : col0 + PROJ_GROUP] = (acc * scale).astype(p_ref.dtype)

    @pl.when(j >= COL_GR // PROJ_TN)
    def _gate():
        for row0, col0, rows, acc in groups():
            y = 0.5 * jnp.tanh(0.5 * acc) + 0.5
            p_ref[row0 : row0 + rows, col0 : col0 + PROJ_GROUP] = y.astype(p_ref.dtype)


def _w_tile(j):
    first_kv = COL_SQ // PROJ_TN + 1
    return jnp.where(j < first_kv, j, j + 2 * W_SB // PROJ_TN)


def _project_slab(xn, cos, sin, w_in, *, tm, rope_period_tiles, p_dtype):
    m = xn.shape[0]
    return pl.pallas_call(
        functools.partial(_proj_slab_body, row_groups=tm // PROJ_GROUP_ROWS),
        grid=(N_P_TILES, m // tm),
        in_specs=[
            pl.BlockSpec((tm, D_MODEL), lambda j, i: (i, 0)),
            pl.BlockSpec((tm, DK_RET), lambda j, i: (i % rope_period_tiles, 0)),
            pl.BlockSpec((tm, DK_RET), lambda j, i: (i % rope_period_tiles, 0)),
            pl.BlockSpec((D_MODEL, PROJ_TN), lambda j, i: (0, _w_tile(j))),
        ],
        out_specs=pl.BlockSpec((tm, PROJ_TN), lambda j, i: (i, j)),
        out_shape=jax.ShapeDtypeStruct((m, P_COLS), p_dtype),
        scratch_shapes=[pltpu.VMEM((D_MODEL, PROJ_TN), BF16)],
        compiler_params=pltpu.CompilerParams(
            dimension_semantics=("arbitrary", "arbitrary"), vmem_limit_bytes=VMEM_LIMIT
        ),
        name="proj",
    )(xn, cos, sin, w_in)


def _proj_heads_body(xn_ref, w_ref, tok_ref, head_ref, wb_ref, *, row_groups):
    @pl.when(pl.program_id(0) == 0)
    def _cast():
        wb_ref[...] = w_ref[...].astype(BF16)

    for row0, col0, rows, acc in _proj_groups(xn_ref, wb_ref, row_groups):
        for off, ah in _lane_blocks(acc):
            head = (col0 + off) // D_SB
            tok_ref[pl.ds(row0 * H_SB + head, rows, stride=H_SB), :] = ah
            head_ref[head, row0 : row0 + rows, :] = ah.astype(BF16)


def _proj_heads_norm_body(x_ref, gain_ref, w_ref, tok_ref, head_ref, xn_ref, wb_ref, *, row_groups):
    _rmsnorm_body(x_ref, gain_ref, xn_ref)
    _proj_heads_body(xn_ref, w_ref, tok_ref, head_ref, wb_ref, row_groups=row_groups)


def _project_heads_norm(x, gain, w_in, col, *, tm):
    m = x.shape[0]
    return pl.pallas_call(
        functools.partial(_proj_heads_norm_body, row_groups=tm // PROJ_GROUP_ROWS),
        grid=(m // tm,),
        in_specs=[
            pl.BlockSpec((tm, D_MODEL), lambda i: (i, 0)),
            pl.BlockSpec((1, D_MODEL), lambda i: (0, 0)),
            pl.BlockSpec((D_MODEL, W_SB), lambda i: (0, col // W_SB), pipeline_mode=pl.Buffered(1)),
        ],
        out_specs=[
            pl.BlockSpec((tm * H_SB, D_SB), lambda i: (i, 0)),
            pl.BlockSpec((H_SB, tm, D_SB), lambda i: (0, i, 0)),
            pl.BlockSpec((tm, D_MODEL), lambda i: (i, 0)),
        ],
        out_shape=[
            jax.ShapeDtypeStruct((m * H_SB, D_SB), F32),
            jax.ShapeDtypeStruct((H_SB, m, D_SB), BF16),
            jax.ShapeDtypeStruct((m, D_MODEL), BF16),
        ],
        scratch_shapes=[pltpu.VMEM((D_MODEL, W_SB), BF16)],
        compiler_params=pltpu.CompilerParams(dimension_semantics=("arbitrary",), vmem_limit_bytes=VMEM_LIMIT),
        name="proj_heads_norm",
    )(x, gain, w_in)


def _proj_heads_cast_body(xn_ref, w_ref, *rest, row_groups):
    n = (len(rest) - 3) // 2
    src_refs, (tok_ref, head_ref), dst_refs, wb_ref = rest[:n], rest[n : n + 2], rest[n + 2 : 2 * n + 2], rest[-1]
    _proj_heads_body(xn_ref, w_ref, tok_ref, head_ref, wb_ref, row_groups=row_groups)
    for src_ref, dst_ref in zip(src_refs, dst_refs):
        dst_ref[...] = src_ref[...].astype(dst_ref.dtype)


def _project_heads_cast(xn, w_in, col, weights, *, tm):
    m = xn.shape[0]
    steps = m // tm
    band = lambda w: pl.BlockSpec((w.shape[0] // steps, w.shape[1]), lambda i: (i, 0))
    outs = pl.pallas_call(
        functools.partial(_proj_heads_cast_body, row_groups=tm // PROJ_GROUP_ROWS),
        grid=(steps,),
        in_specs=[
            pl.BlockSpec((tm, D_MODEL), lambda i: (i, 0)),
            pl.BlockSpec((D_MODEL, W_SB), lambda i: (0, col // W_SB), pipeline_mode=pl.Buffered(1)),
        ]
        + [band(w) for w in weights],
        out_specs=[
            pl.BlockSpec((tm * H_SB, D_SB), lambda i: (i, 0)),
            pl.BlockSpec((H_SB, tm, D_SB), lambda i: (0, i, 0)),
        ]
        + [band(w) for w in weights],
        out_shape=[
            jax.ShapeDtypeStruct((m * H_SB, D_SB), F32),
            jax.ShapeDtypeStruct((H_SB, m, D_SB), BF16),
        ]
        + [jax.ShapeDtypeStruct(w.shape, BF16) for w in weights],
        scratch_shapes=[pltpu.VMEM((D_MODEL, W_SB), BF16)],
        compiler_params=pltpu.CompilerParams(dimension_semantics=("arbitrary",), vmem_limit_bytes=VMEM_LIMIT),
        name="proj_heads_cast",
    )(xn, w_in, *weights)
    return outs[0], outs[1], outs[2:]


def _project_heads(xn, w_in, col, *, tm):
    m = xn.shape[0]
    return pl.pallas_call(
        functools.partial(_proj_heads_body, row_groups=tm // PROJ_GROUP_ROWS),
        grid=(m // tm,),
        in_specs=[
            pl.BlockSpec((tm, D_MODEL), lambda i: (i, 0)),
            pl.BlockSpec((D_MODEL, W_SB), lambda i: (0, col // W_SB), pipeline_mode=pl.Buffered(1)),
        ],
        out_specs=[
            pl.BlockSpec((tm * H_SB, D_SB), lambda i: (i, 0)),
            pl.BlockSpec((H_SB, tm, D_SB), lambda i: (0, i, 0)),
        ],
        out_shape=[
            jax.ShapeDtypeStruct((m * H_SB, D_SB), F32),
            jax.ShapeDtypeStruct((H_SB, m, D_SB), BF16),
        ],
        scratch_shapes=[pltpu.VMEM((D_MODEL, W_SB), BF16)],
        compiler_params=pltpu.CompilerParams(dimension_semantics=("arbitrary",), vmem_limit_bytes=VMEM_LIMIT),
        name="proj_heads",
    )(xn, w_in)


def _group_norm_gate(o, gain, g):
    mu = jnp.mean(o, axis=-1, keepdims=True)
    d = o - mu
    var = jnp.mean(d * d, axis=-1, keepdims=True)
    return d * lax.rsqrt(var + EPS_GN) * gain * _silu(g)


def _ret_prompt_body(q_ref, k_ref, v_ref, g_ref, gain_ref, decay_ref, cd_ref, kd_ref, gs_ref, out_ref, s_ref):
    c = pl.program_id(1)

    @pl.when(c == 0)
    def _init():
        s_ref[...] = jnp.zeros_like(s_ref)

    heads = range(H_RET)
    q = [q_ref[:, h * DK_RET : (h + 1) * DK_RET] for h in heads]
    k = [k_ref[:, h * DK_RET : (h + 1) * DK_RET] for h in heads]
    v = [v_ref[:, h * DV_RET : (h + 1) * DV_RET] for h in heads]
    s = [s_ref[0, h] for h in heads]
    inner = [lax.dot_general(q[h], k[h], _NT, preferred_element_type=F32) for h in heads]
    cross = [jnp.dot(q[h], s[h].astype(BF16), preferred_element_type=F32) for h in heads]
    for h in heads:
        kdec_t = (k[h].astype(F32) * kd_ref[:, h : h + 1]).T.astype(BF16)
        s_ref[0, h] = gs_ref[h] * s[h] + jnp.dot(kdec_t, v[h], preferred_element_type=F32)
    for h in heads:
        o = jnp.dot((inner[h] * decay_ref[h]).astype(BF16), v[h], preferred_element_type=F32)
        o = o + cross[h] * cd_ref[:, h : h + 1]
        g = g_ref[:, h * DV_RET : (h + 1) * DV_RET].astype(F32)
        gain = gain_ref[:, h * DV_RET : (h + 1) * DV_RET]
        out_ref[:, h * DV_RET : (h + 1) * DV_RET] = _group_norm_gate(o, gain, g).astype(out_ref.dtype)


def _ret_tables(chunk):
    lg = np.log1p(-np.exp2(-5.0 - np.arange(H_RET, dtype=np.float64)))
    idx = np.arange(chunk, dtype=np.float64)
    diff = idx[:, None] - idx[None, :]
    decay = np.where(diff >= 0, np.exp(lg[:, None, None] * np.maximum(diff, 0.0)[None]), 0.0)
    cross = np.exp(lg[:, None] * (idx + 1.0)[None, :]).T
    kdec = np.exp(lg[:, None] * (chunk - 1.0 - idx)[None, :]).T
    sdec = np.broadcast_to(np.exp(lg * chunk)[:, None, None], (H_RET, 1, DV_RET))
    return tuple(np.asarray(a, np.float32) for a in (decay, cross, kdec, sdec))


def _ret_prompt(p, gn_gain, batch, seq):
    nc = seq // RET_CHUNK
    decay, cross, kdec, sdec = _ret_tables(RET_CHUNK)
    row = lambda b, c: b * nc + c
    const2 = lambda b, c: (0, 0)
    const3 = lambda b, c: (0, 0, 0)
    return pl.pallas_call(
        _ret_prompt_body,
        grid=(batch, nc),
        in_specs=[
            pl.BlockSpec((RET_CHUNK, W_QK), lambda b, c: (row(b, c), COL_Q // W_QK)),
            pl.BlockSpec((RET_CHUNK, W_QK), lambda b, c: (row(b, c), COL_K // W_QK)),
            pl.BlockSpec((RET_CHUNK, W_V), lambda b, c: (row(b, c), COL_V // W_V)),
            pl.BlockSpec((RET_CHUNK, W_V), lambda b, c: (row(b, c), COL_G // W_V)),
            pl.BlockSpec((1, W_V), const2),
            pl.BlockSpec((H_RET, RET_CHUNK, RET_CHUNK), const3),
            pl.BlockSpec((RET_CHUNK, H_RET), const2),
            pl.BlockSpec((RET_CHUNK, H_RET), const2),
            pl.BlockSpec((H_RET, 1, DV_RET), const3),
        ],
        out_specs=[
            pl.BlockSpec((RET_CHUNK, W_V), lambda b, c: (row(b, c), 0)),
            pl.BlockSpec((1, H_RET, DK_RET, DV_RET), lambda b, c: (b, 0, 0, 0)),
        ],
        out_shape=[
            jax.ShapeDtypeStruct((batch * seq, W_V), BF16),
            jax.ShapeDtypeStruct((batch, H_RET, DK_RET, DV_RET), F32),
        ],
        compiler_params=pltpu.CompilerParams(
            dimension_semantics=("arbitrary", "arbitrary"), vmem_limit_bytes=VMEM_LIMIT
        ),
        name="ret_prompt",
    )(p, p, p, p, gn_gain, decay, cross, kdec, sdec)


RS_SEQS = 32


RS_SLOTS = 3


def _ret_sample_body(
    q_ref, k_ref, v_ref, g_ref, state_ref, gain_ref, decay_ref, cd_ref, kd_ref, gs_ref, out_ref, snew_ref, sbuf_ref, sem_ref
):
    n_seq = sbuf_ref.shape[1]
    rows = q_ref.shape[0]
    t_len = rows // n_seq
    n_heads = pl.num_programs(1)
    step = pl.program_id(0) * n_heads + pl.program_id(1)
    n_steps = pl.num_programs(0) * n_heads

    def state_copy(t):
        block = state_ref.at[pl.ds((t // n_heads) * n_seq, n_seq), t % n_heads]
        slot = t % RS_SLOTS
        return pltpu.make_async_copy(block, sbuf_ref.at[slot], sem_ref.at[slot])

    @pl.when(step == 0)
    def _prime():
        for t in range(RS_SLOTS - 1):
            state_copy(t).start()

    @pl.when(step + RS_SLOTS - 1 < n_steps)
    def _refill():
        state_copy(step + RS_SLOTS - 1).start()

    state_copy(step).wait()
    q = q_ref[...]
    k = k_ref[...]
    v = v_ref[...].astype(BF16)
    s = sbuf_ref[step % RS_SLOTS]
    inner = lax.dot_general(q.astype(BF16), k.astype(BF16), _NT, preferred_element_type=F32) * decay_ref[...]
    o = jnp.dot(inner.astype(BF16), v, preferred_element_type=F32)
    row_seq = lax.broadcasted_iota(jnp.int32, (rows, DK_RET), 0) // t_len
    q_exp = jnp.concatenate([jnp.where(row_seq == b, q, 0.0) for b in range(n_seq)], axis=1).astype(BF16)
    s_rows = s.reshape(n_seq * DK_RET, DV_RET).astype(BF16)
    o = o + jnp.dot(q_exp, s_rows, preferred_element_type=F32) * cd_ref[...]
    kdec_t = (k * kd_ref[...]).T
    col_seq = lax.broadcasted_iota(jnp.int32, (DK_RET, rows), 1) // t_len
    k_exp_t = jnp.concatenate([jnp.where(col_seq == b, kdec_t, 0.0) for b in range(n_seq)], axis=0).astype(BF16)
    upd = jnp.dot(k_exp_t, v, preferred_element_type=F32)
    snew_ref[...] = gs_ref[...] * s + upd.reshape(n_seq, DK_RET, DV_RET)
    out_ref[...] = _group_norm_gate(o, gain_ref[...], g_ref[...])


def _ret_sample(p, state, gn_gain, batch, t_len):
    decay, cross, kdec, sdec = _ret_tables(t_len)
    rows = RS_SEQS * t_len
    decay_bd = np.stack([np.kron(np.eye(RS_SEQS, dtype=np.float32), decay[h]) for h in range(H_RET)])
    cross_rows = np.tile(cross.T, (1, RS_SEQS))[:, :, None]
    kdec_rows = np.tile(kdec.T, (1, RS_SEQS))[:, :, None]
    qk_blk = lambda col: pl.BlockSpec((rows, DK_RET), lambda i, h: (i, col // DK_RET + h))
    v_blk = lambda col: pl.BlockSpec((rows, DV_RET), lambda i, h: (i, col // DV_RET + h))
    state_blk = pl.BlockSpec((RS_SEQS, None, DK_RET, DV_RET), lambda i, h: (i, h, 0, 0))
    return pl.pallas_call(
        _ret_sample_body,
        grid=(batch // RS_SEQS, H_RET),
        in_specs=[
            qk_blk(COL_Q),
            qk_blk(COL_K),
            v_blk(COL_V),
            v_blk(COL_G),
            pl.BlockSpec(memory_space=pl.ANY),
            pl.BlockSpec((1, DV_RET), lambda i, h: (0, h)),
            pl.BlockSpec((None, rows, rows), lambda i, h: (h, 0, 0)),
            pl.BlockSpec((None, rows, 1), lambda i, h: (h, 0, 0)),
            pl.BlockSpec((None, rows, 1), lambda i, h: (h, 0, 0)),
            pl.BlockSpec((None, 1, DV_RET), lambda i, h: (h, 0, 0)),
        ],
        out_specs=[pl.BlockSpec((rows, DV_RET), lambda i, h: (i, h)), state_blk],
        out_shape=[
            jax.ShapeDtypeStruct((batch * t_len, W_V), F32),
            jax.ShapeDtypeStruct((batch, H_RET, DK_RET, DV_RET), F32),
        ],
        scratch_shapes=[
            pltpu.VMEM((RS_SLOTS, RS_SEQS, DK_RET, DV_RET), F32),
            pltpu.SemaphoreType.DMA((RS_SLOTS,)),
        ],
        compiler_params=pltpu.CompilerParams(
            dimension_semantics=("arbitrary", "arbitrary"), vmem_limit_bytes=VMEM_LIMIT
        ),
        name="ret_sample",
    )(p, p, p, p, state, gn_gain, decay_bd, cross_rows, kdec_rows, sdec)


def _sb_weights(qk2, tri, bias2, carry, mask):
    drop, log2_beta = _sb_drop(qk2, bias2, mask)
    return _sb_finish(drop, log2_beta, _sb_after(drop, tri), carry, mask)


def _sb_drop(qk2, bias2, mask):
    z2 = qk2 + bias2
    drop = jnp.maximum(z2, 0.0) + jnp.log2(1.0 + jnp.exp2(-jnp.abs(z2)))
    log2_beta = z2 - drop
    if mask is not None:
        drop = jnp.where(mask, drop, 0.0)
    return drop, log2_beta


def _sb_after(drop, tri):
    return jnp.dot(drop.astype(BF16), tri, preferred_element_type=F32)


def _sb_finish(drop, log2_beta, after, carry, mask):
    a = jnp.exp2(log2_beta - after - carry)
    if mask is not None:
        a = jnp.where(mask, a, 0.0)
    return a, carry + after[:, 0:1] + drop[:, 0:1]


def _tri_np(n):
    return np.tril(np.ones((n, n), np.float32), -1)


SBP_T = 256
SBP_HEADS = 8


def _sb_prompt_body(bias_ref, q_ref, k_ref, v_ref, sg_ref, tri_ref, out_ref):
    hg = pl.program_id(1)
    i = pl.program_id(2)
    heads = range(SBP_HEADS)
    bias2 = [bias_ref[hg * SBP_HEADS + g] for g in heads]
    tri = tri_ref[...]
    q = [q_ref[:, g * D_SB : (g + 1) * D_SB] for g in heads]
    r = lax.broadcasted_iota(jnp.int32, (SBP_T, SBP_T), 0)
    c = lax.broadcasted_iota(jnp.int32, (SBP_T, SBP_T), 1)

    def tile(kb, state, mask):
        off = pl.multiple_of(kb * SBP_T, SBP_T)
        qk2 = [lax.dot_general(q[g], k_ref[g, pl.ds(off, SBP_T), :], _NT, preferred_element_type=F32) for g in heads]
        dl = [_sb_drop(qk2[g], bias2[g], mask) for g in heads]
        after = [_sb_after(dl[g][0], tri) for g in heads]
        new_state = []
        for g in heads:
            carry, acc = state[g]
            a, carry = _sb_finish(dl[g][0], dl[g][1], after[g], carry, mask)
            acc = acc + jnp.dot(a.astype(BF16), v_ref[g, pl.ds(off, SBP_T), :], preferred_element_type=F32)
            new_state.append((carry, acc))
        return tuple(new_state)

    state = tuple((jnp.zeros((SBP_T, 1), F32), jnp.zeros((SBP_T, D_SB), F32)) for _ in heads)
    state = tile(i, state, c < r)
    state = lax.fori_loop(0, i, lambda n, st: tile(i - 1 - n, st, None), state)
    acc = jnp.concatenate([st[1] for st in state], axis=1)
    out_ref[...] = (acc * _silu(sg_ref[...].astype(F32))).astype(out_ref.dtype)


def _sb_prompt(p, sk_heads, sv_heads, sb_bias, batch, seq):
    nq = seq // SBP_T
    width = SBP_HEADS * D_SB
    return pl.pallas_call(
        _sb_prompt_body,
        grid=(batch, H_SB // SBP_HEADS, nq),
        in_specs=[
            pl.BlockSpec(memory_space=pltpu.SMEM),
            pl.BlockSpec((SBP_T, width), lambda b, h, i: (b * nq + i, COL_SQ // width + h)),
            pl.BlockSpec((SBP_HEADS, seq, D_SB), lambda b, h, i: (h, b, 0)),
            pl.BlockSpec((SBP_HEADS, seq, D_SB), lambda b, h, i: (h, b, 0)),
            pl.BlockSpec((SBP_T, width), lambda b, h, i: (b * nq + i, COL_SG // width + h)),
            pl.BlockSpec((SBP_T, SBP_T), lambda b, h, i: (0, 0)),
        ],
        out_specs=pl.BlockSpec((SBP_T, width), lambda b, h, i: (b * nq + i, h)),
        out_shape=jax.ShapeDtypeStruct((batch * seq, W_SB), BF16),
        compiler_params=pltpu.CompilerParams(
            dimension_semantics=("arbitrary", "arbitrary", "arbitrary"), vmem_limit_bytes=VMEM_LIMIT
        ),
        name="sb_prompt",
    )(sb_bias * LOG2E, p, sk_heads, sv_heads, p, jnp.asarray(_tri_np(SBP_T), BF16))


SBS_CHUNK_PAGES = 4


SBS_SLOTS = 4


def _sb_sample_body(
    pt_ref, bias_ref, tri_ref, q_ref, kn_ref, vn_ref, sg_ref, kpool_ref, vpool_ref,
    out_ref, kbuf_ref, vbuf_ref, sem_ref, kd0_ref, kd1_ref, vd0_ref, vd1_ref, knd_ref, vnd_ref, kstage_ref, vstage_ref,
    *, chunks_per_seq,
):
    b = pl.program_id(0)
    n_seq = pl.num_programs(0)
    t_len = q_ref.shape[0]
    rows = H_SB * t_len
    half = H_SB // 2
    ppage = kbuf_ref.shape[2] // half
    lookahead = SBS_SLOTS - 1
    kd_refs, vd_refs = (kd0_ref, kd1_ref), (vd0_ref, vd1_ref)

    def chunk_copies(n, slot):
        copies = []
        for pg in range(SBS_CHUNK_PAGES):
            page_id = pt_ref[n * SBS_CHUNK_PAGES + pg]
            copies.append(pltpu.make_async_copy(kpool_ref.at[page_id], kbuf_ref.at[slot, pg], sem_ref.at[0, slot]))
            copies.append(pltpu.make_async_copy(vpool_ref.at[page_id], vbuf_ref.at[slot, pg], sem_ref.at[1, slot]))
        return copies

    def densify(src_ref, dst_ref, row0, n_pseudo):
        for j in range(half):
            dst_ref[row0 : row0 + n_pseudo, j * D_SB : (j + 1) * D_SB] = src_ref[
                pl.ds(j, n_pseudo, stride=half), :
            ].astype(dst_ref.dtype)

    @pl.when(b == 0)
    def _prime():
        for n in range(lookahead):
            for idx, cp in enumerate(chunk_copies(n, n % SBS_SLOTS)):
                cp.start(priority=idx % 2)

    q = q_ref[...]
    zeros = jnp.zeros((t_len, D_SB), F32)
    qbd = jnp.concatenate(
        [
            jnp.concatenate([q[:, h * D_SB : (h + 1) * D_SB] if j == h % half else zeros for j in range(half)], axis=1)
            for h in range(H_SB)
        ],
        axis=0,
    ).astype(BF16)
    bias2 = bias_ref[...]
    tri = tri_ref[...]

    def parity_ok(n_pseudo):
        r = lax.broadcasted_iota(jnp.int32, (rows, n_pseudo), 0)
        p = lax.broadcasted_iota(jnp.int32, (rows, n_pseudo), 1)
        return (p % 2) == (r // (half * t_len)), r, p

    def sweep(k, v, n_pages, carry, acc, mask):
        qk2 = lax.dot_general(qbd, k, _NT, preferred_element_type=F32)
        drop, log2_beta = _sb_drop(qk2, bias2, mask)
        pages = [slice(pg * ppage, (pg + 1) * ppage) for pg in range(n_pages)]
        stacked = jnp.concatenate([drop[:, cols] for cols in pages], axis=0)
        after_stacked = _sb_after(stacked, tri)
        afters = [after_stacked[pg * rows : (pg + 1) * rows] for pg in range(n_pages)]
        shifted = [None] * n_pages
        for pg in reversed(range(n_pages)):
            shifted[pg] = afters[pg] + carry
            carry = carry + afters[pg][:, 0:1] + drop[:, pg * ppage : pg * ppage + 1]
        a = jnp.where(mask, jnp.exp2(log2_beta - jnp.concatenate(shifted, axis=1)), 0.0)
        return carry, acc + jnp.dot(a.astype(BF16), v, preferred_element_type=F32)

    for src_ref, stage_ref, dst_ref in ((kn_ref, kstage_ref, knd_ref), (vn_ref, vstage_ref, vnd_ref)):
        stage_ref[...] = jnp.zeros_like(stage_ref)
        densify(src_ref, stage_ref, 0, 2 * t_len)
        dst_ref[...] = stage_ref[...].astype(BF16)
    ok, r, p = parity_ok(ppage)
    carry, acc = sweep(
        knd_ref[...], vnd_ref[...], 1,
        jnp.zeros((rows, 1), F32), jnp.zeros((rows, half * D_SB), F32), ok & ((p // 2) < (r % t_len)),
    )

    chunk_ok, _, _ = parity_ok(SBS_CHUNK_PAGES * ppage)
    for c in range(chunks_per_seq):
        n = b * chunks_per_seq + c
        slot = c % SBS_SLOTS
        for cp in chunk_copies(n, slot):
            cp.wait()
        for idx, cp in enumerate(chunk_copies(n + lookahead, (c + lookahead) % SBS_SLOTS)):
            cp.start(priority=idx % 2)
        kd_ref, vd_ref = kd_refs[c % 2], vd_refs[c % 2]
        for pg in range(SBS_CHUNK_PAGES):
            densify(kbuf_ref.at[slot, pg], kd_ref, pg * ppage, ppage)
            densify(vbuf_ref.at[slot, pg], vd_ref, pg * ppage, ppage)
        carry, acc = sweep(kd_ref[...], vd_ref[...], SBS_CHUNK_PAGES, carry, acc, chunk_ok)

    o = jnp.concatenate(
        [acc[h * t_len : (h + 1) * t_len, (h % half) * D_SB : (h % half + 1) * D_SB] for h in range(H_SB)], axis=1
    )
    out_ref[...] = o * _silu(sg_ref[...])

    @pl.when(b == n_seq - 1)
    def _drain():
        for i in range(lookahead):
            for cp in chunk_copies(n_seq * chunks_per_seq + i, i % SBS_SLOTS):
                cp.wait()


def _sb_sample(page_table, p, sk, sv, pool_k, pool_v, sb_bias, batch, t_len):
    n_pages = page_table.shape[1]
    chunks_per_seq = n_pages // SBS_CHUNK_PAGES
    assert n_pages % SBS_CHUNK_PAGES == 0 and chunks_per_seq % SBS_SLOTS == 0
    page_rows = pool_k.shape[1]
    rows = H_SB * t_len
    half = H_SB // 2
    ppage = page_rows // half
    width = half * D_SB
    bias_rows = jnp.repeat(sb_bias * LOG2E, t_len)[:, None]
    tri = jnp.asarray(_tri_np(ppage), BF16)
    fetch = page_table.reshape(batch, chunks_per_seq, SBS_CHUNK_PAGES)[:, ::-1, :].reshape(-1)
    fetch = jnp.concatenate([fetch, fetch[: (SBS_SLOTS - 1) * SBS_CHUNK_PAGES]])

    const2 = lambda b, pt: (0, 0)
    grid_spec = pltpu.PrefetchScalarGridSpec(
        num_scalar_prefetch=1,
        grid=(batch,),
        in_specs=[
            pl.BlockSpec((rows, 1), const2),
            pl.BlockSpec((ppage, ppage), const2),
            pl.BlockSpec((t_len, W_SB), lambda b, pt: (b, COL_SQ // W_SB)),
            pl.BlockSpec((rows, D_SB), lambda b, pt: (b, 0)),
            pl.BlockSpec((rows, D_SB), lambda b, pt: (b, 0)),
            pl.BlockSpec((t_len, W_SB), lambda b, pt: (b, COL_SG // W_SB)),
            pl.BlockSpec(memory_space=pl.ANY),
            pl.BlockSpec(memory_space=pl.ANY),
        ],
        out_specs=pl.BlockSpec((t_len, W_SB), lambda b, pt: (b, 0)),
        scratch_shapes=[
            pltpu.VMEM((SBS_SLOTS, SBS_CHUNK_PAGES, page_rows, D_SB), F32),
            pltpu.VMEM((SBS_SLOTS, SBS_CHUNK_PAGES, page_rows, D_SB), F32),
            pltpu.SemaphoreType.DMA((2, SBS_SLOTS)),
            pltpu.VMEM((SBS_CHUNK_PAGES * ppage, width), BF16),
            pltpu.VMEM((SBS_CHUNK_PAGES * ppage, width), BF16),
            pltpu.VMEM((SBS_CHUNK_PAGES * ppage, width), BF16),
            pltpu.VMEM((SBS_CHUNK_PAGES * ppage, width), BF16),
            pltpu.VMEM((ppage, width), BF16),
            pltpu.VMEM((ppage, width), BF16),
            pltpu.VMEM((ppage, width), F32),
            pltpu.VMEM((ppage, width), F32),
        ],
    )
    return pl.pallas_call(
        functools.partial(_sb_sample_body, chunks_per_seq=chunks_per_seq),
        grid_spec=grid_spec,
        out_shape=jax.ShapeDtypeStruct((batch * t_len, W_SB), F32),
        compiler_params=pltpu.CompilerParams(dimension_semantics=("arbitrary",), vmem_limit_bytes=VMEM_LIMIT),
        name="sb_sample",
    )(fetch, bias_rows, tri, p, sk, sv, p, pool_k, pool_v)


def _merge_body(x_ref, rb_ref, sb_ref, gr_ref, gs_ref, wr_ref, ws_ref, wo_ref, fg_ref, y_ref):
    m = gr_ref[...].astype(F32) * jnp.dot(rb_ref[...].astype(BF16), wr_ref[...], preferred_element_type=F32)
    m = m + gs_ref[...].astype(F32) * jnp.dot(sb_ref[...].astype(BF16), ws_ref[...], preferred_element_type=F32)
    hid = x_ref[...] + jnp.dot(m.astype(BF16), wo_ref[...], preferred_element_type=F32)
    r = lax.rsqrt(jnp.mean(hid * hid, axis=-1, keepdims=True) + EPS_RMS)
    y_ref[...] = hid * r * fg_ref[...]


def _merge(x, ret_branch, sb_branch, p, w_ret, w_sb, w_out, final_gain, *, tm):
    m = x.shape[0]
    resident = functools.partial(pl.BlockSpec, index_map=lambda i: (0, 0), pipeline_mode=pl.Buffered(1))
    return pl.pallas_call(
        _merge_body,
        grid=(m // tm,),
        in_specs=[
            pl.BlockSpec((tm, D_MODEL), lambda i: (i, 0)),
            pl.BlockSpec((tm, W_V), lambda i: (i, 0)),
            pl.BlockSpec((tm, W_SB), lambda i: (i, 0)),
            pl.BlockSpec((tm, D_MODEL), lambda i: (i, COL_GR // D_MODEL)),
            pl.BlockSpec((tm, D_MODEL), lambda i: (i, COL_GS // D_MODEL)),
            resident((W_V, D_MODEL)),
            resident((W_SB, D_MODEL)),
            resident((D_MODEL, D_MODEL)),
            pl.BlockSpec((1, D_MODEL), lambda i: (0, 0)),
        ],
        out_specs=pl.BlockSpec((tm, D_MODEL), lambda i: (i, 0)),
        out_shape=jax.ShapeDtypeStruct((m, D_MODEL), F32),
        compiler_params=pltpu.CompilerParams(dimension_semantics=("arbitrary",), vmem_limit_bytes=VMEM_LIMIT),
        name="merge",
    )(x, ret_branch, sb_branch, p, p, w_ret, w_sb, w_out, final_gain)


def _rope_tables(pos):
    half = DK_RET // 2
    inv = ROPE_BASE ** (-np.arange(half, dtype=np.float64) / half)
    ang = np.asarray(pos, np.float64)[:, None] * inv[None, :]
    cos, sin = np.cos(ang), np.sin(ang)
    cos_full = np.concatenate([cos, cos], axis=-1)
    sin_signed = np.concatenate([-sin, sin], axis=-1)
    return cos_full.astype(np.float32), sin_signed.astype(np.float32)


def _layer(xp, xs, state, pool_k, pool_v, page_table, norm_gain, w_in, sb_bias, gn_gain, w_ret, w_sb, w_out, final_gain):
    batch, seq, _ = xp.shape
    dec_batch, t_len, _ = xs.shape
    n_pool, page = pool_k.shape[:2]
    past_len = page_table.shape[1] * page

    gain2 = norm_gain[None, :]
    gn_gain2 = gn_gain[None, :]
    final_gain2 = final_gain[None, :]

    x2p = xp.reshape(batch * seq, D_MODEL)
    x2s = xs.reshape(dec_batch * t_len, D_MODEL)
    tm_slab, tm_heads = 1024, 512
    cos_p, sin_p = _rope_tables(np.arange(seq))
    cos_s, sin_s = _rope_tables(past_len + np.arange(t_len))
    cos_s, sin_s = np.tile(cos_s, (tm_slab // t_len, 1)), np.tile(sin_s, (tm_slab // t_len, 1))

    skp, skp_heads, xnp = _project_heads_norm(x2p, gain2, w_in, W_IN_COL_SK, tm=tm_heads)
    sks, _, xns = _project_heads_norm(x2s, gain2, w_in, W_IN_COL_SK, tm=tm_heads)
    pp = _project_slab(xnp, cos_p, sin_p, w_in, tm=tm_slab, rope_period_tiles=seq // tm_slab, p_dtype=BF16)
    ps = _project_slab(xns, cos_s, sin_s, w_in, tm=tm_slab, rope_period_tiles=1, p_dtype=F32)
    svp, svp_heads, (w_ret_b, w_sb_b, w_out_b) = _project_heads_cast(
        xnp, w_in, W_IN_COL_SV, (w_ret, w_sb, w_out), tm=tm_slab
    )
    svs, _ = _project_heads(xns, w_in, W_IN_COL_SV, tm=tm_slab)

    rbp, state_p = _ret_prompt(pp, gn_gain2, batch, seq)
    rbs, state_s = _ret_sample(ps, state, gn_gain2, dec_batch, t_len)

    sbp = _sb_prompt(pp, skp_heads, svp_heads, sb_bias, batch, seq)
    sbs = _sb_sample(
        page_table, ps, sks, svs, pool_k.reshape(n_pool, page * H_SB, D_SB), pool_v.reshape(n_pool, page * H_SB, D_SB),
        sb_bias, dec_batch, t_len,
    )

    yp = _merge(x2p, rbp, sbp, pp, w_ret_b, w_sb_b, w_out_b, final_gain2, tm=256)
    ys = _merge(x2s, rbs, sbs, ps, w_ret_b, w_sb_b, w_out_b, final_gain2, tm=256)

    return (
        yp.reshape(batch, seq, D_MODEL),
        ys.reshape(dec_batch, t_len, D_MODEL),
        state_p,
        skp.reshape(batch, seq, H_SB, D_SB),
        svp.reshape(batch, seq, H_SB, D_SB),
        state_s,
        sks.reshape(dec_batch, t_len, H_SB, D_SB),
        svs.reshape(dec_batch, t_len, H_SB, D_SB),
    )


def kernel(x_prompt, x_sample, state_ret, cache_sb_k, cache_sb_v, page_table, norm_gain, w_in, sb_bias, ret_gn_gain, w_ret_proj, w_sb_proj, w_out, final_norm_gain):
    depth = w_in.shape[0]
    assert depth == 1, "single-layer trunk"
    outs = _layer(
        x_prompt, x_sample, state_ret[0], cache_sb_k[0], cache_sb_v[0], page_table, norm_gain[0], w_in[0],
        sb_bias[0], ret_gn_gain[0], w_ret_proj[0], w_sb_proj[0], w_out[0], final_norm_gain,
    )
    yp, ys, state_p, skp, svp, state_s, sks, svs = outs
    return (yp, ys, state_p[None], skp[None], svp[None], state_s[None], sks[None], svs[None])
```
